```python
import jax, jax.numpy as jnp
from jax import lax
import numpy as np

D_MODEL = 1024
BATCH = 8
SEQ = 4096
DEPTH = 4

HEAD_DIM = 64
A_Q_HEADS = 8
A_KV_HEADS = 2
B_GROUPS = ((128, 1), (512, 4), (2048, 16))
B_HEADS = 4
N_B = len(B_GROUPS)
D_FF = 4 * D_MODEL
GRID_W = 64
ROPE_THETA = 10000.0
Q_BLOCK = 128
EPS = 1e-6

A_Q_W = A_Q_HEADS * HEAD_DIM
A_KV_W = A_KV_HEADS * HEAD_DIM
B_W = B_HEADS * HEAD_DIM
IN_SIZES = [A_Q_W, A_KV_W, A_KV_W] + [B_W] * (3 * N_B) + [D_MODEL, D_MODEL]
IN_W = sum(IN_SIZES)
IN_SPLITS = [int(v) for v in np.cumsum(IN_SIZES)[:-1]]

kernel_name = "hybrid_gqa_axial_dilated_gated_encoder"


def rms(x, g):
    x32 = x.astype(jnp.float32)
    y = x32 * lax.rsqrt(jnp.mean(x32 * x32, axis=-1, keepdims=True) + EPS)
    return (y * g.astype(jnp.float32)).astype(x.dtype)


def rope_cos_sin(pos, dim):
    inv = ROPE_THETA ** (-jnp.arange(0, dim, 2, dtype=jnp.float32) / dim)
    ang = pos.astype(jnp.float32)[:, None] * inv[None, :]
    return jnp.cos(ang), jnp.sin(ang)


def apply_rope(x, cos, sin):
    x32 = x.astype(jnp.float32)
    x1, x2 = jnp.split(x32, 2, axis=-1)
    c = cos[None, :, None, :]
    s = sin[None, :, None, :]
    return jnp.concatenate([x1 * c - x2 * s, x2 * c + x1 * s], axis=-1).astype(x.dtype)


def axial_rope(x, row_cs, col_cs):
    half = x.shape[-1] // 2
    return jnp.concatenate([apply_rope(x[..., :half], *row_cs),
                            apply_rope(x[..., half:], *col_cs)], axis=-1)


def gqa_attention(q, k, v):
    B, S, Hq, hd = q.shape
    Hkv = k.shape[2]
    rep = Hq // Hkv
    q = q.reshape(B, S, Hkv, rep, hd)
    scale = hd ** -0.5

    def block(i):
        qi = lax.dynamic_slice_in_dim(q, i * Q_BLOCK, Q_BLOCK, axis=1)
        s = jnp.einsum('bqgrd,bkgd->bgrqk', qi, k, preferred_element_type=jnp.float32) * scale
        p = jax.nn.softmax(s, axis=-1).astype(v.dtype)
        return jnp.einsum('bgrqk,bkgd->bqgrd', p, v)

    o = lax.map(block, jnp.arange(S // Q_BLOCK))
    return jnp.moveaxis(o, 0, 1).reshape(B, S, Hq * hd)


def dilated_attention(qs, ks, vs):
    B, S, H, hd = qs[0].shape
    scale = hd ** -0.5
    offs = []
    for (w, d) in B_GROUPS:
        kk = (w // 2) // d
        offs.append(jnp.asarray(np.arange(-kk, kk + 1) * d, dtype=jnp.int32))

    def block(i):
        t = i * Q_BLOCK + jnp.arange(Q_BLOCK, dtype=jnp.int32)
        outs, lses = [], []
        for q, k, v, off in zip(qs, ks, vs, offs):
            idx = t[:, None] + off[None, :]
            valid = (idx >= 0) & (idx < S)
            idx = jnp.clip(idx, 0, S - 1)
            qi = lax.dynamic_slice_in_dim(q, i * Q_BLOCK, Q_BLOCK, axis=1)
            kg = k[:, idx]
            vg = v[:, idx]
            s = jnp.einsum('bqhd,bqkhd->bhqk', qi, kg, preferred_element_type=jnp.float32) * scale
            s = jnp.where(valid[None, None], s, -jnp.inf)
            lse = jax.nn.logsumexp(s, axis=-1)
            p = jnp.exp(s - lse[..., None]).astype(vg.dtype)
            outs.append(jnp.einsum('bhqk,bqkhd->bqhd', p, vg))
            lses.append(lse)
        wts = jax.nn.softmax(jnp.stack(lses), axis=0)
        wts = jnp.transpose(wts, (0, 1, 3, 2))[..., None]
        return jnp.sum(wts.astype(outs[0].dtype) * jnp.stack(outs), axis=0)

    o = lax.map(block, jnp.arange(S // Q_BLOCK))
    return jnp.moveaxis(o, 0, 1).reshape(B, S, H * hd)


def setup_inputs(seed: int = 0) -> dict:
    key = jax.random.key(seed)
    ks = jax.random.split(key, 16)
    f = jnp.float32
    nrm = lambda k, shape, fan: jax.random.normal(k, shape, f) * (fan ** -0.5)
    return {
        "x": jax.random.normal(ks[0], (BATCH, SEQ, D_MODEL), f),
        "c": jax.random.normal(ks[1], (BATCH, D_MODEL), f),
        "w_ada": nrm(ks[2], (DEPTH, D_MODEL, 6 * D_MODEL), D_MODEL),
        "b_ada": 0.02 * jax.random.normal(ks[3], (DEPTH, 6 * D_MODEL), f),
        "g_mix": 1.0 + 0.05 * jax.random.normal(ks[4], (DEPTH, D_MODEL), f),
        "g_mlp": 1.0 + 0.05 * jax.random.normal(ks[5], (DEPTH, D_MODEL), f),
        "w_in": nrm(ks[6], (DEPTH, D_MODEL, IN_W), D_MODEL),
        "q_norm_a": 1.0 + 0.05 * jax.random.normal(ks[7], (DEPTH, HEAD_DIM), f),
        "k_norm_a": 1.0 + 0.05 * jax.random.normal(ks[8], (DEPTH, HEAD_DIM), f),
        "q_norm_b": 1.0 + 0.05 * jax.random.normal(ks[9], (DEPTH, N_B, HEAD_DIM), f),
        "k_norm_b": 1.0 + 0.05 * jax.random.normal(ks[10], (DEPTH, N_B, HEAD_DIM), f),
        "w_branch_a": nrm(ks[11], (DEPTH, A_Q_W, D_MODEL), A_Q_W),
        "w_branch_b": nrm(ks[12], (DEPTH, B_W, D_MODEL), B_W),
        "w_out": nrm(ks[13], (DEPTH, D_MODEL, D_MODEL), D_MODEL),
        "w_ff1": nrm(ks[14], (DEPTH, D_MODEL, D_FF), D_MODEL),
        "w_ff2": nrm(ks[15], (DEPTH, D_FF, D_MODEL), D_FF),
    }


def reference(x, c, w_ada, b_ada, g_mix, g_mlp, w_in, q_norm_a, k_norm_a, q_norm_b, k_norm_b,
              w_branch_a, w_branch_b, w_out, w_ff1, w_ff2):
    B, S, D = x.shape
    rows = S // GRID_W
    row_idx = jnp.broadcast_to(jnp.arange(rows)[:, None], (rows, GRID_W)).reshape(-1)
    col_idx = jnp.broadcast_to(jnp.arange(GRID_W)[None, :], (rows, GRID_W)).reshape(-1)
    row_cs = rope_cos_sin(row_idx, HEAD_DIM // 2)
    col_cs = rope_cos_sin(col_idx, HEAD_DIM // 2)
    seq_cs = rope_cos_sin(jnp.arange(S), HEAD_DIM)
    c_act = jax.nn.silu(c)

    for l in range(DEPTH):
        ada = c_act @ w_ada[l] + b_ada[l]
        sh1, sc1, gt1, sh2, sc2, gt2 = [a[:, None, :] for a in jnp.split(ada, 6, axis=-1)]

        h = rms(x, g_mix[l]) * (1.0 + sc1) + sh1
        parts = jnp.split(h @ w_in[l], IN_SPLITS, axis=-1)
        qa = parts[0].reshape(B, S, A_Q_HEADS, HEAD_DIM)
        ka = parts[1].reshape(B, S, A_KV_HEADS, HEAD_DIM)
        va = parts[2].reshape(B, S, A_KV_HEADS, HEAD_DIM)
        qa = axial_rope(rms(qa, q_norm_a[l]), row_cs, col_cs)
        ka = axial_rope(rms(ka, k_norm_a[l]), row_cs, col_cs)
        y_a = gqa_attention(qa, ka, va) @ w_branch_a[l]

        qs, kss, vs = [], [], []
        for g in range(N_B):
            qb = parts[3 + 3 * g].reshape(B, S, B_HEADS, HEAD_DIM)
            kb = parts[4 + 3 * g].reshape(B, S, B_HEADS, HEAD_DIM)
            vb = parts[5 + 3 * g].reshape(B, S, B_HEADS, HEAD_DIM)
            qs.append(apply_rope(rms(qb, q_norm_b[l, g]), *seq_cs))
            kss.append(apply_rope(rms(kb, k_norm_b[l, g]), *seq_cs))
            vs.append(vb)
        y_b = dilated_attention(qs, kss, vs) @ w_branch_b[l]

        gate_a = jax.nn.sigmoid(parts[3 + 3 * N_B])
        gate_b = jax.nn.sigmoid(parts[4 + 3 * N_B])
        mixed = gate_a * y_a + gate_b * y_b
        x = x + gt1 * (mixed @ w_out[l])

        h2 = rms(x, g_mlp[l]) * (1.0 + sc2) + sh2
        x = x + gt2 * (jnp.square(jax.nn.relu(h2 @ w_ff1[l])) @ w_ff2[l])

    return x
```

```python
import functools

import jax
import jax.numpy as jnp
import numpy as np
from jax import lax
from jax.experimental import pallas as pl
from jax.experimental.pallas import tpu as pltpu

D_MODEL = 1024
HEAD_DIM = 64
A_Q_HEADS = 8
A_KV_HEADS = 2
B_GROUPS = ((128, 1), (512, 4), (2048, 16))
B_HEADS = 4
N_B = len(B_GROUPS)
D_FF = 4 * D_MODEL
GRID_W = 64
ROPE_THETA = 10000.0
EPS = 1e-6

A_Q_W = A_Q_HEADS * HEAD_DIM
A_KV_W = A_KV_HEADS * HEAD_DIM
B_W = B_HEADS * HEAD_DIM

V7X_LANES = 128
V7X_VMEM_BYTES = 64 * 1024 * 1024
VMEM_LIMIT = 52 * 1024 * 1024

BF16 = jnp.bfloat16
F32 = jnp.float32

TM_IN = 256
TM_MERGE = 512
TM_MLP = 512
FF_CHUNK = 1024
TQ_B = 128
KW_B = 256
NEG_BIG = -1e30

PERM_AXIAL = np.concatenate([np.arange(16, 32), np.arange(0, 16), np.arange(48, 64), np.arange(32, 48)])
PERM_SEQ = np.concatenate([np.arange(32, 64), np.arange(0, 32)])

T_SEGS = (("qa", A_Q_W), ("ka", A_KV_W), ("va", A_KV_W),
          ("qb0", B_W), ("kb0", B_W), ("qb1", B_W), ("kb1", B_W), ("qb2", B_W), ("kb2", B_W))
T_ROWS = sum(r for _, r in T_SEGS)
N_COLS = N_B * B_W + 2 * D_MODEL
N_TABS = 16


def _cparams(sem):
    return pltpu.CompilerParams(dimension_semantics=sem, vmem_limit_bytes=VMEM_LIMIT)


def _ada_kernel(c_ref, w_ref, b_ref, o_ref):
    c = c_ref[...]
    c_act = c * (1.0 / (1.0 + jnp.exp(-c)))
    o_ref[...] = jnp.dot(c_act, w_ref[...], preferred_element_type=F32,
                         precision=lax.Precision.HIGHEST) + b_ref[...]


def _ada_all(c, w_ada, b_ada):
    depth = w_ada.shape[0]
    nb = c.shape[0]
    return pl.pallas_call(
        _ada_kernel,
        grid=(depth, 6),
        in_specs=[
            pl.BlockSpec((nb, D_MODEL), lambda l, j: (0, 0)),
            pl.BlockSpec((None, D_MODEL, D_MODEL), lambda l, j: (l, 0, j)),
            pl.BlockSpec((None, 1, D_MODEL), lambda l, j: (l, 0, j)),
        ],
        out_specs=pl.BlockSpec((None, nb, D_MODEL), lambda l, j: (l, 0, j)),
        out_shape=jax.ShapeDtypeStruct((depth, nb, 6 * D_MODEL), F32),
        compiler_params=_cparams(("arbitrary", "arbitrary")),
        name="ada",
    )(c, w_ada, b_ada.reshape(depth, 1, 6 * D_MODEL))


def _modulated_rms(x, g, scale, shift):
    ms = jnp.mean(x * x, axis=-1, keepdims=True)
    return (x * lax.rsqrt(ms + EPS)) * g * (1.0 + scale) + shift


def _norm_rope_t(q, cos_t, sin_t, perm_blocks):
    ss = jnp.sum(q * q, axis=0, keepdims=True)
    r = lax.rsqrt(ss * (1.0 / HEAD_DIM) + EPS)
    qp = jnp.concatenate([q[a:b] for a, b in perm_blocks], axis=0)
    return (q * cos_t + qp * sin_t) * r


AXIAL_BLOCKS = ((16, 32), (0, 16), (48, 64), (32, 48))
SEQ_BLOCKS = ((32, 64), (0, 32))


def _in_kernel(x_ref, ada_ref, g_ref, wn_ref, wt_ref, tab_ref,
               qat_ref, ka_ref, vat_ref,
               qb0_ref, kb0_ref, vb0_ref, qb1_ref, kb1_ref, vb1_ref, qb2_ref, kb2_ref, vb2_ref,
               ga_ref, gb_ref):
    x = x_ref[...]
    h = _modulated_rms(x, g_ref[...], ada_ref[1:2, :], ada_ref[0:1, :]).astype(BF16)

    vb_refs = (vb0_ref, vb1_ref, vb2_ref)
    for g in range(N_B):
        v = jnp.dot(h, wn_ref[:, g * B_W:(g + 1) * B_W], preferred_element_type=F32)
        vb_refs[g][...] = v.astype(BF16)
    for i, gate_ref in enumerate((ga_ref, gb_ref)):
        lo = N_B * B_W + i * D_MODEL
        z = jnp.dot(h, wn_ref[:, lo:lo + D_MODEL], preferred_element_type=F32)
        gate_ref[...] = (1.0 / (1.0 + jnp.exp(-z))).astype(BF16)

    nt = (((1,), (1,)), ((), ()))

    def proj_t(row0, rows):
        return lax.dot_general(wt_ref[row0:row0 + rows, :], h, nt, preferred_element_type=F32)

    def heads_t(rt, n_heads, tab0, blocks):
        cos_t = tab_ref[tab0 * HEAD_DIM:(tab0 + 1) * HEAD_DIM, :]
        sin_t = tab_ref[(tab0 + 1) * HEAD_DIM:(tab0 + 2) * HEAD_DIM, :]
        return [_norm_rope_t(rt[i * HEAD_DIM:(i + 1) * HEAD_DIM], cos_t, sin_t, blocks)
                for i in range(n_heads)]

    row = 0
    rt = proj_t(row, A_Q_W)
    qa = heads_t(rt, A_Q_HEADS, 0, AXIAL_BLOCKS)
    for i in range(A_Q_HEADS):
        qat_ref[i * HEAD_DIM:(i + 1) * HEAD_DIM, :] = qa[i].astype(BF16)
    row += A_Q_W

    rt = proj_t(row, 2 * A_KV_W)
    ka = heads_t(rt[:A_KV_W], A_KV_HEADS, 2, AXIAL_BLOCKS)
    ka_ref[...] = jnp.concatenate(ka, axis=0).T.astype(BF16)
    vat_ref[...] = rt[A_KV_W:].astype(BF16)
    row += 2 * A_KV_W

    qb_refs = (qb0_ref, qb1_ref, qb2_ref)
    kb_refs = (kb0_ref, kb1_ref, kb2_ref)
    for g in range(N_B):
        rt = proj_t(row, 2 * B_W)
        qb = heads_t(rt[:B_W], B_HEADS, 4 + 4 * g, SEQ_BLOCKS)
        kb = heads_t(rt[B_W:], B_HEADS, 6 + 4 * g, SEQ_BLOCKS)
        qb_refs[g][...] = jnp.concatenate(qb, axis=0).T.astype(BF16)
        kb_refs[g][...] = jnp.concatenate(kb, axis=0).T.astype(BF16)
        row += 2 * B_W


def _in_proj(x, ada_l, g_mix_l, wn, wt, tabs, l):
    nb, seq, _ = x.shape
    ns = seq // TM_IN
    tok = lambda w: pl.BlockSpec((None, TM_IN, w), lambda s, b: (b, s, 0))
    tchunk = lambda r: pl.BlockSpec((None, None, r, TM_IN), lambda s, b: (b, s, 0, 0))
    nat = lambda w: jax.ShapeDtypeStruct((nb, seq, w), BF16)
    out_specs = [tchunk(A_Q_W), tok(A_KV_W), tchunk(A_KV_W)]
    out_shape = [jax.ShapeDtypeStruct((nb, ns, A_Q_W, TM_IN), BF16), nat(A_KV_W),
                 jax.ShapeDtypeStruct((nb, ns, A_KV_W, TM_IN), BF16)]
    for _ in range(N_B):
        out_specs += [tok(B_W)] * 3
        out_shape += [nat(B_W)] * 3
    out_specs += [tok(D_MODEL)] * 2
    out_shape += [nat(D_MODEL)] * 2
    return pl.pallas_call(
        _in_kernel,
        grid=(ns, nb),
        in_specs=[
            tok(D_MODEL),
            pl.BlockSpec((None, 6, D_MODEL), lambda s, b: (b, 0, 0)),
            pl.BlockSpec((None, 1, D_MODEL), lambda s, b: (l, 0, 0)),
            pl.BlockSpec((None, D_MODEL, N_COLS), lambda s, b: (l, 0, 0), pipeline_mode=pl.Buffered(1)),
            pl.BlockSpec((None, T_ROWS, D_MODEL), lambda s, b: (l, 0, 0), pipeline_mode=pl.Buffered(1)),
            pl.BlockSpec((None, N_TABS * HEAD_DIM, TM_IN), lambda s, b: (l, 0, s)),
        ],
        out_specs=out_specs,
        out_shape=out_shape,
        compiler_params=_cparams(("arbitrary", "arbitrary")),
        name=f"in_proj",
    )(x, ada_l, g_mix_l, wn, wt, tabs)


def _attn_a_kernel(qt_ref, k_ref, vt_ref, o_ref, ot_ref):
    n_chunks, _, tk = vt_ref.shape
    tq = qt_ref.shape[1]
    rep = A_Q_HEADS // A_KV_HEADS
    zeros = jnp.zeros((HEAD_DIM, tq), BF16)
    for g in range(A_KV_HEADS):
        def head_body(hh, _, g=g):
            row0 = pl.multiple_of((g * rep + hh) * HEAD_DIM, HEAD_DIM)
            qh = qt_ref[pl.ds(row0, HEAD_DIM), :]
            qpad = jnp.concatenate([qh, zeros] if g == 0 else [zeros, qh], axis=0)

            def chunk_body(c, carry):
                m, l, acc = carry
                kc = k_ref[pl.ds(pl.multiple_of(c * tk, tk), tk), :]
                s = jnp.dot(kc, qpad, preferred_element_type=F32)
                m_new = jnp.maximum(m, jnp.max(s, axis=0, keepdims=True))
                alpha = jnp.exp(m - m_new)
                p = jnp.exp(s - m_new)
                l = alpha * l + jnp.sum(p, axis=0, keepdims=True)
                vc = vt_ref[c, g * HEAD_DIM:(g + 1) * HEAD_DIM, :]
                acc = acc * alpha + jnp.dot(vc, p.astype(BF16), preferred_element_type=F32)
                return m_new, l, acc

            init = (jnp.full((1, tq), NEG_BIG, F32), jnp.zeros((1, tq), F32),
                    jnp.zeros((HEAD_DIM, tq), F32))
            m, l, acc = lax.fori_loop(0, n_chunks, chunk_body, init)
            ot_ref[pl.ds(row0, HEAD_DIM), :] = acc / l
            return 0

        lax.fori_loop(0, rep, head_body, 0)
    for j in range(A_Q_W // V7X_LANES):
        blk = ot_ref[j * V7X_LANES:(j + 1) * V7X_LANES, :]
        o_ref[:, j * V7X_LANES:(j + 1) * V7X_LANES] = blk.T.astype(BF16)


def _attn_a(qat, ka, vat):
    nb, ns, _, tq = qat.shape
    seq = ka.shape[1]
    return pl.pallas_call(
        _attn_a_kernel,
        grid=(nb, ns),
        in_specs=[
            pl.BlockSpec((None, None, A_Q_W, tq), lambda b, s: (b, s, 0, 0)),
            pl.BlockSpec((None, seq, A_KV_W), lambda b, s: (b, 0, 0)),
            pl.BlockSpec((None, ns, A_KV_W, tq), lambda b, s: (b, 0, 0, 0)),
        ],
        out_specs=pl.BlockSpec((None, tq, A_Q_W), lambda b, s: (b, s, 0)),
        out_shape=jax.ShapeDtypeStruct((nb, seq, A_Q_W), BF16),
        scratch_shapes=[pltpu.VMEM((A_Q_W, tq), F32)],
        compiler_params=_cparams(("arbitrary", "arbitrary")),
        name="attn_a",
    )(qat, ka, vat)


def _attn_b_kernel(q_ref, k_ref, v_ref, o_ref, lse_ref, *, half_window):
    tj = q_ref.shape[0]
    length = k_ref.shape[0]
    j_tile = pl.program_id(2) * tj
    lane = lax.broadcasted_iota(jnp.int32, (1, V7X_LANES), 1)
    first_head = lane < HEAD_DIM
    nt = (((1,), (1,)), ((), ()))
    for qi in range(tj // TQ_B):
        j0 = j_tile + qi * TQ_B
        ks = jnp.clip(j0 - half_window, 0, length - KW_B)
        ks = pl.multiple_of(ks, half_window)
        jq = j0 + lax.broadcasted_iota(jnp.int32, (TQ_B, 1), 0)
        jk = ks + lax.broadcasted_iota(jnp.int32, (1, KW_B), 1)
        valid = jnp.abs(jq - jk) <= half_window
        for pair in range(B_W // V7X_LANES):
            cols = slice(pair * V7X_LANES, (pair + 1) * V7X_LANES)
            q2 = q_ref[qi * TQ_B:(qi + 1) * TQ_B, cols]
            k2 = k_ref[pl.ds(ks, KW_B), cols]
            v2 = v_ref[pl.ds(ks, KW_B), cols]
            res = []
            for hh in range(2):
                keep = first_head if hh == 0 else jnp.logical_not(first_head)
                qm = jnp.where(keep, q2, jnp.zeros_like(q2))
                s = lax.dot_general(qm, k2, nt, preferred_element_type=F32)
                s = jnp.where(valid, s, NEG_BIG)
                m = jnp.max(s, axis=1, keepdims=True)
                p = jnp.exp(s - m)
                l = jnp.sum(p, axis=1, keepdims=True)
                o = jnp.dot(p.astype(BF16), v2, preferred_element_type=F32)
                res.append((o / l, m + jnp.log(l)))
            o_pair = jnp.where(first_head, res[0][0], res[1][0])
            lse_pair = jnp.where(first_head, res[0][1], res[1][1])
            o_ref[qi * TQ_B:(qi + 1) * TQ_B, cols] = o_pair.astype(BF16)
            lse_ref[qi * TQ_B:(qi + 1) * TQ_B, cols] = lse_pair


def _attn_b(q, k, v, window, dil):
    nb, seq, _ = q.shape
    length = seq // dil
    half_window = (window // 2) // dil
    assert half_window * 2 + TQ_B == KW_B and length >= KW_B
    tj = min(512, length)
    view = lambda a: a.reshape(nb, length, dil * B_W)
    qspec = pl.BlockSpec((None, tj, B_W), lambda b, r, j: (b, j, r))
    kvspec = pl.BlockSpec((None, length, B_W), lambda b, r, j: (b, 0, r))
    o, lse = pl.pallas_call(
        functools.partial(_attn_b_kernel, half_window=half_window),
        grid=(nb, dil, length // tj),
        in_specs=[qspec, kvspec, kvspec],
        out_specs=[qspec, qspec],
        out_shape=[jax.ShapeDtypeStruct((nb, length, dil * B_W), BF16),
                   jax.ShapeDtypeStruct((nb, length, dil * B_W), F32)],
        compiler_params=_cparams(("arbitrary", "arbitrary", "arbitrary")),
        name=f"attn_b_d{dil}",
    )(view(q), view(k), view(v))
    return o.reshape(nb, seq, B_W), lse.reshape(nb, seq, B_W)


def _merge_kernel(x_ref, ada_ref, oa_ref, ob0_ref, ob1_ref, ob2_ref, ls0_ref, ls1_ref, ls2_ref,
                  ga_ref, gb_ref, wa_ref, wb_ref, wo_ref, xo_ref):
    ya = jnp.dot(oa_ref[...], wa_ref[...], preferred_element_type=F32)
    lses = [r[...] for r in (ls0_ref, ls1_ref, ls2_ref)]
    outs = [r[...].astype(F32) for r in (ob0_ref, ob1_ref, ob2_ref)]
    top = jnp.maximum(jnp.maximum(lses[0], lses[1]), lses[2])
    es = [jnp.exp(v - top) for v in lses]
    den = es[0] + es[1] + es[2]
    ob = (es[0] * outs[0] + es[1] * outs[1] + es[2] * outs[2]) / den
    yb = jnp.dot(ob.astype(BF16), wb_ref[...], preferred_element_type=F32)
    mixed = ga_ref[...].astype(F32) * ya + gb_ref[...].astype(F32) * yb
    upd = jnp.dot(mixed.astype(BF16), wo_ref[...], preferred_element_type=F32)
    xo_ref[...] = x_ref[...] + ada_ref[2:3, :] * upd


def _merge(x, ada_l, oa, obs, lss, ga, gb, wa, wb, wo, l):
    nb, seq, _ = x.shape
    tok = lambda w: pl.BlockSpec((None, TM_MERGE, w), lambda b, s: (b, s, 0))
    wspec = lambda r, c: pl.BlockSpec((None, r, c), lambda b, s: (l, 0, 0), pipeline_mode=pl.Buffered(1))
    return pl.pallas_call(
        _merge_kernel,
        grid=(nb, seq // TM_MERGE),
        in_specs=[tok(D_MODEL), pl.BlockSpec((None, 6, D_MODEL), lambda b, s: (b, 0, 0)), tok(A_Q_W)]
                 + [tok(B_W)] * 6 + [tok(D_MODEL)] * 2
                 + [wspec(A_Q_W, D_MODEL), wspec(B_W, D_MODEL), wspec(D_MODEL, D_MODEL)],
        out_specs=tok(D_MODEL),
        out_shape=jax.ShapeDtypeStruct(x.shape, F32),
        input_output_aliases={0: 0},
        compiler_params=_cparams(("arbitrary", "arbitrary")),
        name="merge",
    )(x, ada_l, oa, *obs, *lss, ga, gb, wa, wb, wo)


def _mlp_kernel(x_ref, ada_ref, g_ref, w1_ref, w2_ref, xo_ref):
    x = x_ref[...]
    h = _modulated_rms(x, g_ref[...], ada_ref[4:5, :], ada_ref[3:4, :]).astype(BF16)
    acc = jnp.zeros(x.shape, F32)
    for c in range(D_FF // FF_CHUNK):
        a = jnp.dot(h, w1_ref[:, c * FF_CHUNK:(c + 1) * FF_CHUNK], preferred_element_type=F32)
        a = jnp.square(jnp.maximum(a, 0.0)).astype(BF16)
        acc = acc + jnp.dot(a, w2_ref[c * FF_CHUNK:(c + 1) * FF_CHUNK, :], preferred_element_type=F32)
    xo_ref[...] = x + ada_ref[5:6, :] * acc


def _mlp(x, ada_l, g_mlp_l, w1, w2, l):
    nb, seq, _ = x.shape
    tok = pl.BlockSpec((None, TM_MLP, D_MODEL), lambda b, s: (b, s, 0))
    return pl.pallas_call(
        _mlp_kernel,
        grid=(nb, seq // TM_MLP),
        in_specs=[
            tok,
            pl.BlockSpec((None, 6, D_MODEL), lambda b, s: (b, 0, 0)),
            pl.BlockSpec((None, 1, D_MODEL), lambda b, s: (l, 0, 0)),
            pl.BlockSpec((None, D_MODEL, D_FF), lambda b, s: (l, 0, 0), pipeline_mode=pl.Buffered(1)),
            pl.BlockSpec((None, D_FF, D_MODEL), lambda b, s: (l, 0, 0), pipeline_mode=pl.Buffered(1)),
        ],
        out_specs=tok,
        out_shape=jax.ShapeDtypeStruct(x.shape, F32),
        input_output_aliases={0: 0},
        compiler_params=_cparams(("arbitrary", "arbitrary")),
        name="mlp",
    )(x, ada_l, g_mlp_l, w1, w2)


def _rope_tables(seq):
    pos = jnp.arange(seq)
    rows = seq // GRID_W
    row_idx = jnp.broadcast_to(jnp.arange(rows)[:, None], (rows, GRID_W)).reshape(-1)
    col_idx = jnp.broadcast_to(jnp.arange(GRID_W)[None, :], (rows, GRID_W)).reshape(-1)

    def cos_sin(p, dim):
        inv = ROPE_THETA ** (-jnp.arange(0, dim, 2, dtype=F32) / dim)
        ang = p.astype(F32)[:, None] * inv[None, :]
        return jnp.cos(ang), jnp.sin(ang)

    rc, rs = cos_sin(row_idx, HEAD_DIM // 2)
    cc, cs = cos_sin(col_idx, HEAD_DIM // 2)
    sc, ss = cos_sin(pos, HEAD_DIM)
    cos_ax = jnp.concatenate([rc, rc, cc, cc], axis=-1)
    sin_ax = jnp.concatenate([-rs, rs, -cs, cs], axis=-1)
    cos_sq = jnp.concatenate([sc, sc], axis=-1)
    sin_sq = jnp.concatenate([-ss, ss], axis=-1)
    return (cos_ax, sin_ax), (cos_sq, sin_sq)


def _gain_tables(q_norm_a, k_norm_a, q_norm_b, k_norm_b, seq):
    (cos_ax, sin_ax), (cos_sq, sin_sq) = _rope_tables(seq)
    scale = HEAD_DIM ** -0.5

    def pair(gain, cos, sin, perm, mult):
        c = gain[:, None, :] * cos[None] * mult
        s = gain[:, perm][:, None, :] * sin[None] * mult
        return [jnp.swapaxes(c, 1, 2), jnp.swapaxes(s, 1, 2)]

    tabs = pair(q_norm_a, cos_ax, sin_ax, PERM_AXIAL, scale) + pair(k_norm_a, cos_ax, sin_ax, PERM_AXIAL, 1.0)
    for g in range(N_B):
        tabs += pair(q_norm_b[:, g], cos_sq, sin_sq, PERM_SEQ, scale)
        tabs += pair(k_norm_b[:, g], cos_sq, sin_sq, PERM_SEQ, 1.0)
    return jnp.concatenate(tabs, axis=1)


def _split_w_in(w_in):
    sizes = [A_Q_W, A_KV_W, A_KV_W] + [B_W] * (3 * N_B) + [D_MODEL, D_MODEL]
    offs = np.concatenate([[0], np.cumsum(sizes)])
    col = lambda i: w_in[:, :, offs[i]:offs[i + 1]]
    names = ["qa", "ka", "va"] + [f"{t}b{g}" for g in range(N_B) for t in "qkv"] + ["ga", "gb"]
    parts = {n: col(i) for i, n in enumerate(names)}
    wn = jnp.concatenate([parts["vb0"], parts["vb1"], parts["vb2"], parts["ga"], parts["gb"]], axis=-1)
    wt = jnp.concatenate([parts[n] for n, _ in T_SEGS], axis=-1)
    return wn.astype(BF16), jnp.swapaxes(wt, 1, 2).astype(BF16)


def kernel(x, c, w_ada, b_ada, g_mix, g_mlp, w_in, q_norm_a, k_norm_a, q_norm_b, k_norm_b,
           w_branch_a, w_branch_b, w_out, w_ff1, w_ff2):
    nb, seq, _ = x.shape
    depth = w_in.shape[0]
    assert seq % GRID_W == 0 and seq % TM_MERGE == 0 and seq % TM_IN == 0

    ada = _ada_all(c, w_ada, b_ada).reshape(depth, nb, 6, D_MODEL)
    tabs = _gain_tables(q_norm_a, k_norm_a, q_norm_b, k_norm_b, seq)
    wn, wt = _split_w_in(w_in)
    wa, wb, wo = w_branch_a.astype(BF16), w_branch_b.astype(BF16), w_out.astype(BF16)
    w1, w2 = w_ff1.astype(BF16), w_ff2.astype(BF16)
    g_mix = g_mix.reshape(depth, 1, D_MODEL)
    g_mlp = g_mlp.reshape(depth, 1, D_MODEL)

    for l in range(depth):
        (qat, ka, vat, qb0, kb0, vb0, qb1, kb1, vb1, qb2, kb2, vb2, ga, gb) = _in_proj(
            x, ada[l], g_mix, wn, wt, tabs, l)
        oa = _attn_a(qat, ka, vat)
        obs, lss = [], []
        for (window, dil), q, k, v in zip(B_GROUPS, (qb0, qb1, qb2), (kb0, kb1, kb2), (vb0, vb1, vb2)):
            o, lse = _attn_b(q, k, v, window, dil)
            obs.append(o)
            lss.append(lse)
        x = _merge(x, ada[l], oa, obs, lss, ga, gb, wa, wb, wo, l)
        x = _mlp(x, ada[l], g_mlp, w1, w2, l)
    return x
```

```python
import functools

import jax
import jax.numpy as jnp
import numpy as np
from jax import lax
from jax.experimental import pallas as pl
from jax.experimental.pallas import tpu as pltpu

D_MODEL = 1024
HEAD_DIM = 64
A_Q_HEADS = 8
A_KV_HEADS = 2
B_GROUPS = ((128, 1), (512, 4), (2048, 16))
B_HEADS = 4
N_B = len(B_GROUPS)
D_FF = 4 * D_MODEL
GRID_W = 64
ROPE_THETA = 10000.0
EPS = 1e-6

A_Q_W = A_Q_HEADS * HEAD_DIM
A_KV_W = A_KV_HEADS * HEAD_DIM
B_W = B_HEADS * HEAD_DIM

V7X_LANES = 128
V7X_VMEM_BYTES = 64 * 1024 * 1024
VMEM_LIMIT = 52 * 1024 * 1024

BF16 = jnp.bfloat16
F32 = jnp.float32

TM_IN = 256
TM_MERGE = 512
TM_MLP = 512
FF_CHUNK = 1024
TQ_B = 128
KW_B = 256
NEG_BIG = -1e30
LOG2_E = 1.4426950408889634

PERM_AXIAL = np.concatenate([np.arange(16, 32), np.arange(0, 16), np.arange(48, 64), np.arange(32, 48)])
PERM_SEQ = np.concatenate([np.arange(32, 64), np.arange(0, 32)])

T_SEGS = (("qa", A_Q_W), ("ka", A_KV_W), ("va", A_KV_W),
          ("qb0", B_W), ("kb0", B_W), ("qb1", B_W), ("kb1", B_W), ("qb2", B_W), ("kb2", B_W))
T_ROWS = sum(r for _, r in T_SEGS)
N_COLS = N_B * B_W + 2 * D_MODEL
N_TABS = 16


def _cparams(sem):
    return pltpu.CompilerParams(dimension_semantics=sem, vmem_limit_bytes=VMEM_LIMIT)


def _ada_kernel(c_ref, w_ref, b_ref, o_ref):
    c = c_ref[...]
    c_act = c * (1.0 / (1.0 + jnp.exp(-c)))
    o_ref[...] = jnp.dot(c_act, w_ref[...], preferred_element_type=F32,
                         precision=lax.Precision.HIGHEST) + b_ref[...]


def _ada_all(c, w_ada, b_ada):
    depth = w_ada.shape[0]
    nb = c.shape[0]
    return pl.pallas_call(
        _ada_kernel,
        grid=(depth, 6),
        in_specs=[
            pl.BlockSpec((nb, D_MODEL), lambda l, j: (0, 0)),
            pl.BlockSpec((None, D_MODEL, D_MODEL), lambda l, j: (l, 0, j)),
            pl.BlockSpec((None, 1, D_MODEL), lambda l, j: (l, 0, j)),
        ],
        out_specs=pl.BlockSpec((None, nb, D_MODEL), lambda l, j: (l, 0, j)),
        out_shape=jax.ShapeDtypeStruct((depth, nb, 6 * D_MODEL), F32),
        compiler_params=_cparams(("arbitrary", "arbitrary")),
        name="ada",
    )(c, w_ada, b_ada.reshape(depth, 1, 6 * D_MODEL))


def _modulated_rms(x, g, scale, shift):
    ms = jnp.mean(x * x, axis=-1, keepdims=True)
    return (x * lax.rsqrt(ms + EPS)) * g * (1.0 + scale) + shift


def _norm_rope_t(q, cos_t, sin_t, perm_blocks):
    ss = jnp.sum(q * q, axis=0, keepdims=True)
    r = lax.rsqrt(ss * (1.0 / HEAD_DIM) + EPS)
    qp = jnp.concatenate([q[a:b] for a, b in perm_blocks], axis=0)
    return (q * cos_t + qp * sin_t) * r


AXIAL_BLOCKS = ((16, 32), (0, 16), (48, 64), (32, 48))
SEQ_BLOCKS = ((32, 64), (0, 32))


def _in_kernel(x_ref, ada_ref, g_ref, wn_ref, wt_ref, tab_ref,
               qat_ref, ka_ref, vat_ref,
               qb0_ref, kb0_ref, vb0_ref, qb1_ref, kb1_ref, vb1_ref, qb2_ref, kb2_ref, vb2_ref,
               ga_ref, gb_ref):
    x = x_ref[...]
    h = _modulated_rms(x, g_ref[...], ada_ref[1:2, :], ada_ref[0:1, :]).astype(BF16)

    vb_refs = (vb0_ref, vb1_ref, vb2_ref)
    for g in range(N_B):
        v = jnp.dot(h, wn_ref[:, g * B_W:(g + 1) * B_W], preferred_element_type=F32)
        vb_refs[g][...] = v.astype(BF16)
    for i, gate_ref in enumerate((ga_ref, gb_ref)):
        lo = N_B * B_W + i * D_MODEL
        z = jnp.dot(h, wn_ref[:, lo:lo + D_MODEL], preferred_element_type=F32)
        gate_ref[...] = (1.0 / (1.0 + jnp.exp(-z))).astype(BF16)

    nt = (((1,), (1,)), ((), ()))

    def proj_t(row0, rows):
        return lax.dot_general(wt_ref[row0:row0 + rows, :], h, nt, preferred_element_type=F32)

    def heads_t(rt, n_heads, tab0, blocks):
        cos_t = tab_ref[tab0 * HEAD_DIM:(tab0 + 1) * HEAD_DIM, :]
        sin_t = tab_ref[(tab0 + 1) * HEAD_DIM:(tab0 + 2) * HEAD_DIM, :]
        return [_norm_rope_t(rt[i * HEAD_DIM:(i + 1) * HEAD_DIM], cos_t, sin_t, blocks)
                for i in range(n_heads)]

    row = 0
    rt = proj_t(row, A_Q_W)
    qa = heads_t(rt, A_Q_HEADS, 0, AXIAL_BLOCKS)
    for i in range(A_Q_HEADS):
        qat_ref[i * HEAD_DIM:(i + 1) * HEAD_DIM, :] = qa[i].astype(BF16)
    row += A_Q_W

    rt = proj_t(row, 2 * A_KV_W)
    ka = heads_t(rt[:A_KV_W], A_KV_HEADS, 2, AXIAL_BLOCKS)
    ka_ref[...] = jnp.concatenate(ka, axis=0).T.astype(BF16)
    vat_ref[...] = rt[A_KV_W:].astype(BF16)
    row += 2 * A_KV_W

    qb_refs = (qb0_ref, qb1_ref, qb2_ref)
    kb_refs = (kb0_ref, kb1_ref, kb2_ref)
    for g in range(N_B):
        rt = proj_t(row, 2 * B_W)
        qb = heads_t(rt[:B_W], B_HEADS, 4 + 4 * g, SEQ_BLOCKS)
        kb = heads_t(rt[B_W:], B_HEADS, 6 + 4 * g, SEQ_BLOCKS)
        qb_refs[g][...] = jnp.concatenate(qb, axis=0).T.astype(BF16)
        kb_refs[g][...] = jnp.concatenate(kb, axis=0).T.astype(BF16)
        row += 2 * B_W


def _in_proj(x, ada_l, g_mix_l, wn, wt, tabs, l):
    nb, seq, _ = x.shape
    ns = seq // TM_IN
    tok = lambda w: pl.BlockSpec((None, TM_IN, w), lambda s, b: (b, s, 0))
    tchunk = lambda r: pl.BlockSpec((None, None, r, TM_IN), lambda s, b: (b, s, 0, 0))
    nat = lambda w: jax.ShapeDtypeStruct((nb, seq, w), BF16)
    out_specs = [tchunk(A_Q_W), tok(A_KV_W), tchunk(A_KV_W)]
    out_shape = [jax.ShapeDtypeStruct((nb, ns, A_Q_W, TM_IN), BF16), nat(A_KV_W),
                 jax.ShapeDtypeStruct((nb, ns, A_KV_W, TM_IN), BF16)]
    for _ in range(N_B):
        out_specs += [tok(B_W)] * 3
        out_shape += [nat(B_W)] * 3
    out_specs += [tok(D_MODEL)] * 2
    out_shape += [nat(D_MODEL)] * 2
    return pl.pallas_call(
        _in_kernel,
        grid=(ns, nb),
        in_specs=[
            tok(D_MODEL),
            pl.BlockSpec((None, 6, D_MODEL), lambda s, b: (b, 0, 0)),
            pl.BlockSpec((None, 1, D_MODEL), lambda s, b: (l, 0, 0)),
            pl.BlockSpec((None, D_MODEL, N_COLS), lambda s, b: (l, 0, 0), pipeline_mode=pl.Buffered(1)),
            pl.BlockSpec((None, T_ROWS, D_MODEL), lambda s, b: (l, 0, 0), pipeline_mode=pl.Buffered(1)),
            pl.BlockSpec((None, N_TABS * HEAD_DIM, TM_IN), lambda s, b: (l, 0, s)),
        ],
        out_specs=out_specs,
        out_shape=out_shape,
        compiler_params=_cparams(("arbitrary", "arbitrary")),
        name=f"in_proj",
    )(x, ada_l, g_mix_l, wn, wt, tabs)


A_REP = A_Q_HEADS // A_KV_HEADS
ONES_ROWS = 16


def _attn_a_kernel(qt_ref, k_ref, vt_ref, o_ref, qpad_ref, acc_ref, m_ref, s_ref, smax_ref):
    n_chunks, _, tk = vt_ref.shape
    tq = qt_ref.shape[1]
    qpad_ref[...] = jnp.zeros(qpad_ref.shape, BF16)
    for g in range(A_KV_HEADS):
        for hh in range(A_REP):
            h = g * A_REP + hh
            qpad_ref[g, g * HEAD_DIM:(g + 1) * HEAD_DIM, hh * tq:(hh + 1) * tq] = (
                qt_ref[h * HEAD_DIM:(h + 1) * HEAD_DIM, :])
    m_ref[...] = jnp.full(m_ref.shape, NEG_BIG, F32)
    acc_ref[...] = jnp.zeros(acc_ref.shape, F32)
    ones = jnp.ones((ONES_ROWS, tk), BF16)

    def scores(c, slot, g):
        kc = k_ref[pl.ds(pl.multiple_of(c * tk, tk), tk), :]
        s = jnp.dot(kc, qpad_ref[g], preferred_element_type=F32)
        s_ref[slot, g] = s
        smax_ref[slot, g] = jnp.max(s, axis=0, keepdims=True)

    def consume(c, slot, g):
        m_prev = m_ref[g]
        m_new = jnp.maximum(m_prev, smax_ref[slot, g])
        alpha = jnp.exp2(m_prev - m_new)
        p = jnp.exp2(s_ref[slot, g] - m_new).astype(BF16)
        v_aug = jnp.concatenate([vt_ref[c, g * HEAD_DIM:(g + 1) * HEAD_DIM, :], ones], axis=0)
        acc_ref[g] = acc_ref[g] * alpha + jnp.dot(v_aug, p, preferred_element_type=F32)
        m_ref[g] = m_new

    for g in range(A_KV_HEADS):
        scores(0, 0, g)

    def pair_body(i, carry):
        c = 2 * i
        for slot in range(2):
            c_next = jnp.minimum(c + slot + 1, n_chunks - 1)
            for g in range(A_KV_HEADS):
                scores(c_next, 1 - slot, g)
                consume(c + slot, slot, g)
        return carry

    lax.fori_loop(0, n_chunks // 2, pair_body, 0)
    for j in range(A_Q_W // V7X_LANES):
        halves = []
        for h in (2 * j, 2 * j + 1):
            g, hh = divmod(h, A_REP)
            cols = slice(hh * tq, (hh + 1) * tq)
            halves.append(acc_ref[g, :HEAD_DIM, cols] / acc_ref[g, HEAD_DIM:HEAD_DIM + 1, cols])
        blk = jnp.concatenate(halves, axis=0)
        o_ref[:, j * V7X_LANES:(j + 1) * V7X_LANES] = blk.T.astype(BF16)


def _attn_a(qat, ka, vat):
    nb, ns, _, tq = qat.shape
    seq = ka.shape[1]
    wide = A_REP * tq
    return pl.pallas_call(
        _attn_a_kernel,
        grid=(nb, ns),
        in_specs=[
            pl.BlockSpec((None, None, A_Q_W, tq), lambda b, s: (b, s, 0, 0)),
            pl.BlockSpec((None, seq, A_KV_W), lambda b, s: (b, 0, 0)),
            pl.BlockSpec((None, ns, A_KV_W, tq), lambda b, s: (b, 0, 0, 0)),
        ],
        out_specs=pl.BlockSpec((None, tq, A_Q_W), lambda b, s: (b, s, 0)),
        out_shape=jax.ShapeDtypeStruct((nb, seq, A_Q_W), BF16),
        scratch_shapes=[pltpu.VMEM((A_KV_HEADS, A_KV_W, wide), BF16),
                        pltpu.VMEM((A_KV_HEADS, HEAD_DIM + ONES_ROWS, wide), F32),
                        pltpu.VMEM((A_KV_HEADS, 1, wide), F32),
                        pltpu.VMEM((2, A_KV_HEADS, tq, wide), F32),
                        pltpu.VMEM((2, A_KV_HEADS, 1, wide), F32)],
        compiler_params=_cparams(("arbitrary", "arbitrary")),
        name="attn_a",
    )(qat, ka, vat)


def _attn_b_kernel(q_ref, k_ref, v_ref, o_ref, lse_ref, *, half_window):
    tj = q_ref.shape[0]
    length = k_ref.shape[0]
    j_tile = pl.program_id(2) * tj
    lane = lax.broadcasted_iota(jnp.int32, (1, V7X_LANES), 1)
    first_head = lane < HEAD_DIM
    nt = (((1,), (1,)), ((), ()))
    for qi in range(tj // TQ_B):
        j0 = j_tile + qi * TQ_B
        ks = jnp.clip(j0 - half_window, 0, length - KW_B)
        ks = pl.multiple_of(ks, half_window)
        jq = j0 + lax.broadcasted_iota(jnp.int32, (TQ_B, 1), 0)
        jk = ks + lax.broadcasted_iota(jnp.int32, (1, KW_B), 1)
        valid = jnp.abs(jq - jk) <= half_window
        for pair in range(B_W // V7X_LANES):
            cols = slice(pair * V7X_LANES, (pair + 1) * V7X_LANES)
            q2 = q_ref[qi * TQ_B:(qi + 1) * TQ_B, cols]
            k2 = k_ref[pl.ds(ks, KW_B), cols]
            v2 = v_ref[pl.ds(ks, KW_B), cols]
            res = []
            for hh in range(2):
                keep = first_head if hh == 0 else jnp.logical_not(first_head)
                qm = jnp.where(keep, q2, jnp.zeros_like(q2))
                s = lax.dot_general(qm, k2, nt, preferred_element_type=F32)
                s = jnp.where(valid, s, NEG_BIG)
                m = jnp.max(s, axis=1, keepdims=True)
                p = jnp.exp(s - m)
                l = jnp.sum(p, axis=1, keepdims=True)
                o = jnp.dot(p.astype(BF16), v2, preferred_element_type=F32)
                res.append((o / l, m + jnp.log(l)))
            o_pair = jnp.where(first_head, res[0][0], res[1][0])
            lse_pair = jnp.where(first_head, res[0][1], res[1][1])
            o_ref[qi * TQ_B:(qi + 1) * TQ_B, cols] = o_pair.astype(BF16)
            lse_ref[qi * TQ_B:(qi + 1) * TQ_B, cols] = lse_pair


def _attn_b(q, k, v, window, dil):
    nb, seq, _ = q.shape
    length = seq // dil
    half_window = (window // 2) // dil
    assert half_window * 2 + TQ_B == KW_B and length >= KW_B
    tj = min(512, length)
    view = lambda a: a.reshape(nb, length, dil * B_W)
    qspec = pl.BlockSpec((None, tj, B_W), lambda b, r, j: (b, j, r))
    kvspec = pl.BlockSpec((None, length, B_W), lambda b, r, j: (b, 0, r))
    o, lse = pl.pallas_call(
        functools.partial(_attn_b_kernel, half_window=half_window),
        grid=(nb, dil, length // tj),
        in_specs=[qspec, kvspec, kvspec],
        out_specs=[qspec, qspec],
        out_shape=[jax.ShapeDtypeStruct((nb, length, dil * B_W), BF16),
                   jax.ShapeDtypeStruct((nb, length, dil * B_W), F32)],
        compiler_params=_cparams(("arbitrary", "arbitrary", "arbitrary")),
        name=f"attn_b_d{dil}",
    )(view(q), view(k), view(v))
    return o.reshape(nb, seq, B_W), lse.reshape(nb, seq, B_W)


def _merge_kernel(x_ref, ada_ref, oa_ref, ob0_ref, ob1_ref, ob2_ref, ls0_ref, ls1_ref, ls2_ref,
                  ga_ref, gb_ref, wa_ref, wb_ref, wo_ref, xo_ref):
    ya = jnp.dot(oa_ref[...], wa_ref[...], preferred_element_type=F32)
    lses = [r[...] for r in (ls0_ref, ls1_ref, ls2_ref)]
    outs = [r[...].astype(F32) for r in (ob0_ref, ob1_ref, ob2_ref)]
    top = jnp.maximum(jnp.maximum(lses[0], lses[1]), lses[2])
    es = [jnp.exp(v - top) for v in lses]
    den = es[0] + es[1] + es[2]
    ob = (es[0] * outs[0] + es[1] * outs[1] + es[2] * outs[2]) / den
    yb = jnp.dot(ob.astype(BF16), wb_ref[...], preferred_element_type=F32)
    mixed = ga_ref[...].astype(F32) * ya + gb_ref[...].astype(F32) * yb
    upd = jnp.dot(mixed.astype(BF16), wo_ref[...], preferred_element_type=F32)
    xo_ref[...] = x_ref[...] + ada_ref[2:3, :] * upd


def _merge(x, ada_l, oa, obs, lss, ga, gb, wa, wb, wo, l):
    nb, seq, _ = x.shape
    tok = lambda w: pl.BlockSpec((None, TM_MERGE, w), lambda b, s: (b, s, 0))
    wspec = lambda r, c: pl.BlockSpec((None, r, c), lambda b, s: (l, 0, 0), pipeline_mode=pl.Buffered(1))
    return pl.pallas_call(
        _merge_kernel,
        grid=(nb, seq // TM_MERGE),
        in_specs=[tok(D_MODEL), pl.BlockSpec((None, 6, D_MODEL), lambda b, s: (b, 0, 0)), tok(A_Q_W)]
                 + [tok(B_W)] * 6 + [tok(D_MODEL)] * 2
                 + [wspec(A_Q_W, D_MODEL), wspec(B_W, D_MODEL), wspec(D_MODEL, D_MODEL)],
        out_specs=tok(D_MODEL),
        out_shape=jax.ShapeDtypeStruct(x.shape, F32),
        input_output_aliases={0: 0},
        compiler_params=_cparams(("arbitrary", "arbitrary")),
        name="merge",
    )(x, ada_l, oa, *obs, *lss, ga, gb, wa, wb, wo)


def _mlp_kernel(x_ref, ada_ref, g_ref, w1_ref, w2_ref, xo_ref):
    x = x_ref[...]
    h = _modulated_rms(x, g_ref[...], ada_ref[4:5, :], ada_ref[3:4, :]).astype(BF16)
    acc = jnp.zeros(x.shape, F32)
    for c in range(D_FF // FF_CHUNK):
        a = jnp.dot(h, w1_ref[:, c * FF_CHUNK:(c + 1) * FF_CHUNK], preferred_element_type=F32)
        a = jnp.square(jnp.maximum(a, 0.0)).astype(BF16)
        acc = acc + jnp.dot(a, w2_ref[c * FF_CHUNK:(c + 1) * FF_CHUNK, :], preferred_element_type=F32)
    xo_ref[...] = x + ada_ref[5:6, :] * acc


def _mlp(x, ada_l, g_mlp_l, w1, w2, l):
    nb, seq, _ = x.shape
    tok = pl.BlockSpec((None, TM_MLP, D_MODEL), lambda b, s: (b, s, 0))
    return pl.pallas_call(
        _mlp_kernel,
        grid=(nb, seq // TM_MLP),
        in_specs=[
            tok,
            pl.BlockSpec((None, 6, D_MODEL), lambda b, s: (b, 0, 0)),
            pl.BlockSpec((None, 1, D_MODEL), lambda b, s: (l, 0, 0)),
            pl.BlockSpec((None, D_MODEL, D_FF), lambda b, s: (l, 0, 0), pipeline_mode=pl.Buffered(1)),
            pl.BlockSpec((None, D_FF, D_MODEL), lambda b, s: (l, 0, 0), pipeline_mode=pl.Buffered(1)),
        ],
        out_specs=tok,
        out_shape=jax.ShapeDtypeStruct(x.shape, F32),
        input_output_aliases={0: 0},
        compiler_params=_cparams(("arbitrary", "arbitrary")),
        name="mlp",
    )(x, ada_l, g_mlp_l, w1, w2)


def _rope_tables(seq):
    pos = jnp.arange(seq)
    rows = seq // GRID_W
    row_idx = jnp.broadcast_to(jnp.arange(rows)[:, None], (rows, GRID_W)).reshape(-1)
    col_idx = jnp.broadcast_to(jnp.arange(GRID_W)[None, :], (rows, GRID_W)).reshape(-1)

    def cos_sin(p, dim):
        inv = ROPE_THETA ** (-jnp.arange(0, dim, 2, dtype=F32) / dim)
        ang = p.astype(F32)[:, None] * inv[None, :]
        return jnp.cos(ang), jnp.sin(ang)

    rc, rs = cos_sin(row_idx, HEAD_DIM // 2)
    cc, cs = cos_sin(col_idx, HEAD_DIM // 2)
    sc, ss = cos_sin(pos, HEAD_DIM)
    cos_ax = jnp.concatenate([rc, rc, cc, cc], axis=-1)
    sin_ax = jnp.concatenate([-rs, rs, -cs, cs], axis=-1)
    cos_sq = jnp.concatenate([sc, sc], axis=-1)
    sin_sq = jnp.concatenate([-ss, ss], axis=-1)
    return (cos_ax, sin_ax), (cos_sq, sin_sq)


def _gain_tables(q_norm_a, k_norm_a, q_norm_b, k_norm_b, seq):
    (cos_ax, sin_ax), (cos_sq, sin_sq) = _rope_tables(seq)
    scale = HEAD_DIM ** -0.5

    def pair(gain, cos, sin, perm, mult):
        c = gain[:, None, :] * cos[None] * mult
        s = gain[:, perm][:, None, :] * sin[None] * mult
        return [jnp.swapaxes(c, 1, 2), jnp.swapaxes(s, 1, 2)]

    tabs = pair(q_norm_a, cos_ax, sin_ax, PERM_AXIAL, scale * LOG2_E) + pair(k_norm_a, cos_ax, sin_ax, PERM_AXIAL, 1.0)
    for g in range(N_B):
        tabs += pair(q_norm_b[:, g], cos_sq, sin_sq, PERM_SEQ, scale)
        tabs += pair(k_norm_b[:, g], cos_sq, sin_sq, PERM_SEQ, 1.0)
    return jnp.concatenate(tabs, axis=1)


def _split_w_in(w_in):
    sizes = [A_Q_W, A_KV_W, A_KV_W] + [B_W] * (3 * N_B) + [D_MODEL, D_MODEL]
    offs = np.concatenate([[0], np.cumsum(sizes)])
    col = lambda i: w_in[:, :, offs[i]:offs[i + 1]]
    names = ["qa", "ka", "va"] + [f"{t}b{g}" for g in range(N_B) for t in "qkv"] + ["ga", "gb"]
    parts = {n: col(i) for i, n in enumerate(names)}
    wn = jnp.concatenate([parts["vb0"], parts["vb1"], parts["vb2"], parts["ga"], parts["gb"]], axis=-1)
    wt = jnp.concatenate([parts[n] for n, _ in T_SEGS], axis=-1)
    return wn.astype(BF16), jnp.swapaxes(wt, 1, 2).astype(BF16)


def kernel(x, c, w_ada, b_ada, g_mix, g_mlp, w_in, q_norm_a, k_norm_a, q_norm_b, k_norm_b,
           w_branch_a, w_branch_b, w_out, w_ff1, w_ff2):
    nb, seq, _ = x.shape
    depth = w_in.shape[0]
    assert seq % GRID_W == 0 and seq % TM_MERGE == 0 and seq % TM_IN == 0

    ada = _ada_all(c, w_ada, b_ada).reshape(depth, nb, 6, D_MODEL)
    tabs = _gain_tables(q_norm_a, k_norm_a, q_norm_b, k_norm_b, seq)
    wn, wt = _split_w_in(w_in)
    wa, wb, wo = w_branch_a.astype(BF16), w_branch_b.astype(BF16), w_out.astype(BF16)
    w1, w2 = w_ff1.astype(BF16), w_ff2.astype(BF16)
    g_mix = g_mix.reshape(depth, 1, D_MODEL)
    g_mlp = g_mlp.reshape(depth, 1, D_MODEL)

    for l in range(depth):
        (qat, ka, vat, qb0, kb0, vb0, qb1, kb1, vb1, qb2, kb2, vb2, ga, gb) = _in_proj(
            x, ada[l], g_mix, wn, wt, tabs, l)
        oa = _attn_a(qat, ka, vat)
        obs, lss = [], []
        for (window, dil), q, k, v in zip(B_GROUPS, (qb0, qb1, qb2), (kb0, kb1, kb2), (vb0, vb1, vb2)):
            o, lse = _attn_b(q, k, v, window, dil)
            obs.append(o)
            lss.append(lse)
        x = _merge(x, ada[l], oa, obs, lss, ga, gb, wa, wb, wo, l)
        x = _mlp(x, ada[l], g_mlp, w1, w2, l)
    return x
```

```python
import functools

import jax
import jax.numpy as jnp
import numpy as np
from jax import lax
from jax.experimental import pallas as pl
from jax.experimental.pallas import tpu as pltpu

D_MODEL = 1024
HEAD_DIM = 64
A_Q_HEADS = 8
A_KV_HEADS = 2
B_GROUPS = ((128, 1), (512, 4), (2048, 16))
B_HEADS = 4
N_B = len(B_GROUPS)
D_FF = 4 * D_MODEL
GRID_W = 64
ROPE_THETA = 10000.0
EPS = 1e-6

A_Q_W = A_Q_HEADS * HEAD_DIM
A_KV_W = A_KV_HEADS * HEAD_DIM
B_W = B_HEADS * HEAD_DIM

V7X_LANES = 128
V7X_VMEM_BYTES = 64 * 1024 * 1024
VMEM_LIMIT = 52 * 1024 * 1024

BF16 = jnp.bfloat16
F32 = jnp.float32

TM_IN = 256
TM_MERGE = 512
TM_MLP = 512
FF_CHUNK = 1024
TQ_B = 128
KW_B = 256
NEG_BIG = -1e30
LOG2_E = 1.4426950408889634

PERM_AXIAL = np.concatenate([np.arange(16, 32), np.arange(0, 16), np.arange(48, 64), np.arange(32, 48)])
PERM_SEQ = np.concatenate([np.arange(32, 64), np.arange(0, 32)])

T_SEGS = (("qa", A_Q_W), ("ka", A_KV_W), ("va", A_KV_W),
          ("qb0", B_W), ("kb0", B_W), ("qb1", B_W), ("kb1", B_W), ("qb2", B_W), ("kb2", B_W))
T_ROWS = sum(r for _, r in T_SEGS)
N_COLS = N_B * B_W + 2 * D_MODEL
N_TABS = 16


def _cparams(sem):
    return pltpu.CompilerParams(dimension_semantics=sem, vmem_limit_bytes=VMEM_LIMIT)


def _ada_kernel(c_ref, w_ref, b_ref, o_ref):
    c = c_ref[...]
    c_act = c * (1.0 / (1.0 + jnp.exp(-c)))
    o_ref[...] = jnp.dot(c_act, w_ref[...], preferred_element_type=F32,
                         precision=lax.Precision.HIGHEST) + b_ref[...]


def _ada_all(c, w_ada, b_ada):
    depth = w_ada.shape[0]
    nb = c.shape[0]
    return pl.pallas_call(
        _ada_kernel,
        grid=(depth, 6),
        in_specs=[
            pl.BlockSpec((nb, D_MODEL), lambda l, j: (0, 0)),
            pl.BlockSpec((None, D_MODEL, D_MODEL), lambda l, j: (l, 0, j)),
            pl.BlockSpec((None, 1, D_MODEL), lambda l, j: (l, 0, j)),
        ],
        out_specs=pl.BlockSpec((None, nb, D_MODEL), lambda l, j: (l, 0, j)),
        out_shape=jax.ShapeDtypeStruct((depth, nb, 6 * D_MODEL), F32),
        compiler_params=_cparams(("arbitrary", "arbitrary")),
        name="ada",
    )(c, w_ada, b_ada.reshape(depth, 1, 6 * D_MODEL))


def _modulated_rms(x, g, scale, shift):
    ms = jnp.mean(x * x, axis=-1, keepdims=True)
    return (x * lax.rsqrt(ms + EPS)) * g * (1.0 + scale) + shift


def _norm_rope_t(q, cos_t, sin_t, perm_blocks):
    ss = jnp.sum(q * q, axis=0, keepdims=True)
    r = lax.rsqrt(ss * (1.0 / HEAD_DIM) + EPS)
    qp = jnp.concatenate([q[a:b] for a, b in perm_blocks], axis=0)
    return (q * cos_t + qp * sin_t) * r


AXIAL_BLOCKS = ((16, 32), (0, 16), (48, 64), (32, 48))
SEQ_BLOCKS = ((32, 64), (0, 32))


def _store_class_major(out_ref, stage_ref, tile, dil):
    if dil == 1:
        out_ref[...] = tile.astype(BF16)
        return
    rows = tile.shape[0] // dil
    for half in range(B_W // V7X_LANES):
        stage_ref[half] = tile[:, half * V7X_LANES:(half + 1) * V7X_LANES]
    for r in range(dil):
        for half in range(B_W // V7X_LANES):
            col = r * B_W + half * V7X_LANES
            out_ref[:, col:col + V7X_LANES] = stage_ref[half, pl.ds(r, rows, stride=dil), :].astype(BF16)


def _in_kernel(x_ref, ada_ref, g_ref, wn_ref, wt_ref, tab_ref,
               qat_ref, ka_ref, vat_ref,
               qb0_ref, kb0_ref, vb0_ref, qb1_ref, kb1_ref, vb1_ref, qb2_ref, kb2_ref, vb2_ref,
               ga_ref, gb_ref, stage_q_ref, stage_k_ref, stage_v_ref):
    x = x_ref[...]
    h = _modulated_rms(x, g_ref[...], ada_ref[1:2, :], ada_ref[0:1, :]).astype(BF16)

    vb_refs = (vb0_ref, vb1_ref, vb2_ref)
    for g in range(N_B):
        v = jnp.dot(h, wn_ref[:, g * B_W:(g + 1) * B_W], preferred_element_type=F32)
        _store_class_major(vb_refs[g], stage_v_ref, v, B_GROUPS[g][1])
    for i, gate_ref in enumerate((ga_ref, gb_ref)):
        lo = N_B * B_W + i * D_MODEL
        z = jnp.dot(h, wn_ref[:, lo:lo + D_MODEL], preferred_element_type=F32)
        gate_ref[...] = (1.0 / (1.0 + jnp.exp(-z))).astype(BF16)

    nt = (((1,), (1,)), ((), ()))

    def proj_t(row0, rows):
        return lax.dot_general(wt_ref[row0:row0 + rows, :], h, nt, preferred_element_type=F32)

    def heads_t(rt, n_heads, tab0, blocks):
        cos_t = tab_ref[tab0 * HEAD_DIM:(tab0 + 1) * HEAD_DIM, :]
        sin_t = tab_ref[(tab0 + 1) * HEAD_DIM:(tab0 + 2) * HEAD_DIM, :]
        return [_norm_rope_t(rt[i * HEAD_DIM:(i + 1) * HEAD_DIM], cos_t, sin_t, blocks)
                for i in range(n_heads)]

    row = 0
    rt = proj_t(row, A_Q_W)
    qa = heads_t(rt, A_Q_HEADS, 0, AXIAL_BLOCKS)
    for i in range(A_Q_HEADS):
        qat_ref[i * HEAD_DIM:(i + 1) * HEAD_DIM, :] = qa[i].astype(BF16)
    row += A_Q_W

    rt = proj_t(row, 2 * A_KV_W)
    ka = heads_t(rt[:A_KV_W], A_KV_HEADS, 2, AXIAL_BLOCKS)
    ka_ref[...] = jnp.concatenate(ka, axis=0).T.astype(BF16)
    vat_ref[...] = rt[A_KV_W:].astype(BF16)
    row += 2 * A_KV_W

    qb_refs = (qb0_ref, qb1_ref, qb2_ref)
    kb_refs = (kb0_ref, kb1_ref, kb2_ref)
    for g in range(N_B):
        rt = proj_t(row, 2 * B_W)
        qb = heads_t(rt[:B_W], B_HEADS, 4 + 4 * g, SEQ_BLOCKS)
        kb = heads_t(rt[B_W:], B_HEADS, 6 + 4 * g, SEQ_BLOCKS)
        _store_class_major(qb_refs[g], stage_q_ref, jnp.concatenate(qb, axis=0).T, B_GROUPS[g][1])
        _store_class_major(kb_refs[g], stage_k_ref, jnp.concatenate(kb, axis=0).T, B_GROUPS[g][1])
        row += 2 * B_W


def _in_proj(x, ada_l, g_mix_l, wn, wt, tabs, l):
    nb, seq, _ = x.shape
    ns = seq // TM_IN
    tok = lambda w: pl.BlockSpec((None, TM_IN, w), lambda s, b: (b, s, 0))
    tchunk = lambda r: pl.BlockSpec((None, None, r, TM_IN), lambda s, b: (b, s, 0, 0))
    nat = lambda w: jax.ShapeDtypeStruct((nb, seq, w), BF16)
    out_specs = [tchunk(A_Q_W), tok(A_KV_W), tchunk(A_KV_W)]
    out_shape = [jax.ShapeDtypeStruct((nb, ns, A_Q_W, TM_IN), BF16), nat(A_KV_W),
                 jax.ShapeDtypeStruct((nb, ns, A_KV_W, TM_IN), BF16)]
    for _, dil in B_GROUPS:
        out_specs += [pl.BlockSpec((None, TM_IN // dil, dil * B_W), lambda s, b: (b, s, 0))] * 3
        out_shape += [jax.ShapeDtypeStruct((nb, seq // dil, dil * B_W), BF16)] * 3
    out_specs += [tok(D_MODEL)] * 2
    out_shape += [nat(D_MODEL)] * 2
    return pl.pallas_call(
        _in_kernel,
        grid=(ns, nb),
        in_specs=[
            tok(D_MODEL),
            pl.BlockSpec((None, 6, D_MODEL), lambda s, b: (b, 0, 0)),
            pl.BlockSpec((None, 1, D_MODEL), lambda s, b: (l, 0, 0)),
            pl.BlockSpec((None, D_MODEL, N_COLS), lambda s, b: (l, 0, 0), pipeline_mode=pl.Buffered(1)),
            pl.BlockSpec((None, T_ROWS, D_MODEL), lambda s, b: (l, 0, 0), pipeline_mode=pl.Buffered(1)),
            pl.BlockSpec((None, N_TABS * HEAD_DIM, TM_IN), lambda s, b: (l, 0, s)),
        ],
        out_specs=out_specs,
        out_shape=out_shape,
        scratch_shapes=[pltpu.VMEM((B_W // V7X_LANES, TM_IN, V7X_LANES), F32)] * 3,
        compiler_params=_cparams(("arbitrary", "arbitrary")),
        name="in_proj",
    )(x, ada_l, g_mix_l, wn, wt, tabs)


A_REP = A_Q_HEADS // A_KV_HEADS
ONES_ROWS = 16


def _attn_a_kernel(qt_ref, k_ref, vt_ref, o_ref, qpad_ref, acc_ref, m_ref, s_ref, smax_ref):
    n_chunks, _, tk = vt_ref.shape
    tq = qt_ref.shape[1]
    qpad_ref[...] = jnp.zeros(qpad_ref.shape, BF16)
    for g in range(A_KV_HEADS):
        for hh in range(A_REP):
            h = g * A_REP + hh
            qpad_ref[g, g * HEAD_DIM:(g + 1) * HEAD_DIM, hh * tq:(hh + 1) * tq] = (
                qt_ref[h * HEAD_DIM:(h + 1) * HEAD_DIM, :])
    m_ref[...] = jnp.full(m_ref.shape, NEG_BIG, F32)
    acc_ref[...] = jnp.zeros(acc_ref.shape, F32)
    ones = jnp.ones((ONES_ROWS, tk), BF16)

    def scores(c, slot, g):
        kc = k_ref[pl.ds(pl.multiple_of(c * tk, tk), tk), :]
        s = jnp.dot(kc, qpad_ref[g], preferred_element_type=F32)
        s_ref[slot, g] = s
        smax_ref[slot, g] = jnp.max(s, axis=0, keepdims=True)

    def consume(c, slot, g):
        m_prev = m_ref[g]
        m_new = jnp.maximum(m_prev, smax_ref[slot, g])
        alpha = jnp.exp2(m_prev - m_new)
        p = jnp.exp2(s_ref[slot, g] - m_new).astype(BF16)
        v_aug = jnp.concatenate([vt_ref[c, g * HEAD_DIM:(g + 1) * HEAD_DIM, :], ones], axis=0)
        acc_ref[g] = acc_ref[g] * alpha + jnp.dot(v_aug, p, preferred_element_type=F32)
        m_ref[g] = m_new

    for g in range(A_KV_HEADS):
        scores(0, 0, g)

    def pair_body(i, carry):
        c = 2 * i
        for slot in range(2):
            c_next = jnp.minimum(c + slot + 1, n_chunks - 1)
            for g in range(A_KV_HEADS):
                scores(c_next, 1 - slot, g)
                consume(c + slot, slot, g)
        return carry

    lax.fori_loop(0, n_chunks // 2, pair_body, 0)
    for j in range(A_Q_W // V7X_LANES):
        halves = []
        for h in (2 * j, 2 * j + 1):
            g, hh = divmod(h, A_REP)
            cols = slice(hh * tq, (hh + 1) * tq)
            halves.append(acc_ref[g, :HEAD_DIM, cols] / acc_ref[g, HEAD_DIM:HEAD_DIM + 1, cols])
        blk = jnp.concatenate(halves, axis=0)
        o_ref[:, j * V7X_LANES:(j + 1) * V7X_LANES] = blk.T.astype(BF16)


def _attn_a(qat, ka, vat):
    nb, ns, _, tq = qat.shape
    seq = ka.shape[1]
    wide = A_REP * tq
    return pl.pallas_call(
        _attn_a_kernel,
        grid=(nb, ns),
        in_specs=[
            pl.BlockSpec((None, None, A_Q_W, tq), lambda b, s: (b, s, 0, 0)),
            pl.BlockSpec((None, seq, A_KV_W), lambda b, s: (b, 0, 0)),
            pl.BlockSpec((None, ns, A_KV_W, tq), lambda b, s: (b, 0, 0, 0)),
        ],
        out_specs=pl.BlockSpec((None, tq, A_Q_W), lambda b, s: (b, s, 0)),
        out_shape=jax.ShapeDtypeStruct((nb, seq, A_Q_W), BF16),
        scratch_shapes=[pltpu.VMEM((A_KV_HEADS, A_KV_W, wide), BF16),
                        pltpu.VMEM((A_KV_HEADS, HEAD_DIM + ONES_ROWS, wide), F32),
                        pltpu.VMEM((A_KV_HEADS, 1, wide), F32),
                        pltpu.VMEM((2, A_KV_HEADS, tq, wide), F32),
                        pltpu.VMEM((2, A_KV_HEADS, 1, wide), F32)],
        compiler_params=_cparams(("arbitrary", "arbitrary")),
        name="attn_a",
    )(qat, ka, vat)


def _attn_b_kernel(q_ref, k_ref, v_ref, o_ref, lse_ref, *, half_window):
    tj = q_ref.shape[0]
    length = k_ref.shape[0]
    j_tile = pl.program_id(2) * tj
    lane = lax.broadcasted_iota(jnp.int32, (1, V7X_LANES), 1)
    first_head = lane < HEAD_DIM
    nt = (((1,), (1,)), ((), ()))
    for qi in range(tj // TQ_B):
        j0 = j_tile + qi * TQ_B
        ks = jnp.clip(j0 - half_window, 0, length - KW_B)
        ks = pl.multiple_of(ks, half_window)
        jq = j0 + lax.broadcasted_iota(jnp.int32, (TQ_B, 1), 0)
        jk = ks + lax.broadcasted_iota(jnp.int32, (1, KW_B), 1)
        valid = jnp.abs(jq - jk) <= half_window
        for pair in range(B_W // V7X_LANES):
            cols = slice(pair * V7X_LANES, (pair + 1) * V7X_LANES)
            q2 = q_ref[qi * TQ_B:(qi + 1) * TQ_B, cols]
            k2 = k_ref[pl.ds(ks, KW_B), cols]
            v2 = v_ref[pl.ds(ks, KW_B), cols]
            res = []
            for hh in range(2):
                keep = first_head if hh == 0 else jnp.logical_not(first_head)
                qm = jnp.where(keep, q2, jnp.zeros_like(q2))
                s = lax.dot_general(qm, k2, nt, preferred_element_type=F32)
                s = jnp.where(valid, s, NEG_BIG)
                m = jnp.max(s, axis=1, keepdims=True)
                p = jnp.exp(s - m)
                l = jnp.sum(p, axis=1, keepdims=True)
                o = jnp.dot(p.astype(BF16), v2, preferred_element_type=F32)
                res.append((o / l, m + jnp.log(l)))
            o_pair = jnp.where(first_head, res[0][0], res[1][0])
            lse_pair = jnp.where(first_head, res[0][1], res[1][1])
            o_ref[qi * TQ_B:(qi + 1) * TQ_B, cols] = o_pair.astype(BF16)
            lse_ref[qi * TQ_B:(qi + 1) * TQ_B, cols] = lse_pair


def _attn_b(q, k, v, window, dil):
    nb, length, _ = q.shape
    half_window = (window // 2) // dil
    assert half_window * 2 + TQ_B == KW_B and length >= KW_B
    tj = min(512, length)
    qspec = pl.BlockSpec((None, tj, B_W), lambda b, r, j: (b, j, r))
    kvspec = pl.BlockSpec((None, length, B_W), lambda b, r, j: (b, 0, r))
    return pl.pallas_call(
        functools.partial(_attn_b_kernel, half_window=half_window),
        grid=(nb, dil, length // tj),
        in_specs=[qspec, kvspec, kvspec],
        out_specs=[qspec, qspec],
        out_shape=[jax.ShapeDtypeStruct((nb, length, dil * B_W), BF16),
                   jax.ShapeDtypeStruct((nb, length, dil * B_W), F32)],
        compiler_params=_cparams(("arbitrary", "arbitrary", "arbitrary")),
        name=f"attn_b_d{dil}",
    )(q, k, v)


def _load_class_major(blk_ref, stage_ref, dil):
    if dil == 1:
        return blk_ref[...].astype(F32)
    rows = blk_ref.shape[0]
    halves = range(B_W // V7X_LANES)
    for r in range(dil):
        for half in halves:
            col = r * B_W + half * V7X_LANES
            stage_ref[half, pl.ds(r, rows, stride=dil), :] = blk_ref[:, col:col + V7X_LANES].astype(F32)
    return jnp.concatenate([stage_ref[half] for half in halves], axis=1)


def _merge_kernel(x_ref, ada_ref, oa_ref, ob0_ref, ob1_ref, ob2_ref, ls0_ref, ls1_ref, ls2_ref,
                  ga_ref, gb_ref, wa_ref, wb_ref, wo_ref, xo_ref, *stage_refs):
    ya = jnp.dot(oa_ref[...], wa_ref[...], preferred_element_type=F32)
    dils = [d for _, d in B_GROUPS]
    lses = [_load_class_major(r, stage_refs[2 * g], dils[g])
            for g, r in enumerate((ls0_ref, ls1_ref, ls2_ref))]
    outs = [_load_class_major(r, stage_refs[2 * g + 1], dils[g])
            for g, r in enumerate((ob0_ref, ob1_ref, ob2_ref))]
    top = jnp.maximum(jnp.maximum(lses[0], lses[1]), lses[2])
    es = [jnp.exp(v - top) for v in lses]
    den = es[0] + es[1] + es[2]
    ob = (es[0] * outs[0] + es[1] * outs[1] + es[2] * outs[2]) / den
    yb = jnp.dot(ob.astype(BF16), wb_ref[...], preferred_element_type=F32)
    mixed = ga_ref[...].astype(F32) * ya + gb_ref[...].astype(F32) * yb
    upd = jnp.dot(mixed.astype(BF16), wo_ref[...], preferred_element_type=F32)
    xo_ref[...] = x_ref[...] + ada_ref[2:3, :] * upd


def _merge(x, ada_l, oa, obs, lss, ga, gb, wa, wb, wo, l):
    nb, seq, _ = x.shape
    tok = lambda w: pl.BlockSpec((None, TM_MERGE, w), lambda b, s: (b, s, 0))
    wspec = lambda r, c: pl.BlockSpec((None, r, c), lambda b, s: (l, 0, 0), pipeline_mode=pl.Buffered(1))
    cls = [pl.BlockSpec((None, TM_MERGE // d, d * B_W), lambda b, s: (b, s, 0)) for _, d in B_GROUPS]
    return pl.pallas_call(
        _merge_kernel,
        grid=(nb, seq // TM_MERGE),
        in_specs=[tok(D_MODEL), pl.BlockSpec((None, 6, D_MODEL), lambda b, s: (b, 0, 0)), tok(A_Q_W)]
                 + cls + cls + [tok(D_MODEL)] * 2
                 + [wspec(A_Q_W, D_MODEL), wspec(B_W, D_MODEL), wspec(D_MODEL, D_MODEL)],
        out_specs=tok(D_MODEL),
        out_shape=jax.ShapeDtypeStruct(x.shape, F32),
        scratch_shapes=[pltpu.VMEM((B_W // V7X_LANES, TM_MERGE, V7X_LANES), F32)] * (2 * N_B),
        input_output_aliases={0: 0} if l > 0 else {},
        compiler_params=_cparams(("arbitrary", "arbitrary")),
        name="merge",
    )(x, ada_l, oa, *obs, *lss, ga, gb, wa, wb, wo)


def _mlp_kernel(x_ref, ada_ref, g_ref, w1_ref, w2_ref, xo_ref):
    x = x_ref[...]
    h = _modulated_rms(x, g_ref[...], ada_ref[4:5, :], ada_ref[3:4, :]).astype(BF16)
    acc = jnp.zeros(x.shape, F32)
    for c in range(D_FF // FF_CHUNK):
        a = jnp.dot(h, w1_ref[:, c * FF_CHUNK:(c + 1) * FF_CHUNK], preferred_element_type=F32)
        a = jnp.square(jnp.maximum(a, 0.0)).astype(BF16)
        acc = acc + jnp.dot(a, w2_ref[c * FF_CHUNK:(c + 1) * FF_CHUNK, :], preferred_element_type=F32)
    xo_ref[...] = x + ada_ref[5:6, :] * acc


def _mlp(x, ada_l, g_mlp_l, w1, w2, l):
    nb, seq, _ = x.shape
    tok = pl.BlockSpec((None, TM_MLP, D_MODEL), lambda b, s: (b, s, 0))
    return pl.pallas_call(
        _mlp_kernel,
        grid=(nb, seq // TM_MLP),
        in_specs=[
            tok,
            pl.BlockSpec((None, 6, D_MODEL), lambda b, s: (b, 0, 0)),
            pl.BlockSpec((None, 1, D_MODEL), lambda b, s: (l, 0, 0)),
            pl.BlockSpec((None, D_MODEL, D_FF), lambda b, s: (l, 0, 0), pipeline_mode=pl.Buffered(1)),
            pl.BlockSpec((None, D_FF, D_MODEL), lambda b, s: (l, 0, 0), pipeline_mode=pl.Buffered(1)),
        ],
        out_specs=tok,
        out_shape=jax.ShapeDtypeStruct(x.shape, F32),
        input_output_aliases={0: 0},
        compiler_params=_cparams(("arbitrary", "arbitrary")),
        name="mlp",
    )(x, ada_l, g_mlp_l, w1, w2)


def _rope_tables(seq):
    pos = jnp.arange(seq)
    rows = seq // GRID_W
    row_idx = jnp.broadcast_to(jnp.arange(rows)[:, None], (rows, GRID_W)).reshape(-1)
    col_idx = jnp.broadcast_to(jnp.arange(GRID_W)[None, :], (rows, GRID_W)).reshape(-1)

    def cos_sin(p, dim):
        inv = ROPE_THETA ** (-jnp.arange(0, dim, 2, dtype=F32) / dim)
        ang = p.astype(F32)[:, None] * inv[None, :]
        return jnp.cos(ang), jnp.sin(ang)

    rc, rs = cos_sin(row_idx, HEAD_DIM // 2)
    cc, cs = cos_sin(col_idx, HEAD_DIM // 2)
    sc, ss = cos_sin(pos, HEAD_DIM)
    cos_ax = jnp.concatenate([rc, rc, cc, cc], axis=-1)
    sin_ax = jnp.concatenate([-rs, rs, -cs, cs], axis=-1)
    cos_sq = jnp.concatenate([sc, sc], axis=-1)
    sin_sq = jnp.concatenate([-ss, ss], axis=-1)
    return (cos_ax, sin_ax), (cos_sq, sin_sq)


def _gain_tables(q_norm_a, k_norm_a, q_norm_b, k_norm_b, seq):
    (cos_ax, sin_ax), (cos_sq, sin_sq) = _rope_tables(seq)
    scale = HEAD_DIM ** -0.5

    def pair(gain, cos, sin, perm, mult):
        c = gain[:, None, :] * cos[None] * mult
        s = gain[:, perm][:, None, :] * sin[None] * mult
        return [jnp.swapaxes(c, 1, 2), jnp.swapaxes(s, 1, 2)]

    tabs = pair(q_norm_a, cos_ax, sin_ax, PERM_AXIAL, scale * LOG2_E) + pair(k_norm_a, cos_ax, sin_ax, PERM_AXIAL, 1.0)
    for g in range(N_B):
        tabs += pair(q_norm_b[:, g], cos_sq, sin_sq, PERM_SEQ, scale)
        tabs += pair(k_norm_b[:, g], cos_sq, sin_sq, PERM_SEQ, 1.0)
    return jnp.concatenate(tabs, axis=1)


def _split_w_in(w_in):
    sizes = [A_Q_W, A_KV_W, A_KV_W] + [B_W] * (3 * N_B) + [D_MODEL, D_MODEL]
    offs = np.concatenate([[0], np.cumsum(sizes)])
    col = lambda i: w_in[:, :, offs[i]:offs[i + 1]]
    names = ["qa", "ka", "va"] + [f"{t}b{g}" for g in range(N_B) for t in "qkv"] + ["ga", "gb"]
    parts = {n: col(i) for i, n in enumerate(names)}
    wn = jnp.concatenate([parts["vb0"], parts["vb1"], parts["vb2"], parts["ga"], parts["gb"]], axis=-1)
    wt = jnp.concatenate([parts[n] for n, _ in T_SEGS], axis=-1)
    return wn.astype(BF16), jnp.swapaxes(wt, 1, 2).astype(BF16)


def kernel(x, c, w_ada, b_ada, g_mix, g_mlp, w_in, q_norm_a, k_norm_a, q_norm_b, k_norm_b,
           w_branch_a, w_branch_b, w_out, w_ff1, w_ff2):
    nb, seq, _ = x.shape
    depth = w_in.shape[0]
    assert seq % GRID_W == 0 and seq % TM_MERGE == 0 and seq % TM_IN == 0

    ada = _ada_all(c, w_ada, b_ada).reshape(depth, nb, 6, D_MODEL)
    tabs = _gain_tables(q_norm_a, k_norm_a, q_norm_b, k_norm_b, seq)
    wn, wt = _split_w_in(w_in)
    wa, wb, wo = w_branch_a.astype(BF16), w_branch_b.astype(BF16), w_out.astype(BF16)
    w1, w2 = w_ff1.astype(BF16), w_ff2.astype(BF16)
    g_mix = g_mix.reshape(depth, 1, D_MODEL)
    g_mlp = g_mlp.reshape(depth, 1, D_MODEL)

    for l in range(depth):
        (qat, ka, vat, qb0, kb0, vb0, qb1, kb1, vb1, qb2, kb2, vb2, ga, gb) = _in_proj(
            x, ada[l], g_mix, wn, wt, tabs, l)
        oa = _attn_a(qat, ka, vat)
        obs, lss = [], []
        for (window, dil), q, k, v in zip(B_GROUPS, (qb0, qb1, qb2), (kb0, kb1, kb2), (vb0, vb1, vb2)):
            o, lse = _attn_b(q, k, v, window, dil)
            obs.append(o)
            lss.append(lse)
        x = _merge(x, ada[l], oa, obs, lss, ga, gb, wa, wb, wo, l)
        x = _mlp(x, ada[l], g_mlp, w1, w2, l)
    return x
```

```python
import functools

import jax
import jax.numpy as jnp
import numpy as np
from jax import lax
from jax.experimental import pallas as pl
from jax.experimental.pallas import tpu as pltpu

D_MODEL = 1024
HEAD_DIM = 64
A_Q_HEADS = 8
A_KV_HEADS = 2
B_GROUPS = ((128, 1), (512, 4), (2048, 16))
B_HEADS = 4
N_B = len(B_GROUPS)
D_FF = 4 * D_MODEL
GRID_W = 64
ROPE_THETA = 10000.0
EPS = 1e-6

A_Q_W = A_Q_HEADS * HEAD_DIM
A_KV_W = A_KV_HEADS * HEAD_DIM
B_W = B_HEADS * HEAD_DIM

V7X_LANES = 128
V7X_VMEM_BYTES = 64 * 1024 * 1024
VMEM_LIMIT = 52 * 1024 * 1024

BF16 = jnp.bfloat16
F32 = jnp.float32

TM_IN = 256
TM_MERGE = 512
TM_MLP = 512
FF_CHUNK = 1024
TQ_B = 128
KW_B = 256
NEG_BIG = -1e30
LOG2_E = 1.4426950408889634

PERM_AXIAL = np.concatenate([np.arange(16, 32), np.arange(0, 16), np.arange(48, 64), np.arange(32, 48)])
PERM_SEQ = np.concatenate([np.arange(32, 64), np.arange(0, 32)])

T_SEGS = (("qa", A_Q_W), ("ka", A_KV_W), ("va", A_KV_W),
          ("qb0", B_W), ("kb0", B_W), ("qb1", B_W), ("kb1", B_W), ("qb2", B_W), ("kb2", B_W))
T_ROWS = sum(r for _, r in T_SEGS)
N_COLS = N_B * B_W + 2 * D_MODEL
N_TABS = 16


def _cparams(sem):
    return pltpu.CompilerParams(dimension_semantics=sem, vmem_limit_bytes=VMEM_LIMIT)


def _ada_kernel(c_ref, w_ref, b_ref, o_ref):
    c = c_ref[...]
    c_act = c * (1.0 / (1.0 + jnp.exp(-c)))
    o_ref[...] = jnp.dot(c_act, w_ref[...], preferred_element_type=F32,
                         precision=lax.Precision.HIGHEST) + b_ref[...]


def _ada_all(c, w_ada, b_ada):
    depth = w_ada.shape[0]
    nb = c.shape[0]
    return pl.pallas_call(
        _ada_kernel,
        grid=(depth, 6),
        in_specs=[
            pl.BlockSpec((nb, D_MODEL), lambda l, j: (0, 0)),
            pl.BlockSpec((None, D_MODEL, D_MODEL), lambda l, j: (l, 0, j)),
            pl.BlockSpec((None, 1, D_MODEL), lambda l, j: (l, 0, j)),
        ],
        out_specs=pl.BlockSpec((None, nb, D_MODEL), lambda l, j: (l, 0, j)),
        out_shape=jax.ShapeDtypeStruct((depth, nb, 6 * D_MODEL), F32),
        compiler_params=_cparams(("arbitrary", "arbitrary")),
        name="ada",
    )(c, w_ada, b_ada.reshape(depth, 1, 6 * D_MODEL))


def _modulated_rms(x, g, scale, shift):
    ms = jnp.mean(x * x, axis=-1, keepdims=True)
    return (x * lax.rsqrt(ms + EPS)) * g * (1.0 + scale) + shift


def _norm_rope_t(q, cos_t, sin_t, perm_blocks):
    ss = jnp.sum(q * q, axis=0, keepdims=True)
    r = lax.rsqrt(ss * (1.0 / HEAD_DIM) + EPS)
    qp = jnp.concatenate([q[a:b] for a, b in perm_blocks], axis=0)
    return (q * cos_t + qp * sin_t) * r


AXIAL_BLOCKS = ((16, 32), (0, 16), (48, 64), (32, 48))
SEQ_BLOCKS = ((32, 64), (0, 32))


def _store_class_major(out_ref, stage_ref, tile, dil):
    if dil == 1:
        out_ref[...] = tile.astype(BF16)
        return
    rows = tile.shape[0] // dil
    for half in range(B_W // V7X_LANES):
        stage_ref[half] = tile[:, half * V7X_LANES:(half + 1) * V7X_LANES]
    for r in range(dil):
        for half in range(B_W // V7X_LANES):
            col = r * B_W + half * V7X_LANES
            out_ref[:, col:col + V7X_LANES] = stage_ref[half, pl.ds(r, rows, stride=dil), :].astype(BF16)


def _in_kernel(x_ref, ada_ref, g_ref, wn_ref, wt_ref, tab_ref,
               qat_ref, ka_ref, vat_ref,
               qb0_ref, kb0_ref, vb0_ref, qb1_ref, kb1_ref, vb1_ref, qb2_ref, kb2_ref, vb2_ref,
               ga_ref, gb_ref, stage_q_ref, stage_k_ref, stage_v_ref):
    x = x_ref[...]
    h = _modulated_rms(x, g_ref[...], ada_ref[1:2, :], ada_ref[0:1, :]).astype(BF16)

    nt = (((1,), (1,)), ((), ()))

    def proj_t(row0, rows):
        return lax.dot_general(wt_ref[row0:row0 + rows, :], h, nt, preferred_element_type=F32)

    def heads_t(rt, n_heads, tab0, blocks):
        cos_t = tab_ref[tab0 * HEAD_DIM:(tab0 + 1) * HEAD_DIM, :]
        sin_t = tab_ref[(tab0 + 1) * HEAD_DIM:(tab0 + 2) * HEAD_DIM, :]
        return [_norm_rope_t(rt[i * HEAD_DIM:(i + 1) * HEAD_DIM], cos_t, sin_t, blocks)
                for i in range(n_heads)]

    row = 0
    rt = proj_t(row, A_Q_W)
    qa = heads_t(rt, A_Q_HEADS, 0, AXIAL_BLOCKS)
    for i in range(A_Q_HEADS):
        qat_ref[i * HEAD_DIM:(i + 1) * HEAD_DIM, :] = qa[i].astype(BF16)
    row += A_Q_W

    rt = proj_t(row, 2 * A_KV_W)
    ka = heads_t(rt[:A_KV_W], A_KV_HEADS, 2, AXIAL_BLOCKS)
    ka_ref[...] = jnp.concatenate(ka, axis=0).T.astype(BF16)
    vat_ref[...] = rt[A_KV_W:].astype(BF16)
    row += 2 * A_KV_W

    qb_refs = (qb0_ref, qb1_ref, qb2_ref)
    kb_refs = (kb0_ref, kb1_ref, kb2_ref)
    for g in range(N_B):
        rt = proj_t(row, 2 * B_W)
        qb = heads_t(rt[:B_W], B_HEADS, 4 + 4 * g, SEQ_BLOCKS)
        kb = heads_t(rt[B_W:], B_HEADS, 6 + 4 * g, SEQ_BLOCKS)
        _store_class_major(qb_refs[g], stage_q_ref, jnp.concatenate(qb, axis=0).T, B_GROUPS[g][1])
        _store_class_major(kb_refs[g], stage_k_ref, jnp.concatenate(kb, axis=0).T, B_GROUPS[g][1])
        row += 2 * B_W

    for i, gate_ref in enumerate((ga_ref, gb_ref)):
        lo = N_B * B_W + i * D_MODEL
        z = jnp.dot(h, wn_ref[:, lo:lo + D_MODEL], preferred_element_type=F32)
        gate_ref[...] = (1.0 / (1.0 + jnp.exp(-z))).astype(BF16)
    vb_refs = (vb0_ref, vb1_ref, vb2_ref)
    for g in reversed(range(N_B)):
        v = jnp.dot(h, wn_ref[:, g * B_W:(g + 1) * B_W], preferred_element_type=F32)
        _store_class_major(vb_refs[g], stage_v_ref, v, B_GROUPS[g][1])


def _in_proj(x, ada_l, g_mix_l, wn, wt, tabs, l):
    nb, seq, _ = x.shape
    ns = seq // TM_IN
    tok = lambda w: pl.BlockSpec((None, TM_IN, w), lambda s, b: (b, s, 0))
    tchunk = lambda r: pl.BlockSpec((None, None, r, TM_IN), lambda s, b: (b, s, 0, 0))
    nat = lambda w: jax.ShapeDtypeStruct((nb, seq, w), BF16)
    out_specs = [tchunk(A_Q_W), tok(A_KV_W), tchunk(A_KV_W)]
    out_shape = [jax.ShapeDtypeStruct((nb, ns, A_Q_W, TM_IN), BF16), nat(A_KV_W),
                 jax.ShapeDtypeStruct((nb, ns, A_KV_W, TM_IN), BF16)]
    for _, dil in B_GROUPS:
        out_specs += [pl.BlockSpec((None, TM_IN // dil, dil * B_W), lambda s, b: (b, s, 0))] * 3
        out_shape += [jax.ShapeDtypeStruct((nb, seq // dil, dil * B_W), BF16)] * 3
    out_specs += [tok(D_MODEL)] * 2
    out_shape += [nat(D_MODEL)] * 2
    return pl.pallas_call(
        _in_kernel,
        grid=(ns, nb),
        in_specs=[
            tok(D_MODEL),
            pl.BlockSpec((None, 6, D_MODEL), lambda s, b: (b, 0, 0)),
            pl.BlockSpec((None, 1, D_MODEL), lambda s, b: (l, 0, 0)),
            pl.BlockSpec((None, D_MODEL, N_COLS), lambda s, b: (l, 0, 0), pipeline_mode=pl.Buffered(1)),
            pl.BlockSpec((None, T_ROWS, D_MODEL), lambda s, b: (l, 0, 0), pipeline_mode=pl.Buffered(1)),
            pl.BlockSpec((None, N_TABS * HEAD_DIM, TM_IN), lambda s, b: (l, 0, s)),
        ],
        out_specs=out_specs,
        out_shape=out_shape,
        scratch_shapes=[pltpu.VMEM((B_W // V7X_LANES, TM_IN, V7X_LANES), F32)] * 3,
        compiler_params=_cparams(("arbitrary", "arbitrary")),
        name="in_proj",
    )(x, ada_l, g_mix_l, wn, wt, tabs)


A_REP = A_Q_HEADS // A_KV_HEADS
ONES_ROWS = 16


def _attn_a_kernel(qt_ref, k_ref, vt_ref, o_ref, qpad_ref, acc_ref, m_ref, s_ref, smax_ref):
    n_chunks, _, tk = vt_ref.shape
    tq = qt_ref.shape[1]
    qpad_ref[...] = jnp.zeros(qpad_ref.shape, BF16)
    for g in range(A_KV_HEADS):
        for hh in range(A_REP):
            h = g * A_REP + hh
            qpad_ref[g, g * HEAD_DIM:(g + 1) * HEAD_DIM, hh * tq:(hh + 1) * tq] = (
                qt_ref[h * HEAD_DIM:(h + 1) * HEAD_DIM, :])
    m_ref[...] = jnp.full(m_ref.shape, NEG_BIG, F32)
    acc_ref[...] = jnp.zeros(acc_ref.shape, F32)
    ones = jnp.ones((ONES_ROWS, tk), BF16)

    def scores(c, slot, g):
        kc = k_ref[pl.ds(pl.multiple_of(c * tk, tk), tk), :]
        s = jnp.dot(kc, qpad_ref[g], preferred_element_type=F32)
        s_ref[slot, g] = s
        smax_ref[slot, g] = jnp.max(s, axis=0, keepdims=True)

    def consume(c, slot, g):
        m_prev = m_ref[g]
        m_new = jnp.maximum(m_prev, smax_ref[slot, g])
        alpha = jnp.exp2(m_prev - m_new)
        p = jnp.exp2(s_ref[slot, g] - m_new).astype(BF16)
        v_aug = jnp.concatenate([vt_ref[c, g * HEAD_DIM:(g + 1) * HEAD_DIM, :], ones], axis=0)
        acc_ref[g] = acc_ref[g] * alpha + jnp.dot(v_aug, p, preferred_element_type=F32)
        m_ref[g] = m_new

    for g in range(A_KV_HEADS):
        scores(0, 0, g)

    def chunk_pair(c, last):
        for slot in range(2):
            for g in range(A_KV_HEADS):
                if not (last and slot == 1):
                    scores(c + slot + 1, 1 - slot, g)
                consume(c + slot, slot, g)

    def pair_body(i, carry):
        chunk_pair(2 * i, False)
        return carry

    lax.fori_loop(0, n_chunks // 2 - 1, pair_body, 0)
    chunk_pair(n_chunks - 2, True)
    for j in range(A_Q_W // V7X_LANES):
        halves = []
        for h in (2 * j, 2 * j + 1):
            g, hh = divmod(h, A_REP)
            cols = slice(hh * tq, (hh + 1) * tq)
            halves.append(acc_ref[g, :HEAD_DIM, cols] / acc_ref[g, HEAD_DIM:HEAD_DIM + 1, cols])
        blk = jnp.concatenate(halves, axis=0)
        o_ref[:, j * V7X_LANES:(j + 1) * V7X_LANES] = blk.T.astype(BF16)


def _attn_a(qat, ka, vat):
    nb, ns, _, tq = qat.shape
    seq = ka.shape[1]
    wide = A_REP * tq
    return pl.pallas_call(
        _attn_a_kernel,
        grid=(nb, ns),
        in_specs=[
            pl.BlockSpec((None, None, A_Q_W, tq), lambda b, s: (b, s, 0, 0)),
            pl.BlockSpec((None, seq, A_KV_W), lambda b, s: (b, 0, 0)),
            pl.BlockSpec((None, ns, A_KV_W, tq), lambda b, s: (b, 0, 0, 0)),
        ],
        out_specs=pl.BlockSpec((None, tq, A_Q_W), lambda b, s: (b, s, 0)),
        out_shape=jax.ShapeDtypeStruct((nb, seq, A_Q_W), BF16),
        scratch_shapes=[pltpu.VMEM((A_KV_HEADS, A_KV_W, wide), BF16),
                        pltpu.VMEM((A_KV_HEADS, HEAD_DIM + ONES_ROWS, wide), F32),
                        pltpu.VMEM((A_KV_HEADS, 1, wide), F32),
                        pltpu.VMEM((2, A_KV_HEADS, tq, wide), F32),
                        pltpu.VMEM((2, A_KV_HEADS, 1, wide), F32)],
        compiler_params=_cparams(("arbitrary", "arbitrary")),
        name="attn_a",
    )(qat, ka, vat)


def _attn_b_kernel(q_ref, k_ref, v_ref, o_ref, lse_ref, *, half_window):
    tj = q_ref.shape[0]
    length = k_ref.shape[0]
    j_tile = pl.program_id(2) * tj
    lane = lax.broadcasted_iota(jnp.int32, (1, V7X_LANES), 1)
    first_head = lane < HEAD_DIM
    nt = (((1,), (1,)), ((), ()))
    ones = jnp.ones((KW_B, V7X_LANES), BF16)
    for qi in range(tj // TQ_B):
        j0 = j_tile + qi * TQ_B
        ks = jnp.clip(j0 - half_window, 0, length - KW_B)
        ks = pl.multiple_of(ks, half_window)
        jq = j0 + lax.broadcasted_iota(jnp.int32, (TQ_B, 1), 0)
        jk = ks + lax.broadcasted_iota(jnp.int32, (1, KW_B), 1)
        bias = jnp.where(jnp.abs(jq - jk) <= half_window, 0.0, NEG_BIG)
        for pair in range(B_W // V7X_LANES):
            cols = slice(pair * V7X_LANES, (pair + 1) * V7X_LANES)
            q2 = q_ref[qi * TQ_B:(qi + 1) * TQ_B, cols]
            k2 = k_ref[pl.ds(ks, KW_B), cols]
            v_aug = jnp.concatenate([v_ref[pl.ds(ks, KW_B), cols], ones], axis=1)
            ms, ols = [], []
            for hh in range(2):
                keep = first_head if hh == 0 else jnp.logical_not(first_head)
                qm = jnp.where(keep, q2, jnp.zeros_like(q2))
                s = lax.dot_general(qm, k2, nt, preferred_element_type=F32) + bias
                m = jnp.max(s, axis=1, keepdims=True)
                p = jnp.exp2(s - m).astype(BF16)
                ols.append(jnp.dot(p, v_aug, preferred_element_type=F32))
                ms.append(m)
            o_pair = jnp.where(first_head, ols[0][:, :V7X_LANES], ols[1][:, :V7X_LANES])
            l_pair = jnp.where(first_head, ols[0][:, V7X_LANES:], ols[1][:, V7X_LANES:])
            m_pair = jnp.where(first_head, ms[0], ms[1])
            o_ref[qi * TQ_B:(qi + 1) * TQ_B, cols] = (o_pair / l_pair).astype(BF16)
            lse_ref[qi * TQ_B:(qi + 1) * TQ_B, cols] = m_pair + jnp.log2(l_pair)


def _attn_b(q, k, v, window, dil):
    nb, length, _ = q.shape
    half_window = (window // 2) // dil
    assert half_window * 2 + TQ_B == KW_B and length >= KW_B
    tj = min(512, length)
    qspec = pl.BlockSpec((None, tj, B_W), lambda b, r, j: (b, j, r))
    kvspec = pl.BlockSpec((None, length, B_W), lambda b, r, j: (b, 0, r))
    return pl.pallas_call(
        functools.partial(_attn_b_kernel, half_window=half_window),
        grid=(nb, dil, length // tj),
        in_specs=[qspec, kvspec, kvspec],
        out_specs=[qspec, qspec],
        out_shape=[jax.ShapeDtypeStruct((nb, length, dil * B_W), BF16),
                   jax.ShapeDtypeStruct((nb, length, dil * B_W), F32)],
        compiler_params=_cparams(("arbitrary", "arbitrary", "arbitrary")),
        name=f"attn_b_d{dil}",
    )(q, k, v)


def _load_class_major(blk_ref, stage_ref, dil):
    if dil == 1:
        return blk_ref[...].astype(F32)
    rows = blk_ref.shape[0]
    halves = range(B_W // V7X_LANES)
    for r in range(dil):
        for half in halves:
            col = r * B_W + half * V7X_LANES
            stage_ref[half, pl.ds(r, rows, stride=dil), :] = blk_ref[:, col:col + V7X_LANES].astype(F32)
    return jnp.concatenate([stage_ref[half] for half in halves], axis=1)


def _merge_kernel(x_ref, ada_ref, oa_ref, ob0_ref, ob1_ref, ob2_ref, ls0_ref, ls1_ref, ls2_ref,
                  ga_ref, gb_ref, wa_ref, wb_ref, wo_ref, xo_ref, *stage_refs):
    ya = jnp.dot(oa_ref[...], wa_ref[...], preferred_element_type=F32)
    dils = [d for _, d in B_GROUPS]
    lses = [_load_class_major(r, stage_refs[2 * g], dils[g])
            for g, r in enumerate((ls0_ref, ls1_ref, ls2_ref))]
    outs = [_load_class_major(r, stage_refs[2 * g + 1], dils[g])
            for g, r in enumerate((ob0_ref, ob1_ref, ob2_ref))]
    top = jnp.maximum(jnp.maximum(lses[0], lses[1]), lses[2])
    es = [jnp.exp2(v - top) for v in lses]
    den = es[0] + es[1] + es[2]
    ob = (es[0] * outs[0] + es[1] * outs[1] + es[2] * outs[2]) / den
    yb = jnp.dot(ob.astype(BF16), wb_ref[...], preferred_element_type=F32)
    mixed = ga_ref[...].astype(F32) * ya + gb_ref[...].astype(F32) * yb
    upd = jnp.dot(mixed.astype(BF16), wo_ref[...], preferred_element_type=F32)
    xo_ref[...] = x_ref[...] + ada_ref[2:3, :] * upd


def _merge(x, ada_l, oa, obs, lss, ga, gb, wa, wb, wo, l):
    nb, seq, _ = x.shape
    tok = lambda w: pl.BlockSpec((None, TM_MERGE, w), lambda b, s: (b, s, 0))
    wspec = lambda r, c: pl.BlockSpec((None, r, c), lambda b, s: (l, 0, 0), pipeline_mode=pl.Buffered(1))
    cls = [pl.BlockSpec((None, TM_MERGE // d, d * B_W), lambda b, s: (b, s, 0)) for _, d in B_GROUPS]
    return pl.pallas_call(
        _merge_kernel,
        grid=(nb, seq // TM_MERGE),
        in_specs=[tok(D_MODEL), pl.BlockSpec((None, 6, D_MODEL), lambda b, s: (b, 0, 0)), tok(A_Q_W)]
                 + cls + cls + [tok(D_MODEL)] * 2
                 + [wspec(A_Q_W, D_MODEL), wspec(B_W, D_MODEL), wspec(D_MODEL, D_MODEL)],
        out_specs=tok(D_MODEL),
        out_shape=jax.ShapeDtypeStruct(x.shape, F32),
        scratch_shapes=[pltpu.VMEM((B_W // V7X_LANES, TM_MERGE, V7X_LANES), F32)] * (2 * N_B),
        input_output_aliases={0: 0} if l > 0 else {},
        compiler_params=_cparams(("arbitrary", "arbitrary")),
        name="merge",
    )(x, ada_l, oa, *obs, *lss, ga, gb, wa, wb, wo)


def _mlp_kernel(x_ref, ada_ref, g_ref, w1_ref, w2_ref, xo_ref):
    x = x_ref[...]
    h = _modulated_rms(x, g_ref[...], ada_ref[4:5, :], ada_ref[3:4, :]).astype(BF16)
    acc = jnp.zeros(x.shape, F32)
    for c in range(D_FF // FF_CHUNK):
        a = jnp.dot(h, w1_ref[:, c * FF_CHUNK:(c + 1) * FF_CHUNK], preferred_element_type=F32)
        a = jnp.square(jnp.maximum(a, 0.0)).astype(BF16)
        acc = acc + jnp.dot(a, w2_ref[c * FF_CHUNK:(c + 1) * FF_CHUNK, :], preferred_element_type=F32)
    xo_ref[...] = x + ada_ref[5:6, :] * acc


def _mlp(x, ada_l, g_mlp_l, w1, w2, l):
    nb, seq, _ = x.shape
    tok = pl.BlockSpec((None, TM_MLP, D_MODEL), lambda b, s: (b, s, 0))
    return pl.pallas_call(
        _mlp_kernel,
        grid=(nb, seq // TM_MLP),
        in_specs=[
            tok,
            pl.BlockSpec((None, 6, D_MODEL), lambda b, s: (b, 0, 0)),
            pl.BlockSpec((None, 1, D_MODEL), lambda b, s: (l, 0, 0)),
            pl.BlockSpec((None, D_MODEL, D_FF), lambda b, s: (l, 0, 0), pipeline_mode=pl.Buffered(1)),
            pl.BlockSpec((None, D_FF, D_MODEL), lambda b, s: (l, 0, 0), pipeline_mode=pl.Buffered(1)),
        ],
        out_specs=tok,
        out_shape=jax.ShapeDtypeStruct(x.shape, F32),
        input_output_aliases={0: 0},
        compiler_params=_cparams(("arbitrary", "arbitrary")),
        name="mlp",
    )(x, ada_l, g_mlp_l, w1, w2)


def _rope_tables(seq):
    pos = jnp.arange(seq)
    rows = seq // GRID_W
    row_idx = jnp.broadcast_to(jnp.arange(rows)[:, None], (rows, GRID_W)).reshape(-1)
    col_idx = jnp.broadcast_to(jnp.arange(GRID_W)[None, :], (rows, GRID_W)).reshape(-1)

    def cos_sin(p, dim):
        inv = ROPE_THETA ** (-jnp.arange(0, dim, 2, dtype=F32) / dim)
        ang = p.astype(F32)[:, None] * inv[None, :]
        return jnp.cos(ang), jnp.sin(ang)

    rc, rs = cos_sin(row_idx, HEAD_DIM // 2)
    cc, cs = cos_sin(col_idx, HEAD_DIM // 2)
    sc, ss = cos_sin(pos, HEAD_DIM)
    cos_ax = jnp.concatenate([rc, rc, cc, cc], axis=-1)
    sin_ax = jnp.concatenate([-rs, rs, -cs, cs], axis=-1)
    cos_sq = jnp.concatenate([sc, sc], axis=-1)
    sin_sq = jnp.concatenate([-ss, ss], axis=-1)
    return (cos_ax, sin_ax), (cos_sq, sin_sq)


def _gain_tables(q_norm_a, k_norm_a, q_norm_b, k_norm_b, seq):
    (cos_ax, sin_ax), (cos_sq, sin_sq) = _rope_tables(seq)
    scale = HEAD_DIM ** -0.5 * LOG2_E

    def pair(gain, cos, sin, perm, mult):
        c = gain[:, None, :] * cos[None] * mult
        s = gain[:, perm][:, None, :] * sin[None] * mult
        return [jnp.swapaxes(c, 1, 2), jnp.swapaxes(s, 1, 2)]

    tabs = pair(q_norm_a, cos_ax, sin_ax, PERM_AXIAL, scale) + pair(k_norm_a, cos_ax, sin_ax, PERM_AXIAL, 1.0)
    for g in range(N_B):
        tabs += pair(q_norm_b[:, g], cos_sq, sin_sq, PERM_SEQ, scale)
        tabs += pair(k_norm_b[:, g], cos_sq, sin_sq, PERM_SEQ, 1.0)
    return jnp.concatenate(tabs, axis=1)


def _split_w_in(w_in):
    sizes = [A_Q_W, A_KV_W, A_KV_W] + [B_W] * (3 * N_B) + [D_MODEL, D_MODEL]
    offs = np.concatenate([[0], np.cumsum(sizes)])
    col = lambda i: w_in[:, :, offs[i]:offs[i + 1]]
    names = ["qa", "ka", "va"] + [f"{t}b{g}" for g in range(N_B) for t in "qkv"] + ["ga", "gb"]
    parts = {n: col(i) for i, n in enumerate(names)}
    wn = jnp.concatenate([parts["vb0"], parts["vb1"], parts["vb2"], parts["ga"], parts["gb"]], axis=-1)
    wt = jnp.concatenate([parts[n] for n, _ in T_SEGS], axis=-1)
    return wn.astype(BF16), jnp.swapaxes(wt, 1, 2).astype(BF16)


def kernel(x, c, w_ada, b_ada, g_mix, g_mlp, w_in, q_norm_a, k_norm_a, q_norm_b, k_norm_b,
           w_branch_a, w_branch_b, w_out, w_ff1, w_ff2):
    nb, seq, _ = x.shape
    depth = w_in.shape[0]
    assert seq % GRID_W == 0 and seq % TM_MERGE == 0 and seq % TM_IN == 0

    ada = _ada_all(c, w_ada, b_ada).reshape(depth, nb, 6, D_MODEL)
    tabs = _gain_tables(q_norm_a, k_norm_a, q_norm_b, k_norm_b, seq)
    wn, wt = _split_w_in(w_in)
    wa, wb, wo = w_branch_a.astype(BF16), w_branch_b.astype(BF16), w_out.astype(BF16)
    w1, w2 = w_ff1.astype(BF16), w_ff2.astype(BF16)
    g_mix = g_mix.reshape(depth, 1, D_MODEL)
    g_mlp = g_mlp.reshape(depth, 1, D_MODEL)

    for l in range(depth):
        (qat, ka, vat, qb0, kb0, vb0, qb1, kb1, vb1, qb2, kb2, vb2, ga, gb) = _in_proj(
            x, ada[l], g_mix, wn, wt, tabs, l)
        oa = _attn_a(qat, ka, vat)
        obs, lss = [], []
        for (window, dil), q, k, v in zip(B_GROUPS, (qb0, qb1, qb2), (kb0, kb1, kb2), (vb0, vb1, vb2)):
            o, lse = _attn_b(q, k, v, window, dil)
            obs.append(o)
            lss.append(lse)
        x = _merge(x, ada[l], oa, obs, lss, ga, gb, wa, wb, wo, l)
        x = _mlp(x, ada[l], g_mlp, w1, w2, l)
    return x
```

```python
import functools

import jax
import jax.numpy as jnp
import numpy as np
from jax import lax
from jax.experimental import pallas as pl
from jax.experimental.pallas import tpu as pltpu

D_MODEL = 1024
HEAD_DIM = 64
A_Q_HEADS = 8
A_KV_HEADS = 2
B_GROUPS = ((128, 1), (512, 4), (2048, 16))
B_HEADS = 4
N_B = len(B_GROUPS)
D_FF = 4 * D_MODEL
GRID_W = 64
ROPE_THETA = 10000.0
EPS = 1e-6

A_Q_W = A_Q_HEADS * HEAD_DIM
A_KV_W = A_KV_HEADS * HEAD_DIM
B_W = B_HEADS * HEAD_DIM

V7X_LANES = 128
V7X_VMEM_BYTES = 64 * 1024 * 1024
VMEM_LIMIT = 52 * 1024 * 1024

BF16 = jnp.bfloat16
F32 = jnp.float32

CHUNK = 256
TM_IN = 512
TM_MERGE = 512
TM_MLP = 512
FF_CHUNK = 1024
B_ROWS_PER_STEP = 1024
TQ_B = 128
KW_B = 256
NEG_BIG = -1e30
LOG2_E = 1.4426950408889634

PERM_AXIAL = np.concatenate([np.arange(16, 32), np.arange(0, 16), np.arange(48, 64), np.arange(32, 48)])
PERM_SEQ = np.concatenate([np.arange(32, 64), np.arange(0, 32)])

T_SEGS = (("qa", A_Q_W), ("ka", A_KV_W), ("va", A_KV_W),
          ("qb0", B_W), ("kb0", B_W), ("qb1", B_W), ("kb1", B_W), ("qb2", B_W), ("kb2", B_W))
T_ROWS = sum(r for _, r in T_SEGS)
N_COLS = N_B * B_W + 2 * D_MODEL
N_TABS = 16


def _cparams(sem):
    return pltpu.CompilerParams(dimension_semantics=sem, vmem_limit_bytes=VMEM_LIMIT)


def _ada_kernel(c_ref, w_ref, b_ref, o_ref):
    c = c_ref[...]
    c_act = c * (1.0 / (1.0 + jnp.exp(-c)))
    o_ref[...] = jnp.dot(c_act, w_ref[...], preferred_element_type=F32,
                         precision=lax.Precision.HIGHEST) + b_ref[...]


def _ada_all(c, w_ada, b_ada):
    depth = w_ada.shape[0]
    nb = c.shape[0]
    return pl.pallas_call(
        _ada_kernel,
        grid=(depth, 6),
        in_specs=[
            pl.BlockSpec((nb, D_MODEL), lambda l, j: (0, 0)),
            pl.BlockSpec((None, D_MODEL, D_MODEL), lambda l, j: (l, 0, j)),
            pl.BlockSpec((None, 1, D_MODEL), lambda l, j: (l, 0, j)),
        ],
        out_specs=pl.BlockSpec((None, nb, D_MODEL), lambda l, j: (l, 0, j)),
        out_shape=jax.ShapeDtypeStruct((depth, nb, 6 * D_MODEL), F32),
        compiler_params=_cparams(("arbitrary", "arbitrary")),
        name="ada",
    )(c, w_ada, b_ada.reshape(depth, 1, 6 * D_MODEL))


def _modulated_rms(x, g, scale, shift):
    ms = jnp.mean(x * x, axis=-1, keepdims=True)
    return (x * lax.rsqrt(ms + EPS)) * g * (1.0 + scale) + shift


def _norm_rope_t(q, cos_t, sin_t, perm_blocks):
    ss = jnp.sum(q * q, axis=0, keepdims=True)
    r = lax.rsqrt(ss * (1.0 / HEAD_DIM) + EPS)
    qp = jnp.concatenate([q[a:b] for a, b in perm_blocks], axis=0)
    return (q * cos_t + qp * sin_t) * r


AXIAL_BLOCKS = ((16, 32), (0, 16), (48, 64), (32, 48))
SEQ_BLOCKS = ((32, 64), (0, 32))


def _store_class_major(out_ref, stage_ref, tile, dil, sub):
    rows = tile.shape[0] // dil
    out_rows = slice(sub * rows, (sub + 1) * rows)
    if dil == 1:
        out_ref[out_rows, :] = tile.astype(BF16)
        return
    for half in range(B_W // V7X_LANES):
        stage_ref[sub, half] = tile[:, half * V7X_LANES:(half + 1) * V7X_LANES]
    for r in range(dil):
        for half in range(B_W // V7X_LANES):
            col = r * B_W + half * V7X_LANES
            out_ref[out_rows, col:col + V7X_LANES] = (
                stage_ref[sub, half, pl.ds(r, rows, stride=dil), :].astype(BF16))


def _in_kernel(x_ref, ada_ref, g_ref, wn_ref, wt_ref, tab_ref,
               qat_ref, ka_ref, vat_ref,
               qb0_ref, kb0_ref, vb0_ref, qb1_ref, kb1_ref, vb1_ref, qb2_ref, kb2_ref, vb2_ref,
               ga_ref, gb_ref, stage_q_ref, stage_k_ref, stage_v_ref):
    for sub in range(TM_IN // CHUNK):
        _in_subtile(sub, x_ref, ada_ref, g_ref, wn_ref, wt_ref, tab_ref, qat_ref, ka_ref, vat_ref,
                    (qb0_ref, qb1_ref, qb2_ref), (kb0_ref, kb1_ref, kb2_ref), (vb0_ref, vb1_ref, vb2_ref),
                    ga_ref, gb_ref, stage_q_ref, stage_k_ref, stage_v_ref)


def _in_subtile(sub, x_ref, ada_ref, g_ref, wn_ref, wt_ref, tab_ref, qat_ref, ka_ref, vat_ref,
                qb_refs, kb_refs, vb_refs, ga_ref, gb_ref, stage_q_ref, stage_k_ref, stage_v_ref):
    tok = slice(sub * CHUNK, (sub + 1) * CHUNK)
    x = x_ref[tok, :]
    h = _modulated_rms(x, g_ref[...], ada_ref[1:2, :], ada_ref[0:1, :]).astype(BF16)

    nt = (((1,), (1,)), ((), ()))

    def proj_t(row0, rows):
        return lax.dot_general(wt_ref[row0:row0 + rows, :], h, nt, preferred_element_type=F32)

    def heads_t(rt, n_heads, tab0, blocks):
        cos_t = tab_ref[tab0 * HEAD_DIM:(tab0 + 1) * HEAD_DIM, tok]
        sin_t = tab_ref[(tab0 + 1) * HEAD_DIM:(tab0 + 2) * HEAD_DIM, tok]
        return [_norm_rope_t(rt[i * HEAD_DIM:(i + 1) * HEAD_DIM], cos_t, sin_t, blocks)
                for i in range(n_heads)]

    row = 0
    rt = proj_t(row, A_Q_W)
    qa = heads_t(rt, A_Q_HEADS, 0, AXIAL_BLOCKS)
    for i in range(A_Q_HEADS):
        qat_ref[sub, i * HEAD_DIM:(i + 1) * HEAD_DIM, :] = qa[i].astype(BF16)
    row += A_Q_W

    rt = proj_t(row, 2 * A_KV_W)
    ka = heads_t(rt[:A_KV_W], A_KV_HEADS, 2, AXIAL_BLOCKS)
    ka_ref[tok, :] = jnp.concatenate(ka, axis=0).T.astype(BF16)
    vat_ref[sub] = rt[A_KV_W:].astype(BF16)
    row += 2 * A_KV_W

    for g in range(N_B):
        rt = proj_t(row, 2 * B_W)
        qb = heads_t(rt[:B_W], B_HEADS, 4 + 4 * g, SEQ_BLOCKS)
        kb = heads_t(rt[B_W:], B_HEADS, 6 + 4 * g, SEQ_BLOCKS)
        _store_class_major(qb_refs[g], stage_q_ref, jnp.concatenate(qb, axis=0).T, B_GROUPS[g][1], sub)
        _store_class_major(kb_refs[g], stage_k_ref, jnp.concatenate(kb, axis=0).T, B_GROUPS[g][1], sub)
        row += 2 * B_W

    for i, gate_ref in enumerate((ga_ref, gb_ref)):
        lo = N_B * B_W + i * D_MODEL
        z = jnp.dot(h, wn_ref[:, lo:lo + D_MODEL], preferred_element_type=F32)
        gate_ref[tok, :] = (1.0 / (1.0 + jnp.exp(-z))).astype(BF16)
    for g in reversed(range(N_B)):
        v = jnp.dot(h, wn_ref[:, g * B_W:(g + 1) * B_W], preferred_element_type=F32)
        _store_class_major(vb_refs[g], stage_v_ref, v, B_GROUPS[g][1], sub)


def _in_proj(x, ada_l, g_mix_l, wn, wt, tabs, l):
    nb, seq, _ = x.shape
    ns = seq // TM_IN
    nsub = TM_IN // CHUNK
    tok = lambda w: pl.BlockSpec((None, TM_IN, w), lambda s, b: (b, s, 0))
    tchunk = lambda r: pl.BlockSpec((None, nsub, r, CHUNK), lambda s, b: (b, s, 0, 0))
    nat = lambda w: jax.ShapeDtypeStruct((nb, seq, w), BF16)
    out_specs = [tchunk(A_Q_W), tok(A_KV_W), tchunk(A_KV_W)]
    out_shape = [jax.ShapeDtypeStruct((nb, seq // CHUNK, A_Q_W, CHUNK), BF16), nat(A_KV_W),
                 jax.ShapeDtypeStruct((nb, seq // CHUNK, A_KV_W, CHUNK), BF16)]
    for _, dil in B_GROUPS:
        out_specs += [pl.BlockSpec((None, TM_IN // dil, dil * B_W), lambda s, b: (b, s, 0))] * 3
        out_shape += [jax.ShapeDtypeStruct((nb, seq // dil, dil * B_W), BF16)] * 3
    out_specs += [tok(D_MODEL)] * 2
    out_shape += [nat(D_MODEL)] * 2
    return pl.pallas_call(
        _in_kernel,
        grid=(ns, nb),
        in_specs=[
            tok(D_MODEL),
            pl.BlockSpec((None, 6, D_MODEL), lambda s, b: (b, 0, 0)),
            pl.BlockSpec((None, 1, D_MODEL), lambda s, b: (l, 0, 0)),
            pl.BlockSpec((None, D_MODEL, N_COLS), lambda s, b: (l, 0, 0), pipeline_mode=pl.Buffered(1)),
            pl.BlockSpec((None, T_ROWS, D_MODEL), lambda s, b: (l, 0, 0), pipeline_mode=pl.Buffered(1)),
            pl.BlockSpec((None, N_TABS * HEAD_DIM, TM_IN), lambda s, b: (l, 0, s)),
        ],
        out_specs=out_specs,
        out_shape=out_shape,
        scratch_shapes=[pltpu.VMEM((nsub, B_W // V7X_LANES, CHUNK, V7X_LANES), F32)] * 3,
        compiler_params=_cparams(("arbitrary", "arbitrary")),
        name="in_proj",
    )(x, ada_l, g_mix_l, wn, wt, tabs)


A_REP = A_Q_HEADS // A_KV_HEADS
ONES_ROWS = 16


def _attn_a_kernel(qt_ref, k_ref, vt_ref, o_ref, qpad_ref, acc_ref, m_ref, s_ref, smax_ref):
    n_chunks, _, tk = vt_ref.shape
    tq = qt_ref.shape[1]
    qpad_ref[...] = jnp.zeros(qpad_ref.shape, BF16)
    for g in range(A_KV_HEADS):
        for hh in range(A_REP):
            h = g * A_REP + hh
            qpad_ref[g, g * HEAD_DIM:(g + 1) * HEAD_DIM, hh * tq:(hh + 1) * tq] = (
                qt_ref[h * HEAD_DIM:(h + 1) * HEAD_DIM, :])
    m_ref[...] = jnp.full(m_ref.shape, NEG_BIG, F32)
    acc_ref[...] = jnp.zeros(acc_ref.shape, F32)
    ones = jnp.ones((ONES_ROWS, tk), BF16)

    def scores(c, slot, g):
        kc = k_ref[pl.ds(pl.multiple_of(c * tk, tk), tk), :]
        s = jnp.dot(kc, qpad_ref[g], preferred_element_type=F32)
        s_ref[slot, g] = s
        smax_ref[slot, g] = jnp.max(s, axis=0, keepdims=True)

    def consume(c, slot, g):
        m_prev = m_ref[g]
        m_new = jnp.maximum(m_prev, smax_ref[slot, g])
        alpha = jnp.exp2(m_prev - m_new)
        p = jnp.exp2(s_ref[slot, g] - m_new).astype(BF16)
        v_aug = jnp.concatenate([vt_ref[c, g * HEAD_DIM:(g + 1) * HEAD_DIM, :], ones], axis=0)
        acc_ref[g] = acc_ref[g] * alpha + jnp.dot(v_aug, p, preferred_element_type=F32)
        m_ref[g] = m_new

    for g in range(A_KV_HEADS):
        scores(0, 0, g)

    def chunk_pair(c, last):
        for slot in range(2):
            for g in range(A_KV_HEADS):
                if not (last and slot == 1):
                    scores(c + slot + 1, 1 - slot, g)
                consume(c + slot, slot, g)

    def pair_body(i, carry):
        chunk_pair(4 * i, False)
        chunk_pair(4 * i + 2, False)
        return carry

    lax.fori_loop(0, n_chunks // 4 - 1, pair_body, 0)
    chunk_pair(n_chunks - 4, False)
    chunk_pair(n_chunks - 2, True)
    for j in range(A_Q_W // V7X_LANES):
        halves = []
        for h in (2 * j, 2 * j + 1):
            g, hh = divmod(h, A_REP)
            cols = slice(hh * tq, (hh + 1) * tq)
            halves.append(acc_ref[g, :HEAD_DIM, cols] / acc_ref[g, HEAD_DIM:HEAD_DIM + 1, cols])
        blk = jnp.concatenate(halves, axis=0)
        o_ref[:, j * V7X_LANES:(j + 1) * V7X_LANES] = blk.T.astype(BF16)


def _attn_a(qat, ka, vat):
    nb, ns, _, tq = qat.shape
    seq = ka.shape[1]
    wide = A_REP * tq
    return pl.pallas_call(
        _attn_a_kernel,
        grid=(nb, ns),
        in_specs=[
            pl.BlockSpec((None, None, A_Q_W, tq), lambda b, s: (b, s, 0, 0)),
            pl.BlockSpec((None, seq, A_KV_W), lambda b, s: (b, 0, 0)),
            pl.BlockSpec((None, ns, A_KV_W, tq), lambda b, s: (b, 0, 0, 0)),
        ],
        out_specs=pl.BlockSpec((None, tq, A_Q_W), lambda b, s: (b, s, 0)),
        out_shape=jax.ShapeDtypeStruct((nb, seq, A_Q_W), BF16),
        scratch_shapes=[pltpu.VMEM((A_KV_HEADS, A_KV_W, wide), BF16),
                        pltpu.VMEM((A_KV_HEADS, HEAD_DIM + ONES_ROWS, wide), F32),
                        pltpu.VMEM((A_KV_HEADS, 1, wide), F32),
                        pltpu.VMEM((2, A_KV_HEADS, tq, wide), F32),
                        pltpu.VMEM((2, A_KV_HEADS, 1, wide), F32)],
        compiler_params=_cparams(("arbitrary", "arbitrary")),
        name="attn_a",
    )(qat, ka, vat)


def _attn_b_kernel(q_ref, k_ref, v_ref, o_ref, lse_ref, *, half_window):
    tj = q_ref.shape[0]
    length = k_ref.shape[0]
    j_tile = pl.program_id(2) * tj
    lane = lax.broadcasted_iota(jnp.int32, (1, V7X_LANES), 1)
    first_head = lane < HEAD_DIM
    nt = (((1,), (1,)), ((), ()))
    ones = jnp.ones((KW_B, V7X_LANES), BF16)
    for qi in range(tj // TQ_B):
        j0 = j_tile + qi * TQ_B
        ks = jnp.clip(j0 - half_window, 0, length - KW_B)
        ks = pl.multiple_of(ks, half_window)
        jq = j0 + lax.broadcasted_iota(jnp.int32, (TQ_B, 1), 0)
        jk = ks + lax.broadcasted_iota(jnp.int32, (1, KW_B), 1)
        bias = jnp.where(jnp.abs(jq - jk) <= half_window, 0.0, NEG_BIG)
        for pair in range(q_ref.shape[1] // V7X_LANES):
            cols = slice(pair * V7X_LANES, (pair + 1) * V7X_LANES)
            q2 = q_ref[qi * TQ_B:(qi + 1) * TQ_B, cols]
            k2 = k_ref[pl.ds(ks, KW_B), cols]
            v_aug = jnp.concatenate([v_ref[pl.ds(ks, KW_B), cols], ones], axis=1)
            ms, ols = [], []
            for hh in range(2):
                keep = first_head if hh == 0 else jnp.logical_not(first_head)
                qm = jnp.where(keep, q2, jnp.zeros_like(q2))
                s = lax.dot_general(qm, k2, nt, preferred_element_type=F32) + bias
                m = jnp.max(s, axis=1, keepdims=True)
                p = jnp.exp2(s - m).astype(BF16)
                ols.append(jnp.dot(p, v_aug, preferred_element_type=F32))
                ms.append(m)
            o_pair = jnp.where(first_head, ols[0][:, :V7X_LANES], ols[1][:, :V7X_LANES])
            l_pair = jnp.where(first_head, ols[0][:, V7X_LANES:], ols[1][:, V7X_LANES:])
            m_pair = jnp.where(first_head, ms[0], ms[1])
            o_ref[qi * TQ_B:(qi + 1) * TQ_B, cols] = (o_pair / l_pair).astype(BF16)
            lse_ref[qi * TQ_B:(qi + 1) * TQ_B, cols] = m_pair + jnp.log2(l_pair)


def _attn_b(q, k, v, window, dil):
    nb, length, _ = q.shape
    half_window = (window // 2) // dil
    assert half_window * 2 + TQ_B == KW_B and length >= KW_B
    tj = min(B_ROWS_PER_STEP, length)
    ncls = min(B_ROWS_PER_STEP // tj, dil)
    qspec = pl.BlockSpec((None, tj, ncls * B_W), lambda b, r, j: (b, j, r))
    kvspec = pl.BlockSpec((None, length, ncls * B_W), lambda b, r, j: (b, 0, r))
    return pl.pallas_call(
        functools.partial(_attn_b_kernel, half_window=half_window),
        grid=(nb, dil // ncls, length // tj),
        in_specs=[qspec, kvspec, kvspec],
        out_specs=[qspec, qspec],
        out_shape=[jax.ShapeDtypeStruct((nb, length, dil * B_W), BF16),
                   jax.ShapeDtypeStruct((nb, length, dil * B_W), F32)],
        compiler_params=_cparams(("arbitrary", "arbitrary", "arbitrary")),
        name=f"attn_b_d{dil}",
    )(q, k, v)


def _load_class_major(blk_ref, stage_ref, dil, sub):
    rows = CHUNK // dil
    blk_rows = slice(sub * rows, (sub + 1) * rows)
    if dil == 1:
        return blk_ref[blk_rows, :].astype(F32)
    halves = range(B_W // V7X_LANES)
    for r in range(dil):
        for half in halves:
            col = r * B_W + half * V7X_LANES
            stage_ref[sub, half, pl.ds(r, rows, stride=dil), :] = (
                blk_ref[blk_rows, col:col + V7X_LANES].astype(F32))
    return jnp.concatenate([stage_ref[sub, half] for half in halves], axis=1)


def _merge_kernel(x_ref, ada_ref, oa_ref, ob0_ref, ob1_ref, ob2_ref, ls0_ref, ls1_ref, ls2_ref,
                  ga_ref, gb_ref, wa_ref, wb_ref, wo_ref, xo_ref, *stage_refs):
    dils = [d for _, d in B_GROUPS]
    for sub in range(TM_MERGE // CHUNK):
        tok = slice(sub * CHUNK, (sub + 1) * CHUNK)
        ya = jnp.dot(oa_ref[tok, :], wa_ref[...], preferred_element_type=F32)
        lses = [_load_class_major(r, stage_refs[2 * g], dils[g], sub)
                for g, r in enumerate((ls0_ref, ls1_ref, ls2_ref))]
        outs = [_load_class_major(r, stage_refs[2 * g + 1], dils[g], sub)
                for g, r in enumerate((ob0_ref, ob1_ref, ob2_ref))]
        top = jnp.maximum(jnp.maximum(lses[0], lses[1]), lses[2])
        es = [jnp.exp2(v - top) for v in lses]
        den = es[0] + es[1] + es[2]
        ob = (es[0] * outs[0] + es[1] * outs[1] + es[2] * outs[2]) / den
        yb = jnp.dot(ob.astype(BF16), wb_ref[...], preferred_element_type=F32)
        mixed = ga_ref[tok, :].astype(F32) * ya + gb_ref[tok, :].astype(F32) * yb
        upd = jnp.dot(mixed.astype(BF16), wo_ref[...], preferred_element_type=F32)
        xo_ref[tok, :] = x_ref[tok, :] + ada_ref[2:3, :] * upd


def _merge(x, ada_l, oa, obs, lss, ga, gb, wa, wb, wo, l):
    nb, seq, _ = x.shape
    tok = lambda w: pl.BlockSpec((None, TM_MERGE, w), lambda b, s: (b, s, 0))
    wspec = lambda r, c: pl.BlockSpec((None, r, c), lambda b, s: (l, 0, 0), pipeline_mode=pl.Buffered(1))
    cls = [pl.BlockSpec((None, TM_MERGE // d, d * B_W), lambda b, s: (b, s, 0)) for _, d in B_GROUPS]
    return pl.pallas_call(
        _merge_kernel,
        grid=(nb, seq // TM_MERGE),
        in_specs=[tok(D_MODEL), pl.BlockSpec((None, 6, D_MODEL), lambda b, s: (b, 0, 0)), tok(A_Q_W)]
                 + cls + cls + [tok(D_MODEL)] * 2
                 + [wspec(A_Q_W, D_MODEL), wspec(B_W, D_MODEL), wspec(D_MODEL, D_MODEL)],
        out_specs=tok(D_MODEL),
        out_shape=jax.ShapeDtypeStruct(x.shape, F32),
        scratch_shapes=[pltpu.VMEM((TM_MERGE // CHUNK, B_W // V7X_LANES, CHUNK, V7X_LANES), F32)] * (2 * N_B),
        input_output_aliases={0: 0} if l > 0 else {},
        compiler_params=_cparams(("arbitrary", "arbitrary")),
        name="merge",
    )(x, ada_l, oa, *obs, *lss, ga, gb, wa, wb, wo)


def _mlp_kernel(x_ref, ada_ref, g_ref, w1_ref, w2_ref, xo_ref):
    x = x_ref[...]
    h = _modulated_rms(x, g_ref[...], ada_ref[4:5, :], ada_ref[3:4, :]).astype(BF16)
    acc = jnp.zeros(x.shape, F32)
    for c in range(D_FF // FF_CHUNK):
        a = jnp.dot(h, w1_ref[:, c * FF_CHUNK:(c + 1) * FF_CHUNK], preferred_element_type=F32)
        a = jnp.square(jnp.maximum(a, 0.0)).astype(BF16)
        acc = acc + jnp.dot(a, w2_ref[c * FF_CHUNK:(c + 1) * FF_CHUNK, :], preferred_element_type=F32)
    xo_ref[...] = x + ada_ref[5:6, :] * acc


def _mlp(x, ada_l, g_mlp_l, w1, w2, l):
    nb, seq, _ = x.shape
    tok = pl.BlockSpec((None, TM_MLP, D_MODEL), lambda b, s: (b, s, 0))
    return pl.pallas_call(
        _mlp_kernel,
        grid=(nb, seq // TM_MLP),
        in_specs=[
            tok,
            pl.BlockSpec((None, 6, D_MODEL), lambda b, s: (b, 0, 0)),
            pl.BlockSpec((None, 1, D_MODEL), lambda b, s: (l, 0, 0)),
            pl.BlockSpec((None, D_MODEL, D_FF), lambda b, s: (l, 0, 0), pipeline_mode=pl.Buffered(1)),
            pl.BlockSpec((None, D_FF, D_MODEL), lambda b, s: (l, 0, 0), pipeline_mode=pl.Buffered(1)),
        ],
        out_specs=tok,
        out_shape=jax.ShapeDtypeStruct(x.shape, F32),
        input_output_aliases={0: 0},
        compiler_params=_cparams(("arbitrary", "arbitrary")),
        name="mlp",
    )(x, ada_l, g_mlp_l, w1, w2)


def _rope_tables(seq):
    pos = jnp.arange(seq)
    rows = seq // GRID_W
    row_idx = jnp.broadcast_to(jnp.arange(rows)[:, None], (rows, GRID_W)).reshape(-1)
    col_idx = jnp.broadcast_to(jnp.arange(GRID_W)[None, :], (rows, GRID_W)).reshape(-1)

    def cos_sin(p, dim):
        inv = ROPE_THETA ** (-jnp.arange(0, dim, 2, dtype=F32) / dim)
        ang = p.astype(F32)[:, None] * inv[None, :]
        return jnp.cos(ang), jnp.sin(ang)

    rc, rs = cos_sin(row_idx, HEAD_DIM // 2)
    cc, cs = cos_sin(col_idx, HEAD_DIM // 2)
    sc, ss = cos_sin(pos, HEAD_DIM)
    cos_ax = jnp.concatenate([rc, rc, cc, cc], axis=-1)
    sin_ax = jnp.concatenate([-rs, rs, -cs, cs], axis=-1)
    cos_sq = jnp.concatenate([sc, sc], axis=-1)
    sin_sq = jnp.concatenate([-ss, ss], axis=-1)
    return (cos_ax, sin_ax), (cos_sq, sin_sq)


def _gain_tables(q_norm_a, k_norm_a, q_norm_b, k_norm_b, seq):
    (cos_ax, sin_ax), (cos_sq, sin_sq) = _rope_tables(seq)
    scale = HEAD_DIM ** -0.5 * LOG2_E

    def pair(gain, cos, sin, perm, mult):
        c = gain[:, None, :] * cos[None] * mult
        s = gain[:, perm][:, None, :] * sin[None] * mult
        return [jnp.swapaxes(c, 1, 2), jnp.swapaxes(s, 1, 2)]

    tabs = pair(q_norm_a, cos_ax, sin_ax, PERM_AXIAL, scale) + pair(k_norm_a, cos_ax, sin_ax, PERM_AXIAL, 1.0)
    for g in range(N_B):
        tabs += pair(q_norm_b[:, g], cos_sq, sin_sq, PERM_SEQ, scale)
        tabs += pair(k_norm_b[:, g], cos_sq, sin_sq, PERM_SEQ, 1.0)
    return jnp.concatenate(tabs, axis=1)


def _split_w_in(w_in):
    sizes = [A_Q_W, A_KV_W, A_KV_W] + [B_W] * (3 * N_B) + [D_MODEL, D_MODEL]
    offs = np.concatenate([[0], np.cumsum(sizes)])
    col = lambda i: w_in[:, :, offs[i]:offs[i + 1]]
    names = ["qa", "ka", "va"] + [f"{t}b{g}" for g in range(N_B) for t in "qkv"] + ["ga", "gb"]
    parts = {n: col(i) for i, n in enumerate(names)}
    wn = jnp.concatenate([parts["vb0"], parts["vb1"], parts["vb2"], parts["ga"], parts["gb"]], axis=-1)
    wt = jnp.concatenate([parts[n] for n, _ in T_SEGS], axis=-1)
    return wn.astype(BF16), jnp.swapaxes(wt, 1, 2).astype(BF16)


def kernel(x, c, w_ada, b_ada, g_mix, g_mlp, w_in, q_norm_a, k_norm_a, q_norm_b, k_norm_b,
           w_branch_a, w_branch_b, w_out, w_ff1, w_ff2):
    nb, seq, _ = x.shape
    depth = w_in.shape[0]
    assert seq % GRID_W == 0 and seq % TM_MERGE == 0 and seq % TM_IN == 0

    ada = _ada_all(c, w_ada, b_ada).reshape(depth, nb, 6, D_MODEL)
    tabs = _gain_tables(q_norm_a, k_norm_a, q_norm_b, k_norm_b, seq)
    wn, wt = _split_w_in(w_in)
    wa, wb, wo = w_branch_a.astype(BF16), w_branch_b.astype(BF16), w_out.astype(BF16)
    w1, w2 = w_ff1.astype(BF16), w_ff2.astype(BF16)
    g_mix = g_mix.reshape(depth, 1, D_MODEL)
    g_mlp = g_mlp.reshape(depth, 1, D_MODEL)

    for l in range(depth):
        (qat, ka, vat, qb0, kb0, vb0, qb1, kb1, vb1, qb2, kb2, vb2, ga, gb) = _in_proj(
            x, ada[l], g_mix, wn, wt, tabs, l)
        oa = _attn_a(qat, ka, vat)
        obs, lss = [], []
        for (window, dil), q, k, v in zip(B_GROUPS, (qb0, qb1, qb2), (kb0, kb1, kb2), (vb0, vb1, vb2)):
            o, lse = _attn_b(q, k, v, window, dil)
            obs.append(o)
            lss.append(lse)
        x = _merge(x, ada[l], oa, obs, lss, ga, gb, wa, wb, wo, l)
        x = _mlp(x, ada[l], g_mlp, w1, w2, l)
    return x
```

```python
import functools

import jax
import jax.numpy as jnp
import numpy as np
from jax import lax
from jax.experimental import pallas as pl
from jax.experimental.pallas import tpu as pltpu

D_MODEL = 1024
HEAD_DIM = 64
A_Q_HEADS = 8
A_KV_HEADS = 2
B_GROUPS = ((128, 1), (512, 4), (2048, 16))
B_HEADS = 4
N_B = len(B_GROUPS)
D_FF = 4 * D_MODEL
GRID_W = 64
ROPE_THETA = 10000.0
EPS = 1e-6

A_Q_W = A_Q_HEADS * HEAD_DIM
A_KV_W = A_KV_HEADS * HEAD_DIM
B_W = B_HEADS * HEAD_DIM

V7X_LANES = 128
V7X_VMEM_BYTES = 64 * 1024 * 1024
VMEM_LIMIT = 52 * 1024 * 1024

BF16 = jnp.bfloat16
F32 = jnp.float32

CHUNK = 256
TM_IN = 512
TQ_A = 512
A_PAIRS_PER_BODY = 2
TM_MIX = 512
FF_CHUNK = 1024
B_ROWS_PER_STEP = 1024
TQ_B = 128
KW_B = 256
NEG_BIG = -1e30
LOG2_E = 1.4426950408889634

PERM_AXIAL = np.concatenate([np.arange(16, 32), np.arange(0, 16), np.arange(48, 64), np.arange(32, 48)])
PERM_SEQ = np.concatenate([np.arange(32, 64), np.arange(0, 32)])

T_SEGS = (("qa", A_Q_W), ("ka", A_KV_W), ("va", A_KV_W),
          ("qb0", B_W), ("kb0", B_W), ("qb1", B_W), ("kb1", B_W), ("qb2", B_W), ("kb2", B_W))
T_ROWS = sum(r for _, r in T_SEGS)
N_COLS = N_B * B_W + 2 * D_MODEL
N_TABS = 16


def _cparams(sem):
    return pltpu.CompilerParams(dimension_semantics=sem, vmem_limit_bytes=VMEM_LIMIT)


def _ada_kernel(c_ref, w_ref, b_ref, o_ref):
    c = c_ref[...]
    c_act = c * (1.0 / (1.0 + jnp.exp(-c)))
    o_ref[...] = jnp.dot(c_act, w_ref[...], preferred_element_type=F32,
                         precision=lax.Precision.HIGHEST) + b_ref[...]


def _ada_all(c, w_ada, b_ada):
    depth = w_ada.shape[0]
    nb = c.shape[0]
    return pl.pallas_call(
        _ada_kernel,
        grid=(depth, 6),
        in_specs=[
            pl.BlockSpec((nb, D_MODEL), lambda l, j: (0, 0)),
            pl.BlockSpec((None, D_MODEL, D_MODEL), lambda l, j: (l, 0, j)),
            pl.BlockSpec((None, 1, D_MODEL), lambda l, j: (l, 0, j)),
        ],
        out_specs=pl.BlockSpec((None, nb, D_MODEL), lambda l, j: (l, 0, j)),
        out_shape=jax.ShapeDtypeStruct((depth, nb, 6 * D_MODEL), F32),
        compiler_params=_cparams(("arbitrary", "arbitrary")),
        name="ada",
    )(c, w_ada, b_ada.reshape(depth, 1, 6 * D_MODEL))


def _modulated_rms(x, g, scale, shift):
    ms = jnp.mean(x * x, axis=-1, keepdims=True)
    return (x * lax.rsqrt(ms + EPS)) * g * (1.0 + scale) + shift


def _norm_rope_t(q, cos_t, sin_t, perm_blocks):
    ss = jnp.sum(q * q, axis=0, keepdims=True)
    r = lax.rsqrt(ss * (1.0 / HEAD_DIM) + EPS)
    qp = jnp.concatenate([q[a:b] for a, b in perm_blocks], axis=0)
    return (q * cos_t + qp * sin_t) * r


AXIAL_BLOCKS = ((16, 32), (0, 16), (48, 64), (32, 48))
SEQ_BLOCKS = ((32, 64), (0, 32))


def _store_class_major(out_ref, stage_ref, tile, dil, sub):
    rows = tile.shape[0] // dil
    out_rows = slice(sub * rows, (sub + 1) * rows)
    if dil == 1:
        out_ref[out_rows, :] = tile.astype(BF16)
        return
    for half in range(B_W // V7X_LANES):
        stage_ref[sub, half] = tile[:, half * V7X_LANES:(half + 1) * V7X_LANES]
    for r in range(dil):
        for half in range(B_W // V7X_LANES):
            col = r * B_W + half * V7X_LANES
            out_ref[out_rows, col:col + V7X_LANES] = (
                stage_ref[sub, half, pl.ds(r, rows, stride=dil), :].astype(BF16))


def _in_kernel(x_ref, ada_ref, g_ref, wn_ref, wt_ref, tab_ref,
               qat_ref, ka_ref, vat_ref,
               qb0_ref, kb0_ref, vb0_ref, qb1_ref, kb1_ref, vb1_ref, qb2_ref, kb2_ref, vb2_ref,
               ga_ref, gb_ref, stage_q_ref, stage_k_ref, stage_v_ref):
    for sub in range(TM_IN // CHUNK):
        _in_subtile(sub, x_ref, ada_ref, g_ref, wn_ref, wt_ref, tab_ref, qat_ref, ka_ref, vat_ref,
                    (qb0_ref, qb1_ref, qb2_ref), (kb0_ref, kb1_ref, kb2_ref), (vb0_ref, vb1_ref, vb2_ref),
                    ga_ref, gb_ref, stage_q_ref, stage_k_ref, stage_v_ref)


def _in_subtile(sub, x_ref, ada_ref, g_ref, wn_ref, wt_ref, tab_ref, qat_ref, ka_ref, vat_ref,
                qb_refs, kb_refs, vb_refs, ga_ref, gb_ref, stage_q_ref, stage_k_ref, stage_v_ref):
    tok = slice(sub * CHUNK, (sub + 1) * CHUNK)
    x = x_ref[tok, :]
    h = _modulated_rms(x, g_ref[...], ada_ref[1:2, :], ada_ref[0:1, :]).astype(BF16)

    nt = (((1,), (1,)), ((), ()))

    def proj_t(row0, rows):
        return lax.dot_general(wt_ref[row0:row0 + rows, :], h, nt, preferred_element_type=F32)

    def heads_t(rt, n_heads, tab0, blocks):
        cos_t = tab_ref[tab0 * HEAD_DIM:(tab0 + 1) * HEAD_DIM, tok]
        sin_t = tab_ref[(tab0 + 1) * HEAD_DIM:(tab0 + 2) * HEAD_DIM, tok]
        return [_norm_rope_t(rt[i * HEAD_DIM:(i + 1) * HEAD_DIM], cos_t, sin_t, blocks)
                for i in range(n_heads)]

    row = 0
    rt = proj_t(row, A_Q_W)
    qa = heads_t(rt, A_Q_HEADS, 0, AXIAL_BLOCKS)
    for i in range(A_Q_HEADS):
        qat_ref[sub, i * HEAD_DIM:(i + 1) * HEAD_DIM, :] = qa[i].astype(BF16)
    row += A_Q_W

    rt = proj_t(row, 2 * A_KV_W)
    ka = heads_t(rt[:A_KV_W], A_KV_HEADS, 2, AXIAL_BLOCKS)
    ka_ref[tok, :] = jnp.concatenate(ka, axis=0).T.astype(BF16)
    vat_ref[sub] = rt[A_KV_W:].astype(BF16)
    row += 2 * A_KV_W

    for g in range(N_B):
        rt = proj_t(row, 2 * B_W)
        qb = heads_t(rt[:B_W], B_HEADS, 4 + 4 * g, SEQ_BLOCKS)
        kb = heads_t(rt[B_W:], B_HEADS, 6 + 4 * g, SEQ_BLOCKS)
        _store_class_major(qb_refs[g], stage_q_ref, jnp.concatenate(qb, axis=0).T, B_GROUPS[g][1], sub)
        _store_class_major(kb_refs[g], stage_k_ref, jnp.concatenate(kb, axis=0).T, B_GROUPS[g][1], sub)
        row += 2 * B_W

    for i, gate_ref in enumerate((ga_ref, gb_ref)):
        lo = N_B * B_W + i * D_MODEL
        z = jnp.dot(h, wn_ref[:, lo:lo + D_MODEL], preferred_element_type=F32)
        gate_ref[tok, :] = (1.0 / (1.0 + jnp.exp(-z))).astype(BF16)
    for g in reversed(range(N_B)):
        v = jnp.dot(h, wn_ref[:, g * B_W:(g + 1) * B_W], preferred_element_type=F32)
        _store_class_major(vb_refs[g], stage_v_ref, v, B_GROUPS[g][1], sub)


def _in_proj(x, ada_l, g_mix_l, wn, wt, tabs, l):
    nb, seq, _ = x.shape
    ns = seq // TM_IN
    nsub = TM_IN // CHUNK
    tok = lambda w: pl.BlockSpec((None, TM_IN, w), lambda s, b: (b, s, 0))
    tchunk = lambda r: pl.BlockSpec((None, nsub, r, CHUNK), lambda s, b: (b, s, 0, 0))
    nat = lambda w: jax.ShapeDtypeStruct((nb, seq, w), BF16)
    out_specs = [tchunk(A_Q_W), tok(A_KV_W), tchunk(A_KV_W)]
    out_shape = [jax.ShapeDtypeStruct((nb, seq // CHUNK, A_Q_W, CHUNK), BF16), nat(A_KV_W),
                 jax.ShapeDtypeStruct((nb, seq // CHUNK, A_KV_W, CHUNK), BF16)]
    for _, dil in B_GROUPS:
        out_specs += [pl.BlockSpec((None, TM_IN // dil, dil * B_W), lambda s, b: (b, s, 0))] * 3
        out_shape += [jax.ShapeDtypeStruct((nb, seq // dil, dil * B_W), BF16)] * 3
    out_specs += [tok(D_MODEL)] * 2
    out_shape += [nat(D_MODEL)] * 2
    return pl.pallas_call(
        _in_kernel,
        grid=(ns, nb),
        in_specs=[
            tok(D_MODEL),
            pl.BlockSpec((None, 6, D_MODEL), lambda s, b: (b, 0, 0)),
            pl.BlockSpec((None, 1, D_MODEL), lambda s, b: (l, 0, 0)),
            pl.BlockSpec((None, D_MODEL, N_COLS), lambda s, b: (l, 0, 0), pipeline_mode=pl.Buffered(1)),
            pl.BlockSpec((None, T_ROWS, D_MODEL), lambda s, b: (l, 0, 0), pipeline_mode=pl.Buffered(1)),
            pl.BlockSpec((None, N_TABS * HEAD_DIM, TM_IN), lambda s, b: (l, 0, s)),
        ],
        out_specs=out_specs,
        out_shape=out_shape,
        scratch_shapes=[pltpu.VMEM((nsub, B_W // V7X_LANES, CHUNK, V7X_LANES), F32)] * 3,
        compiler_params=_cparams(("arbitrary", "arbitrary")),
        name="in_proj",
    )(x, ada_l, g_mix_l, wn, wt, tabs)


A_REP = A_Q_HEADS // A_KV_HEADS
ONES_ROWS = 16


def _attn_a_kernel(qt_ref, k_ref, vt_ref, o_ref, qpad_ref, acc_ref, m_ref, s_ref, smax_ref):
    for sub in range(qt_ref.shape[0]):
        _attn_a_tile(qt_ref.at[sub], k_ref, vt_ref, o_ref.at[sub * CHUNK:(sub + 1) * CHUNK],
                     qpad_ref.at[sub], acc_ref.at[sub], m_ref.at[sub], s_ref, smax_ref)


def _attn_a_tile(qt_ref, k_ref, vt_ref, o_ref, qpad_ref, acc_ref, m_ref, s_ref, smax_ref):
    n_chunks, _, tk = vt_ref.shape
    tq = qt_ref.shape[1]
    qpad_ref[...] = jnp.zeros(qpad_ref.shape, BF16)
    for g in range(A_KV_HEADS):
        for hh in range(A_REP):
            h = g * A_REP + hh
            qpad_ref[g, g * HEAD_DIM:(g + 1) * HEAD_DIM, hh * tq:(hh + 1) * tq] = (
                qt_ref[h * HEAD_DIM:(h + 1) * HEAD_DIM, :])
    m_ref[...] = jnp.full(m_ref.shape, NEG_BIG, F32)
    acc_ref[...] = jnp.zeros(acc_ref.shape, F32)
    ones = jnp.ones((ONES_ROWS, tk), BF16)

    def scores(c, slot, g):
        kc = k_ref[pl.ds(pl.multiple_of(c * tk, tk), tk), :]
        s = jnp.dot(kc, qpad_ref[g], preferred_element_type=F32)
        s_ref[slot, g] = s
        smax_ref[slot, g] = jnp.max(s, axis=0, keepdims=True)

    def consume(c, slot, g):
        m_prev = m_ref[g]
        m_new = jnp.maximum(m_prev, smax_ref[slot, g])
        alpha = jnp.exp2(m_prev - m_new)
        p = jnp.exp2(s_ref[slot, g] - m_new).astype(BF16)
        v_aug = jnp.concatenate([vt_ref[c, g * HEAD_DIM:(g + 1) * HEAD_DIM, :], ones], axis=0)
        acc_ref[g] = acc_ref[g] * alpha + jnp.dot(v_aug, p, preferred_element_type=F32)
        m_ref[g] = m_new

    for g in range(A_KV_HEADS):
        scores(0, 0, g)

    def chunk_pair(c, last):
        for slot in range(2):
            for g in range(A_KV_HEADS):
                if not (last and slot == 1):
                    scores(c + slot + 1, 1 - slot, g)
                consume(c + slot, slot, g)

    body_chunks = 2 * A_PAIRS_PER_BODY

    def loop_body(i, carry):
        for j in range(A_PAIRS_PER_BODY):
            chunk_pair(body_chunks * i + 2 * j, False)
        return carry

    lax.fori_loop(0, n_chunks // body_chunks - 1, loop_body, 0)
    for j in range(A_PAIRS_PER_BODY):
        chunk_pair(n_chunks - body_chunks + 2 * j, j == A_PAIRS_PER_BODY - 1)
    for j in range(A_Q_W // V7X_LANES):
        halves = []
        for h in (2 * j, 2 * j + 1):
            g, hh = divmod(h, A_REP)
            cols = slice(hh * tq, (hh + 1) * tq)
            halves.append(acc_ref[g, :HEAD_DIM, cols] / acc_ref[g, HEAD_DIM:HEAD_DIM + 1, cols])
        blk = jnp.concatenate(halves, axis=0)
        o_ref[:, j * V7X_LANES:(j + 1) * V7X_LANES] = blk.T.astype(BF16)


def _attn_a(qat, ka, vat):
    nb, ns, _, chunk = qat.shape
    seq = ka.shape[1]
    nq = TQ_A // chunk
    wide = A_REP * chunk
    return pl.pallas_call(
        _attn_a_kernel,
        grid=(nb, seq // TQ_A),
        in_specs=[
            pl.BlockSpec((None, nq, A_Q_W, chunk), lambda b, s: (b, s, 0, 0)),
            pl.BlockSpec((None, seq, A_KV_W), lambda b, s: (b, 0, 0)),
            pl.BlockSpec((None, ns, A_KV_W, chunk), lambda b, s: (b, 0, 0, 0)),
        ],
        out_specs=pl.BlockSpec((None, TQ_A, A_Q_W), lambda b, s: (b, s, 0)),
        out_shape=jax.ShapeDtypeStruct((nb, seq, A_Q_W), BF16),
        scratch_shapes=[pltpu.VMEM((nq, A_KV_HEADS, A_KV_W, wide), BF16),
                        pltpu.VMEM((nq, A_KV_HEADS, HEAD_DIM + ONES_ROWS, wide), F32),
                        pltpu.VMEM((nq, A_KV_HEADS, 1, wide), F32),
                        pltpu.VMEM((2, A_KV_HEADS, chunk, wide), F32),
                        pltpu.VMEM((2, A_KV_HEADS, 1, wide), F32)],
        compiler_params=_cparams(("arbitrary", "arbitrary")),
        name="attn_a",
    )(qat, ka, vat)


def _attn_b_kernel(q_ref, k_ref, v_ref, o_ref, lse_ref, *, half_window):
    tj = q_ref.shape[0]
    length = k_ref.shape[0]
    j_tile = pl.program_id(2) * tj
    lane = lax.broadcasted_iota(jnp.int32, (1, V7X_LANES), 1)
    first_head = lane < HEAD_DIM
    nt = (((1,), (1,)), ((), ()))
    ones = jnp.ones((KW_B, V7X_LANES), BF16)
    for qi in range(tj // TQ_B):
        j0 = j_tile + qi * TQ_B
        ks = jnp.clip(j0 - half_window, 0, length - KW_B)
        ks = pl.multiple_of(ks, half_window)
        jq = j0 + lax.broadcasted_iota(jnp.int32, (TQ_B, 1), 0)
        jk = ks + lax.broadcasted_iota(jnp.int32, (1, KW_B), 1)
        bias = jnp.where(jnp.abs(jq - jk) <= half_window, 0.0, NEG_BIG)
        for pair in range(q_ref.shape[1] // V7X_LANES):
            cols = slice(pair * V7X_LANES, (pair + 1) * V7X_LANES)
            q2 = q_ref[qi * TQ_B:(qi + 1) * TQ_B, cols]
            k2 = k_ref[pl.ds(ks, KW_B), cols]
            v_aug = jnp.concatenate([v_ref[pl.ds(ks, KW_B), cols], ones], axis=1)
            ms, ols = [], []
            for hh in range(2):
                keep = first_head if hh == 0 else jnp.logical_not(first_head)
                qm = jnp.where(keep, q2, jnp.zeros_like(q2))
                s = lax.dot_general(qm, k2, nt, preferred_element_type=F32) + bias
                m = jnp.max(s, axis=1, keepdims=True)
                p = jnp.exp2(s - m).astype(BF16)
                ols.append(jnp.dot(p, v_aug, preferred_element_type=F32))
                ms.append(m)
            o_pair = jnp.where(first_head, ols[0][:, :V7X_LANES], ols[1][:, :V7X_LANES])
            l_pair = jnp.where(first_head, ols[0][:, V7X_LANES:], ols[1][:, V7X_LANES:])
            m_pair = jnp.where(first_head, ms[0], ms[1])
            o_ref[qi * TQ_B:(qi + 1) * TQ_B, cols] = (o_pair / l_pair).astype(BF16)
            lse_ref[qi * TQ_B:(qi + 1) * TQ_B, cols] = m_pair + jnp.log2(l_pair)


def _attn_b(q, k, v, window, dil):
    nb, length, _ = q.shape
    half_window = (window // 2) // dil
    assert half_window * 2 + TQ_B == KW_B and length >= KW_B
    tj = min(B_ROWS_PER_STEP, length)
    ncls = min(B_ROWS_PER_STEP // tj, dil)
    qspec = pl.BlockSpec((None, tj, ncls * B_W), lambda b, r, j: (b, j, r))
    kvspec = pl.BlockSpec((None, length, ncls * B_W), lambda b, r, j: (b, 0, r))
    return pl.pallas_call(
        functools.partial(_attn_b_kernel, half_window=half_window),
        grid=(nb, dil // ncls, length // tj),
        in_specs=[qspec, kvspec, kvspec],
        out_specs=[qspec, qspec],
        out_shape=[jax.ShapeDtypeStruct((nb, length, dil * B_W), BF16),
                   jax.ShapeDtypeStruct((nb, length, dil * B_W), F32)],
        compiler_params=_cparams(("arbitrary", "arbitrary", "arbitrary")),
        name=f"attn_b_d{dil}",
    )(q, k, v)


def _load_class_major(blk_ref, stage_ref, dil, sub):
    rows = CHUNK // dil
    blk_rows = slice(sub * rows, (sub + 1) * rows)
    if dil == 1:
        return blk_ref[blk_rows, :].astype(F32)
    halves = range(B_W // V7X_LANES)
    for r in range(dil):
        for half in halves:
            col = r * B_W + half * V7X_LANES
            stage_ref[sub, half, pl.ds(r, rows, stride=dil), :] = (
                blk_ref[blk_rows, col:col + V7X_LANES].astype(F32))
    return jnp.concatenate([stage_ref[sub, half] for half in halves], axis=1)


def _mix_mlp_kernel(x_ref, ada_ref, oa_ref, ob0_ref, ob1_ref, ob2_ref, ls0_ref, ls1_ref, ls2_ref,
                    ga_ref, gb_ref, wa_ref, wb_ref, wo_ref, g_ref, w1_ref, w2_ref, xo_ref, *stage_refs):
    dils = [d for _, d in B_GROUPS]
    for sub in range(TM_MIX // CHUNK):
        tok = slice(sub * CHUNK, (sub + 1) * CHUNK)
        ya = jnp.dot(oa_ref[tok, :], wa_ref[...], preferred_element_type=F32)
        lses = [_load_class_major(r, stage_refs[2 * g], dils[g], sub)
                for g, r in enumerate((ls0_ref, ls1_ref, ls2_ref))]
        outs = [_load_class_major(r, stage_refs[2 * g + 1], dils[g], sub)
                for g, r in enumerate((ob0_ref, ob1_ref, ob2_ref))]
        top = jnp.maximum(jnp.maximum(lses[0], lses[1]), lses[2])
        es = [jnp.exp2(v - top) for v in lses]
        den = es[0] + es[1] + es[2]
        ob = (es[0] * outs[0] + es[1] * outs[1] + es[2] * outs[2]) / den
        yb = jnp.dot(ob.astype(BF16), wb_ref[...], preferred_element_type=F32)
        mixed = ga_ref[tok, :].astype(F32) * ya + gb_ref[tok, :].astype(F32) * yb
        upd = jnp.dot(mixed.astype(BF16), wo_ref[...], preferred_element_type=F32)
        xo_ref[tok, :] = x_ref[tok, :] + ada_ref[2:3, :] * upd

    x = xo_ref[...]
    h = _modulated_rms(x, g_ref[...], ada_ref[4:5, :], ada_ref[3:4, :]).astype(BF16)
    acc = jnp.zeros(x.shape, F32)
    for c in range(D_FF // FF_CHUNK):
        a = jnp.dot(h, w1_ref[:, c * FF_CHUNK:(c + 1) * FF_CHUNK], preferred_element_type=F32)
        a = jnp.square(jnp.maximum(a, 0.0)).astype(BF16)
        acc = acc + jnp.dot(a, w2_ref[c * FF_CHUNK:(c + 1) * FF_CHUNK, :], preferred_element_type=F32)
    xo_ref[...] = x + ada_ref[5:6, :] * acc


def _mix_mlp(x, ada_l, oa, obs, lss, ga, gb, wa, wb, wo, g_mlp_l, w1, w2, l):
    nb, seq, _ = x.shape
    tok = lambda w: pl.BlockSpec((None, TM_MIX, w), lambda b, s: (b, s, 0))
    wspec = lambda r, c: pl.BlockSpec((None, r, c), lambda b, s: (l, 0, 0), pipeline_mode=pl.Buffered(1))
    cls = [pl.BlockSpec((None, TM_MIX // d, d * B_W), lambda b, s: (b, s, 0)) for _, d in B_GROUPS]
    return pl.pallas_call(
        _mix_mlp_kernel,
        grid=(nb, seq // TM_MIX),
        in_specs=[tok(D_MODEL), pl.BlockSpec((None, 6, D_MODEL), lambda b, s: (b, 0, 0)), tok(A_Q_W)]
                 + cls + cls + [tok(D_MODEL)] * 2
                 + [wspec(A_Q_W, D_MODEL), wspec(B_W, D_MODEL), wspec(D_MODEL, D_MODEL),
                    pl.BlockSpec((None, 1, D_MODEL), lambda b, s: (l, 0, 0)),
                    wspec(D_MODEL, D_FF), wspec(D_FF, D_MODEL)],
        out_specs=tok(D_MODEL),
        out_shape=jax.ShapeDtypeStruct(x.shape, F32),
        scratch_shapes=[pltpu.VMEM((TM_MIX // CHUNK, B_W // V7X_LANES, CHUNK, V7X_LANES), F32)] * (2 * N_B),
        input_output_aliases={0: 0} if l > 0 else {},
        compiler_params=_cparams(("arbitrary", "arbitrary")),
        name="mix_mlp",
    )(x, ada_l, oa, *obs, *lss, ga, gb, wa, wb, wo, g_mlp_l, w1, w2)


def _rope_tables(seq):
    pos = jnp.arange(seq)
    rows = seq // GRID_W
    row_idx = jnp.broadcast_to(jnp.arange(rows)[:, None], (rows, GRID_W)).reshape(-1)
    col_idx = jnp.broadcast_to(jnp.arange(GRID_W)[None, :], (rows, GRID_W)).reshape(-1)

    def cos_sin(p, dim):
        inv = ROPE_THETA ** (-jnp.arange(0, dim, 2, dtype=F32) / dim)
        ang = p.astype(F32)[:, None] * inv[None, :]
        return jnp.cos(ang), jnp.sin(ang)

    rc, rs = cos_sin(row_idx, HEAD_DIM // 2)
    cc, cs = cos_sin(col_idx, HEAD_DIM // 2)
    sc, ss = cos_sin(pos, HEAD_DIM)
    cos_ax = jnp.concatenate([rc, rc, cc, cc], axis=-1)
    sin_ax = jnp.concatenate([-rs, rs, -cs, cs], axis=-1)
    cos_sq = jnp.concatenate([sc, sc], axis=-1)
    sin_sq = jnp.concatenate([-ss, ss], axis=-1)
    return (cos_ax, sin_ax), (cos_sq, sin_sq)


def _gain_tables(q_norm_a, k_norm_a, q_norm_b, k_norm_b, seq):
    (cos_ax, sin_ax), (cos_sq, sin_sq) = _rope_tables(seq)
    scale = HEAD_DIM ** -0.5 * LOG2_E

    def pair(gain, cos, sin, perm, mult):
        c = gain[:, None, :] * cos[None] * mult
        s = gain[:, perm][:, None, :] * sin[None] * mult
        return [jnp.swapaxes(c, 1, 2), jnp.swapaxes(s, 1, 2)]

    tabs = pair(q_norm_a, cos_ax, sin_ax, PERM_AXIAL, scale) + pair(k_norm_a, cos_ax, sin_ax, PERM_AXIAL, 1.0)
    for g in range(N_B):
        tabs += pair(q_norm_b[:, g], cos_sq, sin_sq, PERM_SEQ, scale)
        tabs += pair(k_norm_b[:, g], cos_sq, sin_sq, PERM_SEQ, 1.0)
    return jnp.concatenate(tabs, axis=1)


def _split_w_in(w_in):
    sizes = [A_Q_W, A_KV_W, A_KV_W] + [B_W] * (3 * N_B) + [D_MODEL, D_MODEL]
    offs = np.concatenate([[0], np.cumsum(sizes)])
    col = lambda i: w_in[:, :, offs[i]:offs[i + 1]]
    names = ["qa", "ka", "va"] + [f"{t}b{g}" for g in range(N_B) for t in "qkv"] + ["ga", "gb"]
    parts = {n: col(i) for i, n in enumerate(names)}
    wn = jnp.concatenate([parts["vb0"], parts["vb1"], parts["vb2"], parts["ga"], parts["gb"]], axis=-1)
    wt = jnp.concatenate([parts[n] for n, _ in T_SEGS], axis=-1)
    return wn.astype(BF16), jnp.swapaxes(wt, 1, 2).astype(BF16)


def kernel(x, c, w_ada, b_ada, g_mix, g_mlp, w_in, q_norm_a, k_norm_a, q_norm_b, k_norm_b,
           w_branch_a, w_branch_b, w_out, w_ff1, w_ff2):
    nb, seq, _ = x.shape
    depth = w_in.shape[0]
    assert seq % GRID_W == 0 and seq % TM_MIX == 0 and seq % TM_IN == 0

    ada = _ada_all(c, w_ada, b_ada).reshape(depth, nb, 6, D_MODEL)
    tabs = _gain_tables(q_norm_a, k_norm_a, q_norm_b, k_norm_b, seq)
    wn, wt = _split_w_in(w_in)
    wa, wb, wo = w_branch_a.astype(BF16), w_branch_b.astype(BF16), w_out.astype(BF16)
    w1, w2 = w_ff1.astype(BF16), w_ff2.astype(BF16)
    g_mix = g_mix.reshape(depth, 1, D_MODEL)
    g_mlp = g_mlp.reshape(depth, 1, D_MODEL)

    for l in range(depth):
        (qat, ka, vat, qb0, kb0, vb0, qb1, kb1, vb1, qb2, kb2, vb2, ga, gb) = _in_proj(
            x, ada[l], g_mix, wn, wt, tabs, l)
        oa = _attn_a(qat, ka, vat)
        obs, lss = [], []
        for (window, dil), q, k, v in zip(B_GROUPS, (qb0, qb1, qb2), (kb0, kb1, kb2), (vb0, vb1, vb2)):
            o, lse = _attn_b(q, k, v, window, dil)
            obs.append(o)
            lss.append(lse)
        x = _mix_mlp(x, ada[l], oa, obs, lss, ga, gb, wa, wb, wo, g_mlp, w1, w2, l)
    return x
```

```python
import functools

import jax
import jax.numpy as jnp
import numpy as np
from jax import lax
from jax.experimental import pallas as pl
from jax.experimental.pallas import tpu as pltpu

D_MODEL = 1024
HEAD_DIM = 64
A_Q_HEADS = 8
A_KV_HEADS = 2
B_GROUPS = ((128, 1), (512, 4), (2048, 16))
B_HEADS = 4
N_B = len(B_GROUPS)
D_FF = 4 * D_MODEL
GRID_W = 64
ROPE_THETA = 10000.0
EPS = 1e-6

A_Q_W = A_Q_HEADS * HEAD_DIM
A_KV_W = A_KV_HEADS * HEAD_DIM
B_W = B_HEADS * HEAD_DIM

V7X_LANES = 128
V7X_VMEM_BYTES = 64 * 1024 * 1024
VMEM_LIMIT = 52 * 1024 * 1024

BF16 = jnp.bfloat16
F32 = jnp.float32

CHUNK = 256
TM_IN = 512
TQ_A = 512
A_PAIRS_PER_BODY = 2
TM_MIX = 512
FF_CHUNK = 1024
B_ROWS_PER_STEP = 1024
TQ_B = 128
KW_B = 256
NEG_BIG = -1e30
LOG2_E = 1.4426950408889634

PERM_AXIAL = np.concatenate([np.arange(16, 32), np.arange(0, 16), np.arange(48, 64), np.arange(32, 48)])
PERM_SEQ = np.concatenate([np.arange(32, 64), np.arange(0, 32)])

T_SEGS = (("qa", A_Q_W), ("ka", A_KV_W), ("va", A_KV_W),
          ("qb0", B_W), ("kb0", B_W), ("qb1", B_W), ("kb1", B_W), ("qb2", B_W), ("kb2", B_W))
T_ROWS = sum(r for _, r in T_SEGS)
N_COLS = N_B * B_W + 2 * D_MODEL
N_TABS = 16


def _cparams(sem):
    return pltpu.CompilerParams(dimension_semantics=sem, vmem_limit_bytes=VMEM_LIMIT)


def _ada_kernel(c_ref, w_ref, b_ref, o_ref):
    c = c_ref[...]
    c_act = c * (1.0 / (1.0 + jnp.exp(-c)))
    o_ref[...] = jnp.dot(c_act, w_ref[...], preferred_element_type=F32,
                         precision=lax.Precision.HIGHEST) + b_ref[...]


def _ada_all(c, w_ada, b_ada):
    depth = w_ada.shape[0]
    nb = c.shape[0]
    return pl.pallas_call(
        _ada_kernel,
        grid=(depth, 6),
        in_specs=[
            pl.BlockSpec((nb, D_MODEL), lambda l, j: (0, 0)),
            pl.BlockSpec((None, D_MODEL, D_MODEL), lambda l, j: (l, 0, j)),
            pl.BlockSpec((None, 1, D_MODEL), lambda l, j: (l, 0, j)),
        ],
        out_specs=pl.BlockSpec((None, nb, D_MODEL), lambda l, j: (l, 0, j)),
        out_shape=jax.ShapeDtypeStruct((depth, nb, 6 * D_MODEL), F32),
        compiler_params=_cparams(("arbitrary", "arbitrary")),
        name="ada",
    )(c, w_ada, b_ada.reshape(depth, 1, 6 * D_MODEL))


def _modulated_rms(x, g, scale, shift):
    ms = jnp.mean(x * x, axis=-1, keepdims=True)
    return (x * lax.rsqrt(ms + EPS)) * g * (1.0 + scale) + shift


def _norm_rope_t(q, cos_t, sin_t, perm_blocks):
    ss = jnp.sum(q * q, axis=0, keepdims=True)
    r = lax.rsqrt(ss * (1.0 / HEAD_DIM) + EPS)
    qp = jnp.concatenate([q[a:b] for a, b in perm_blocks], axis=0)
    return (q * cos_t + qp * sin_t) * r


AXIAL_BLOCKS = ((16, 32), (0, 16), (48, 64), (32, 48))
SEQ_BLOCKS = ((32, 64), (0, 32))


def _store_class_major(out_ref, stage_ref, tile, dil, sub):
    rows = tile.shape[0] // dil
    out_rows = slice(sub * rows, (sub + 1) * rows)
    if dil == 1:
        out_ref[out_rows, :] = tile.astype(BF16)
        return
    for half in range(B_W // V7X_LANES):
        stage_ref[sub, half] = tile[:, half * V7X_LANES:(half + 1) * V7X_LANES]
    for r in range(dil):
        for half in range(B_W // V7X_LANES):
            col = r * B_W + half * V7X_LANES
            out_ref[out_rows, col:col + V7X_LANES] = (
                stage_ref[sub, half, pl.ds(r, rows, stride=dil), :].astype(BF16))


def _in_kernel(x_ref, ada_ref, g_ref, wn_ref, wt_ref, tab_ref,
               qat_ref, ka_ref, vat_ref,
               qb0_ref, kb0_ref, vb0_ref, qb1_ref, kb1_ref, vb1_ref, qb2_ref, kb2_ref, vb2_ref,
               ga_ref, gb_ref, stage_q_ref, stage_k_ref, stage_v_ref):
    for sub in range(TM_IN // CHUNK):
        _in_subtile(sub, x_ref, ada_ref, g_ref, wn_ref, wt_ref, tab_ref, qat_ref, ka_ref, vat_ref,
                    (qb0_ref, qb1_ref, qb2_ref), (kb0_ref, kb1_ref, kb2_ref), (vb0_ref, vb1_ref, vb2_ref),
                    ga_ref, gb_ref, stage_q_ref, stage_k_ref, stage_v_ref)


def _in_subtile(sub, x_ref, ada_ref, g_ref, wn_ref, wt_ref, tab_ref, qat_ref, ka_ref, vat_ref,
                qb_refs, kb_refs, vb_refs, ga_ref, gb_ref, stage_q_ref, stage_k_ref, stage_v_ref):
    tok = slice(sub * CHUNK, (sub + 1) * CHUNK)
    x = x_ref[tok, :]
    h = _modulated_rms(x, g_ref[...], ada_ref[1:2, :], ada_ref[0:1, :]).astype(BF16)

    nt = (((1,), (1,)), ((), ()))

    def proj_t(row0, rows):
        return lax.dot_general(wt_ref[row0:row0 + rows, :], h, nt, preferred_element_type=F32)

    def heads_t(rt, n_heads, tab0, blocks):
        cos_t = tab_ref[tab0 * HEAD_DIM:(tab0 + 1) * HEAD_DIM, tok]
        sin_t = tab_ref[(tab0 + 1) * HEAD_DIM:(tab0 + 2) * HEAD_DIM, tok]
        return [_norm_rope_t(rt[i * HEAD_DIM:(i + 1) * HEAD_DIM], cos_t, sin_t, blocks)
                for i in range(n_heads)]

    row = 0
    rt = proj_t(row, A_Q_W)
    qa = heads_t(rt, A_Q_HEADS, 0, AXIAL_BLOCKS)
    for i in range(A_Q_HEADS):
        qat_ref[sub, i * HEAD_DIM:(i + 1) * HEAD_DIM, :] = qa[i].astype(BF16)
    row += A_Q_W

    rt = proj_t(row, 2 * A_KV_W)
    ka = heads_t(rt[:A_KV_W], A_KV_HEADS, 2, AXIAL_BLOCKS)
    ka_ref[tok, :] = jnp.concatenate(ka, axis=0).T.astype(BF16)
    vat_ref[sub] = rt[A_KV_W:].astype(BF16)
    row += 2 * A_KV_W

    for g in range(N_B):
        rt = proj_t(row, 2 * B_W)
        qb = heads_t(rt[:B_W], B_HEADS, 4 + 4 * g, SEQ_BLOCKS)
        kb = heads_t(rt[B_W:], B_HEADS, 6 + 4 * g, SEQ_BLOCKS)
        _store_class_major(qb_refs[g], stage_q_ref, jnp.concatenate(qb, axis=0).T, B_GROUPS[g][1], sub)
        _store_class_major(kb_refs[g], stage_k_ref, jnp.concatenate(kb, axis=0).T, B_GROUPS[g][1], sub)
        row += 2 * B_W

    for i, gate_ref in enumerate((ga_ref, gb_ref)):
        lo = N_B * B_W + i * D_MODEL
        z = jnp.dot(h, wn_ref[:, lo:lo + D_MODEL], preferred_element_type=F32)
        gate_ref[tok, :] = (1.0 / (1.0 + jnp.exp(-z))).astype(BF16)
    for g in reversed(range(N_B)):
        v = jnp.dot(h, wn_ref[:, g * B_W:(g + 1) * B_W], preferred_element_type=F32)
        _store_class_major(vb_refs[g], stage_v_ref, v, B_GROUPS[g][1], sub)


def _in_proj(x, ada_l, g_mix_l, wn, wt, tabs, l):
    nb, seq, _ = x.shape
    ns = seq // TM_IN
    nsub = TM_IN // CHUNK
    tok = lambda w: pl.BlockSpec((None, TM_IN, w), lambda s, b: (b, s, 0))
    tchunk = lambda r: pl.BlockSpec((None, nsub, r, CHUNK), lambda s, b: (b, s, 0, 0))
    nat = lambda w: jax.ShapeDtypeStruct((nb, seq, w), BF16)
    out_specs = [tchunk(A_Q_W), tok(A_KV_W), tchunk(A_KV_W)]
    out_shape = [jax.ShapeDtypeStruct((nb, seq // CHUNK, A_Q_W, CHUNK), BF16), nat(A_KV_W),
                 jax.ShapeDtypeStruct((nb, seq // CHUNK, A_KV_W, CHUNK), BF16)]
    for _, dil in B_GROUPS:
        out_specs += [pl.BlockSpec((None, TM_IN // dil, dil * B_W), lambda s, b: (b, s, 0))] * 3
        out_shape += [jax.ShapeDtypeStruct((nb, seq // dil, dil * B_W), BF16)] * 3
    out_specs += [tok(D_MODEL)] * 2
    out_shape += [nat(D_MODEL)] * 2
    return pl.pallas_call(
        _in_kernel,
        grid=(ns, nb),
        in_specs=[
            tok(D_MODEL),
            pl.BlockSpec((None, 6, D_MODEL), lambda s, b: (b, 0, 0)),
            pl.BlockSpec((None, 1, D_MODEL), lambda s, b: (l, 0, 0)),
            pl.BlockSpec((None, D_MODEL, N_COLS), lambda s, b: (l, 0, 0), pipeline_mode=pl.Buffered(1)),
            pl.BlockSpec((None, T_ROWS, D_MODEL), lambda s, b: (l, 0, 0), pipeline_mode=pl.Buffered(1)),
            pl.BlockSpec((None, N_TABS * HEAD_DIM, TM_IN), lambda s, b: (l, 0, s)),
        ],
        out_specs=out_specs,
        out_shape=out_shape,
        scratch_shapes=[pltpu.VMEM((nsub, B_W // V7X_LANES, CHUNK, V7X_LANES), F32)] * 3,
        compiler_params=_cparams(("arbitrary", "arbitrary")),
        name="in_proj",
    )(x, ada_l, g_mix_l, wn, wt, tabs)


A_REP = A_Q_HEADS // A_KV_HEADS
ONES_ROWS = 16


def _attn_a_kernel(qt_ref, k_ref, vt_ref, o_ref, qpad_ref, acc_ref, m_ref, s_ref, smax_ref):
    for sub in range(qt_ref.shape[0]):
        _attn_a_tile(qt_ref.at[sub], k_ref, vt_ref, o_ref.at[sub * CHUNK:(sub + 1) * CHUNK],
                     qpad_ref.at[sub], acc_ref.at[sub], m_ref.at[sub], s_ref, smax_ref)


def _attn_a_tile(qt_ref, k_ref, vt_ref, o_ref, qpad_ref, acc_ref, m_ref, s_ref, smax_ref):
    n_chunks, _, tk = vt_ref.shape
    tq = qt_ref.shape[1]
    wide = A_REP * tq
    qpad_ref[...] = jnp.zeros(qpad_ref.shape, BF16)
    for g in range(A_KV_HEADS):
        for hh in range(A_REP):
            h = g * A_REP + hh
            qpad_ref[g, g * HEAD_DIM:(g + 1) * HEAD_DIM, hh * tq:(hh + 1) * tq] = (
                qt_ref[h * HEAD_DIM:(h + 1) * HEAD_DIM, :])
    m_ref[...] = jnp.full(m_ref.shape, NEG_BIG, F32)
    acc_ref[...] = jnp.zeros(acc_ref.shape, F32)
    ones = jnp.ones((ONES_ROWS, tk), BF16)

    def scores(c, slot, g):
        kc = k_ref[pl.ds(pl.multiple_of(c * tk, tk), tk), :]
        s = jnp.dot(kc, qpad_ref[g], preferred_element_type=F32)
        s_ref[slot, g, :, :wide] = s
        smax_ref[slot, g] = jnp.max(s, axis=0, keepdims=True)

    def consume(c, slot, g):
        m_prev = m_ref[g]
        m_new = jnp.maximum(m_prev, smax_ref[slot, g])
        alpha = jnp.exp2(m_prev - m_new)
        p = jnp.exp2(s_ref[slot, g, :, :wide] - m_new).astype(BF16)
        v_aug = jnp.concatenate([vt_ref[c, g * HEAD_DIM:(g + 1) * HEAD_DIM, :], ones], axis=0)
        acc_ref[g] = acc_ref[g] * alpha + jnp.dot(v_aug, p, preferred_element_type=F32)
        m_ref[g] = m_new

    for g in range(A_KV_HEADS):
        scores(0, 0, g)

    def chunk_pair(c, last):
        for slot in range(2):
            for g in range(A_KV_HEADS):
                if not (last and slot == 1):
                    scores(c + slot + 1, 1 - slot, g)
                consume(c + slot, slot, g)

    body_chunks = 2 * A_PAIRS_PER_BODY

    def loop_body(i, carry):
        for j in range(A_PAIRS_PER_BODY):
            chunk_pair(body_chunks * i + 2 * j, False)
        return carry

    lax.fori_loop(0, n_chunks // body_chunks - 1, loop_body, 0)
    for j in range(A_PAIRS_PER_BODY):
        chunk_pair(n_chunks - body_chunks + 2 * j, j == A_PAIRS_PER_BODY - 1)
    for j in range(A_Q_W // V7X_LANES):
        halves = []
        for h in (2 * j, 2 * j + 1):
            g, hh = divmod(h, A_REP)
            cols = slice(hh * tq, (hh + 1) * tq)
            halves.append(acc_ref[g, :HEAD_DIM, cols] / acc_ref[g, HEAD_DIM:HEAD_DIM + 1, cols])
        blk = jnp.concatenate(halves, axis=0)
        o_ref[:, j * V7X_LANES:(j + 1) * V7X_LANES] = blk.T.astype(BF16)


def _attn_a(qat, ka, vat):
    nb, ns, _, chunk = qat.shape
    seq = ka.shape[1]
    nq = TQ_A // chunk
    wide = A_REP * chunk
    return pl.pallas_call(
        _attn_a_kernel,
        grid=(nb, seq // TQ_A),
        in_specs=[
            pl.BlockSpec((None, nq, A_Q_W, chunk), lambda b, s: (b, s, 0, 0)),
            pl.BlockSpec((None, seq, A_KV_W), lambda b, s: (b, 0, 0)),
            pl.BlockSpec((None, ns, A_KV_W, chunk), lambda b, s: (b, 0, 0, 0)),
        ],
        out_specs=pl.BlockSpec((None, TQ_A, A_Q_W), lambda b, s: (b, s, 0)),
        out_shape=jax.ShapeDtypeStruct((nb, seq, A_Q_W), BF16),
        scratch_shapes=[pltpu.VMEM((nq, A_KV_HEADS, A_KV_W, wide), BF16),
                        pltpu.VMEM((nq, A_KV_HEADS, HEAD_DIM + ONES_ROWS, wide), F32),
                        pltpu.VMEM((nq, A_KV_HEADS, 1, wide), F32),
                        pltpu.VMEM((2, A_KV_HEADS, chunk, wide + V7X_LANES), F32),
                        pltpu.VMEM((2, A_KV_HEADS, 1, wide), F32)],
        compiler_params=_cparams(("arbitrary", "arbitrary")),
        name="attn_a",
    )(qat, ka, vat)


def _attn_b_kernel(q_ref, k_ref, v_ref, o_ref, lse_ref, *, half_window):
    tj = q_ref.shape[0]
    length = k_ref.shape[0]
    j_tile = pl.program_id(2) * tj
    lane = lax.broadcasted_iota(jnp.int32, (1, V7X_LANES), 1)
    first_head = lane < HEAD_DIM
    nt = (((1,), (1,)), ((), ()))
    ones = jnp.ones((KW_B, V7X_LANES), BF16)
    for qi in range(tj // TQ_B):
        j0 = j_tile + qi * TQ_B
        ks = jnp.clip(j0 - half_window, 0, length - KW_B)
        ks = pl.multiple_of(ks, half_window)
        jq = j0 + lax.broadcasted_iota(jnp.int32, (TQ_B, 1), 0)
        jk = ks + lax.broadcasted_iota(jnp.int32, (1, KW_B), 1)
        bias = jnp.where(jnp.abs(jq - jk) <= half_window, 0.0, NEG_BIG)
        for pair in range(q_ref.shape[1] // V7X_LANES):
            cols = slice(pair * V7X_LANES, (pair + 1) * V7X_LANES)
            q2 = q_ref[qi * TQ_B:(qi + 1) * TQ_B, cols]
            k2 = k_ref[pl.ds(ks, KW_B), cols]
            v_aug = jnp.concatenate([v_ref[pl.ds(ks, KW_B), cols], ones], axis=1)
            ms, ols = [], []
            for hh in range(2):
                keep = first_head if hh == 0 else jnp.logical_not(first_head)
                qm = jnp.where(keep, q2, jnp.zeros_like(q2))
                s = lax.dot_general(qm, k2, nt, preferred_element_type=F32) + bias
                m = jnp.max(s, axis=1, keepdims=True)
                p = jnp.exp2(s - m).astype(BF16)
                ols.append(jnp.dot(p, v_aug, preferred_element_type=F32))
                ms.append(m)
            o_pair = jnp.where(first_head, ols[0][:, :V7X_LANES], ols[1][:, :V7X_LANES])
            l_pair = jnp.where(first_head, ols[0][:, V7X_LANES:], ols[1][:, V7X_LANES:])
            m_pair = jnp.where(first_head, ms[0], ms[1])
            o_ref[qi * TQ_B:(qi + 1) * TQ_B, cols] = (o_pair / l_pair).astype(BF16)
            lse_ref[qi * TQ_B:(qi + 1) * TQ_B, cols] = m_pair + jnp.log2(l_pair)


def _attn_b(q, k, v, window, dil):
    nb, length, _ = q.shape
    half_window = (window // 2) // dil
    assert half_window * 2 + TQ_B == KW_B and length >= KW_B
    tj = min(B_ROWS_PER_STEP, length)
    ncls = min(B_ROWS_PER_STEP // tj, dil)
    qspec = pl.BlockSpec((None, tj, ncls * B_W), lambda b, r, j: (b, j, r))
    kvspec = pl.BlockSpec((None, length, ncls * B_W), lambda b, r, j: (b, 0, r))
    return pl.pallas_call(
        functools.partial(_attn_b_kernel, half_window=half_window),
        grid=(nb, dil // ncls, length // tj),
        in_specs=[qspec, kvspec, kvspec],
        out_specs=[qspec, qspec],
        out_shape=[jax.ShapeDtypeStruct((nb, length, dil * B_W), BF16),
                   jax.ShapeDtypeStruct((nb, length, dil * B_W), F32)],
        compiler_params=_cparams(("arbitrary", "arbitrary", "arbitrary")),
        name=f"attn_b_d{dil}",
    )(q, k, v)


def _load_class_major(blk_ref, stage_ref, dil, sub):
    rows = CHUNK // dil
    blk_rows = slice(sub * rows, (sub + 1) * rows)
    if dil == 1:
        return blk_ref[blk_rows, :].astype(F32)
    halves = range(B_W // V7X_LANES)
    for r in range(dil):
        for half in halves:
            col = r * B_W + half * V7X_LANES
            stage_ref[sub, half, pl.ds(r, rows, stride=dil), :] = (
                blk_ref[blk_rows, col:col + V7X_LANES].astype(F32))
    return jnp.concatenate([stage_ref[sub, half] for half in halves], axis=1)


def _mix_mlp_kernel(x_ref, ada_ref, oa_ref, ob0_ref, ob1_ref, ob2_ref, ls0_ref, ls1_ref, ls2_ref,
                    ga_ref, gb_ref, wa_ref, wb_ref, wo_ref, g_ref, w1_ref, w2_ref, xo_ref, *stage_refs):
    dils = [d for _, d in B_GROUPS]
    for sub in range(TM_MIX // CHUNK):
        tok = slice(sub * CHUNK, (sub + 1) * CHUNK)
        ya = jnp.dot(oa_ref[tok, :], wa_ref[...], preferred_element_type=F32)
        lses = [_load_class_major(r, stage_refs[2 * g], dils[g], sub)
                for g, r in enumerate((ls0_ref, ls1_ref, ls2_ref))]
        outs = [_load_class_major(r, stage_refs[2 * g + 1], dils[g], sub)
                for g, r in enumerate((ob0_ref, ob1_ref, ob2_ref))]
        top = jnp.maximum(jnp.maximum(lses[0], lses[1]), lses[2])
        es = [jnp.exp2(v - top) for v in lses]
        den = es[0] + es[1] + es[2]
        ob = (es[0] * outs[0] + es[1] * outs[1] + es[2] * outs[2]) / den
        yb = jnp.dot(ob.astype(BF16), wb_ref[...], preferred_element_type=F32)
        mixed = ga_ref[tok, :].astype(F32) * ya + gb_ref[tok, :].astype(F32) * yb
        upd = jnp.dot(mixed.astype(BF16), wo_ref[...], preferred_element_type=F32)
        xo_ref[tok, :] = x_ref[tok, :] + ada_ref[2:3, :] * upd

    x = xo_ref[...]
    h = _modulated_rms(x, g_ref[...], ada_ref[4:5, :], ada_ref[3:4, :]).astype(BF16)
    acc = jnp.zeros(x.shape, F32)
    for c in range(D_FF // FF_CHUNK):
        a = jnp.dot(h, w1_ref[:, c * FF_CHUNK:(c + 1) * FF_CHUNK], preferred_element_type=F32)
        a = jnp.square(jnp.maximum(a, 0.0)).astype(BF16)
        acc = acc + jnp.dot(a, w2_ref[c * FF_CHUNK:(c + 1) * FF_CHUNK, :], preferred_element_type=F32)
    xo_ref[...] = x + ada_ref[5:6, :] * acc


def _mix_mlp(x, ada_l, oa, obs, lss, ga, gb, wa, wb, wo, g_mlp_l, w1, w2, l):
    nb, seq, _ = x.shape
    tok = lambda w: pl.BlockSpec((None, TM_MIX, w), lambda b, s: (b, s, 0))
    wspec = lambda r, c: pl.BlockSpec((None, r, c), lambda b, s: (l, 0, 0), pipeline_mode=pl.Buffered(1))
    cls = [pl.BlockSpec((None, TM_MIX // d, d * B_W), lambda b, s: (b, s, 0)) for _, d in B_GROUPS]
    return pl.pallas_call(
        _mix_mlp_kernel,
        grid=(nb, seq // TM_MIX),
        in_specs=[tok(D_MODEL), pl.BlockSpec((None, 6, D_MODEL), lambda b, s: (b, 0, 0)), tok(A_Q_W)]
                 + cls + cls + [tok(D_MODEL)] * 2
                 + [wspec(A_Q_W, D_MODEL), wspec(B_W, D_MODEL), wspec(D_MODEL, D_MODEL),
                    pl.BlockSpec((None, 1, D_MODEL), lambda b, s: (l, 0, 0)),
                    wspec(D_MODEL, D_FF), wspec(D_FF, D_MODEL)],
        out_specs=tok(D_MODEL),
        out_shape=jax.ShapeDtypeStruct(x.shape, F32),
        scratch_shapes=[pltpu.VMEM((TM_MIX // CHUNK, B_W // V7X_LANES, CHUNK, V7X_LANES), F32)] * (2 * N_B),
        input_output_aliases={0: 0} if l > 0 else {},
        compiler_params=_cparams(("arbitrary", "arbitrary")),
        name="mix_mlp",
    )(x, ada_l, oa, *obs, *lss, ga, gb, wa, wb, wo, g_mlp_l, w1, w2)


def _rope_tables(seq):
    pos = jnp.arange(seq)
    rows = seq // GRID_W
    row_idx = jnp.broadcast_to(jnp.arange(rows)[:, None], (rows, GRID_W)).reshape(-1)
    col_idx = jnp.broadcast_to(jnp.arange(GRID_W)[None, :], (rows, GRID_W)).reshape(-1)

    def cos_sin(p, dim):
        inv = ROPE_THETA ** (-jnp.arange(0, dim, 2, dtype=F32) / dim)
        ang = p.astype(F32)[:, None] * inv[None, :]
        return jnp.cos(ang), jnp.sin(ang)

    rc, rs = cos_sin(row_idx, HEAD_DIM // 2)
    cc, cs = cos_sin(col_idx, HEAD_DIM // 2)
    sc, ss = cos_sin(pos, HEAD_DIM)
    cos_ax = jnp.concatenate([rc, rc, cc, cc], axis=-1)
    sin_ax = jnp.concatenate([-rs, rs, -cs, cs], axis=-1)
    cos_sq = jnp.concatenate([sc, sc], axis=-1)
    sin_sq = jnp.concatenate([-ss, ss], axis=-1)
    return (cos_ax, sin_ax), (cos_sq, sin_sq)


def _gain_tables(q_norm_a, k_norm_a, q_norm_b, k_norm_b, seq):
    (cos_ax, sin_ax), (cos_sq, sin_sq) = _rope_tables(seq)
    scale = HEAD_DIM ** -0.5 * LOG2_E

    def pair(gain, cos, sin, perm, mult):
        c = gain[:, None, :] * cos[None] * mult
        s = gain[:, perm][:, None, :] * sin[None] * mult
        return [jnp.swapaxes(c, 1, 2), jnp.swapaxes(s, 1, 2)]

    tabs = pair(q_norm_a, cos_ax, sin_ax, PERM_AXIAL, scale) + pair(k_norm_a, cos_ax, sin_ax, PERM_AXIAL, 1.0)
    for g in range(N_B):
        tabs += pair(q_norm_b[:, g], cos_sq, sin_sq, PERM_SEQ, scale)
        tabs += pair(k_norm_b[:, g], cos_sq, sin_sq, PERM_SEQ, 1.0)
    return jnp.concatenate(tabs, axis=1)


def _split_w_in(w_in):
    sizes = [A_Q_W, A_KV_W, A_KV_W] + [B_W] * (3 * N_B) + [D_MODEL, D_MODEL]
    offs = np.concatenate([[0], np.cumsum(sizes)])
    col = lambda i: w_in[:, :, offs[i]:offs[i + 1]]
    names = ["qa", "ka", "va"] + [f"{t}b{g}" for g in range(N_B) for t in "qkv"] + ["ga", "gb"]
    parts = {n: col(i) for i, n in enumerate(names)}
    wn = jnp.concatenate([parts["vb0"], parts["vb1"], parts["vb2"], parts["ga"], parts["gb"]], axis=-1)
    wt = jnp.concatenate([parts[n] for n, _ in T_SEGS], axis=-1)
    return wn.astype(BF16), jnp.swapaxes(wt, 1, 2).astype(BF16)


def kernel(x, c, w_ada, b_ada, g_mix, g_mlp, w_in, q_norm_a, k_norm_a, q_norm_b, k_norm_b,
           w_branch_a, w_branch_b, w_out, w_ff1, w_ff2):
    nb, seq, _ = x.shape
    depth = w_in.shape[0]
    assert seq % GRID_W == 0 and seq % TM_MIX == 0 and seq % TM_IN == 0

    ada = _ada_all(c, w_ada, b_ada).reshape(depth, nb, 6, D_MODEL)
    tabs = _gain_tables(q_norm_a, k_norm_a, q_norm_b, k_norm_b, seq)
    wn, wt = _split_w_in(w_in)
    wa, wb, wo = w_branch_a.astype(BF16), w_branch_b.astype(BF16), w_out.astype(BF16)
    w1, w2 = w_ff1.astype(BF16), w_ff2.astype(BF16)
    g_mix = g_mix.reshape(depth, 1, D_MODEL)
    g_mlp = g_mlp.reshape(depth, 1, D_MODEL)

    for l in range(depth):
        (qat, ka, vat, qb0, kb0, vb0, qb1, kb1, vb1, qb2, kb2, vb2, ga, gb) = _in_proj(
            x, ada[l], g_mix, wn, wt, tabs, l)
        oa = _attn_a(qat, ka, vat)
        obs, lss = [], []
        for (window, dil), q, k, v in zip(B_GROUPS, (qb0, qb1, qb2), (kb0, kb1, kb2), (vb0, vb1, vb2)):
            o, lse = _attn_b(q, k, v, window, dil)
            obs.append(o)
            lss.append(lse)
        x = _mix_mlp(x, ada[l], oa, obs, lss, ga, gb, wa, wb, wo, g_mlp, w1, w2, l)
    return x
```

```python
import functools

import jax
import jax.numpy as jnp
import numpy as np
from jax import lax
from jax.experimental import pallas as pl
from jax.experimental.pallas import tpu as pltpu

D_MODEL = 1024
HEAD_DIM = 64
A_Q_HEADS = 8
A_KV_HEADS = 2
B_GROUPS = ((128, 1), (512, 4), (2048, 16))
B_HEADS = 4
N_B = len(B_GROUPS)
D_FF = 4 * D_MODEL
GRID_W = 64
ROPE_THETA = 10000.0
EPS = 1e-6

A_Q_W = A_Q_HEADS * HEAD_DIM
A_KV_W = A_KV_HEADS * HEAD_DIM
B_W = B_HEADS * HEAD_DIM

V7X_LANES = 128
V7X_VMEM_BYTES = 64 * 1024 * 1024
VMEM_LIMIT = 52 * 1024 * 1024

BF16 = jnp.bfloat16
F32 = jnp.float32

CHUNK = 256
TM_IN = 512
TQ_A = 1024
A_PAIRS_PER_BODY = 3
TM_MIX = 512
FF_CHUNK = 1024
B_ROWS_PER_STEP = 1024
TQ_B = 128
KW_B = 256
NEG_BIG = -1e30
LOG2_E = 1.4426950408889634

PERM_AXIAL = np.concatenate([np.arange(16, 32), np.arange(0, 16), np.arange(48, 64), np.arange(32, 48)])
PERM_SEQ = np.concatenate([np.arange(32, 64), np.arange(0, 32)])

T_SEGS = (("qa", A_Q_W), ("ka", A_KV_W), ("va", A_KV_W),
          ("qb0", B_W), ("kb0", B_W), ("qb1", B_W), ("kb1", B_W), ("qb2", B_W), ("kb2", B_W))
T_ROWS = sum(r for _, r in T_SEGS)
N_COLS = N_B * B_W + 2 * D_MODEL
N_TABS = 16


def _cparams(sem):
    return pltpu.CompilerParams(dimension_semantics=sem, vmem_limit_bytes=VMEM_LIMIT)


def _ada_kernel(c_ref, w_ref, b_ref, o_ref):
    c = c_ref[...]
    c_act = c * (1.0 / (1.0 + jnp.exp(-c)))
    o_ref[...] = jnp.dot(c_act, w_ref[...], preferred_element_type=F32,
                         precision=lax.Precision.HIGHEST) + b_ref[...]


def _ada_all(c, w_ada, b_ada):
    depth = w_ada.shape[0]
    nb = c.shape[0]
    return pl.pallas_call(
        _ada_kernel,
        grid=(depth, 6),
        in_specs=[
            pl.BlockSpec((nb, D_MODEL), lambda l, j: (0, 0)),
            pl.BlockSpec((None, D_MODEL, D_MODEL), lambda l, j: (l, 0, j)),
            pl.BlockSpec((None, 1, D_MODEL), lambda l, j: (l, 0, j)),
        ],
        out_specs=pl.BlockSpec((None, nb, D_MODEL), lambda l, j: (l, 0, j)),
        out_shape=jax.ShapeDtypeStruct((depth, nb, 6 * D_MODEL), F32),
        compiler_params=_cparams(("arbitrary", "arbitrary")),
        name="ada",
    )(c, w_ada, b_ada.reshape(depth, 1, 6 * D_MODEL))


def _modulated_rms(x, g, scale, shift):
    ms = jnp.mean(x * x, axis=-1, keepdims=True)
    return (x * lax.rsqrt(ms + EPS)) * g * (1.0 + scale) + shift


def _norm_rope_t(q, cos_t, sin_t, perm_blocks):
    ss = jnp.sum(q * q, axis=0, keepdims=True)
    r = lax.rsqrt(ss * (1.0 / HEAD_DIM) + EPS)
    qp = jnp.concatenate([q[a:b] for a, b in perm_blocks], axis=0)
    return (q * cos_t + qp * sin_t) * r


AXIAL_BLOCKS = ((16, 32), (0, 16), (48, 64), (32, 48))
SEQ_BLOCKS = ((32, 64), (0, 32))


def _store_class_major(out_ref, stage_ref, tile, dil, sub):
    rows = tile.shape[0] // dil
    out_rows = slice(sub * rows, (sub + 1) * rows)
    if dil == 1:
        out_ref[out_rows, :] = tile.astype(BF16)
        return
    for half in range(B_W // V7X_LANES):
        stage_ref[sub, half] = tile[:, half * V7X_LANES:(half + 1) * V7X_LANES]
    for r in range(dil):
        for half in range(B_W // V7X_LANES):
            col = r * B_W + half * V7X_LANES
            out_ref[out_rows, col:col + V7X_LANES] = (
                stage_ref[sub, half, pl.ds(r, rows, stride=dil), :].astype(BF16))


def _in_kernel(x_ref, ada_ref, g_ref, wn_ref, wt_ref, tab_ref,
               qat_ref, ka_ref, vat_ref,
               qb0_ref, kb0_ref, vb0_ref, qb1_ref, kb1_ref, vb1_ref, qb2_ref, kb2_ref, vb2_ref,
               ga_ref, gb_ref, stage_q_ref, stage_k_ref, stage_v_ref):
    for sub in range(TM_IN // CHUNK):
        _in_subtile(sub, x_ref, ada_ref, g_ref, wn_ref, wt_ref, tab_ref, qat_ref, ka_ref, vat_ref,
                    (qb0_ref, qb1_ref, qb2_ref), (kb0_ref, kb1_ref, kb2_ref), (vb0_ref, vb1_ref, vb2_ref),
                    ga_ref, gb_ref, stage_q_ref, stage_k_ref, stage_v_ref)


def _in_subtile(sub, x_ref, ada_ref, g_ref, wn_ref, wt_ref, tab_ref, qat_ref, ka_ref, vat_ref,
                qb_refs, kb_refs, vb_refs, ga_ref, gb_ref, stage_q_ref, stage_k_ref, stage_v_ref):
    tok = slice(sub * CHUNK, (sub + 1) * CHUNK)
    x = x_ref[tok, :]
    h = _modulated_rms(x, g_ref[...], ada_ref[1:2, :], ada_ref[0:1, :]).astype(BF16)

    nt = (((1,), (1,)), ((), ()))

    def proj_t(row0, rows):
        return lax.dot_general(wt_ref[row0:row0 + rows, :], h, nt, preferred_element_type=F32)

    def heads_t(rt, n_heads, tab0, blocks):
        cos_t = tab_ref[tab0 * HEAD_DIM:(tab0 + 1) * HEAD_DIM, tok]
        sin_t = tab_ref[(tab0 + 1) * HEAD_DIM:(tab0 + 2) * HEAD_DIM, tok]
        return [_norm_rope_t(rt[i * HEAD_DIM:(i + 1) * HEAD_DIM], cos_t, sin_t, blocks)
                for i in range(n_heads)]

    row = 0
    rt = proj_t(row, A_Q_W)
    qa = heads_t(rt, A_Q_HEADS, 0, AXIAL_BLOCKS)
    for i in range(A_Q_HEADS):
        qat_ref[sub, i * HEAD_DIM:(i + 1) * HEAD_DIM, :] = qa[i].astype(BF16)
    row += A_Q_W

    rt = proj_t(row, 2 * A_KV_W)
    ka = heads_t(rt[:A_KV_W], A_KV_HEADS, 2, AXIAL_BLOCKS)
    ka_ref[tok, :] = jnp.concatenate(ka, axis=0).T.astype(BF16)
    vat_ref[sub] = rt[A_KV_W:].astype(BF16)
    row += 2 * A_KV_W

    for g in range(N_B):
        rt = proj_t(row, 2 * B_W)
        qb = heads_t(rt[:B_W], B_HEADS, 4 + 4 * g, SEQ_BLOCKS)
        kb = heads_t(rt[B_W:], B_HEADS, 6 + 4 * g, SEQ_BLOCKS)
        _store_class_major(qb_refs[g], stage_q_ref, jnp.concatenate(qb, axis=0).T, B_GROUPS[g][1], sub)
        _store_class_major(kb_refs[g], stage_k_ref, jnp.concatenate(kb, axis=0).T, B_GROUPS[g][1], sub)
        row += 2 * B_W

    for i, gate_ref in enumerate((ga_ref, gb_ref)):
        lo = N_B * B_W + i * D_MODEL
        z = jnp.dot(h, wn_ref[:, lo:lo + D_MODEL], preferred_element_type=F32)
        gate_ref[tok, :] = (1.0 / (1.0 + jnp.exp(-z))).astype(BF16)
    for g in reversed(range(N_B)):
        v = jnp.dot(h, wn_ref[:, g * B_W:(g + 1) * B_W], preferred_element_type=F32)
        _store_class_major(vb_refs[g], stage_v_ref, v, B_GROUPS[g][1], sub)


def _in_proj(x, ada_l, g_mix_l, wn, wt, tabs, l):
    nb, seq, _ = x.shape
    ns = seq // TM_IN
    nsub = TM_IN // CHUNK
    tok = lambda w: pl.BlockSpec((None, TM_IN, w), lambda s, b: (b, s, 0))
    tchunk = lambda r: pl.BlockSpec((None, nsub, r, CHUNK), lambda s, b: (b, s, 0, 0))
    nat = lambda w: jax.ShapeDtypeStruct((nb, seq, w), BF16)
    out_specs = [tchunk(A_Q_W), tok(A_KV_W), tchunk(A_KV_W)]
    out_shape = [jax.ShapeDtypeStruct((nb, seq // CHUNK, A_Q_W, CHUNK), BF16), nat(A_KV_W),
                 jax.ShapeDtypeStruct((nb, seq // CHUNK, A_KV_W, CHUNK), BF16)]
    for _, dil in B_GROUPS:
        out_specs += [pl.BlockSpec((None, TM_IN // dil, dil * B_W), lambda s, b: (b, s, 0))] * 3
        out_shape += [jax.ShapeDtypeStruct((nb, seq // dil, dil * B_W), BF16)] * 3
    out_specs += [tok(D_MODEL)] * 2
    out_shape += [nat(D_MODEL)] * 2
    return pl.pallas_call(
        _in_kernel,
        grid=(ns, nb),
        in_specs=[
            tok(D_MODEL),
            pl.BlockSpec((None, 6, D_MODEL), lambda s, b: (b, 0, 0)),
            pl.BlockSpec((None, 1, D_MODEL), lambda s, b: (l, 0, 0)),
            pl.BlockSpec((None, D_MODEL, N_COLS), lambda s, b: (l, 0, 0), pipeline_mode=pl.Buffered(1)),
            pl.BlockSpec((None, T_ROWS, D_MODEL), lambda s, b: (l, 0, 0), pipeline_mode=pl.Buffered(1)),
            pl.BlockSpec((None, N_TABS * HEAD_DIM, TM_IN), lambda s, b: (l, 0, s)),
        ],
        out_specs=out_specs,
        out_shape=out_shape,
        scratch_shapes=[pltpu.VMEM((nsub, B_W // V7X_LANES, CHUNK, V7X_LANES), F32)] * 3,
        compiler_params=_cparams(("arbitrary", "arbitrary")),
        name="in_proj",
    )(x, ada_l, g_mix_l, wn, wt, tabs)


A_REP = A_Q_HEADS // A_KV_HEADS
ONES_ROWS = 16


def _attn_a_kernel(qt_ref, k_ref, vt_ref, o_ref, qpad_ref, acc_ref, m_ref, s_ref, smax_ref):
    for sub in range(qt_ref.shape[0]):
        _attn_a_tile(qt_ref.at[sub], k_ref, vt_ref, o_ref.at[sub * CHUNK:(sub + 1) * CHUNK],
                     qpad_ref.at[sub], acc_ref.at[sub], m_ref.at[sub], s_ref, smax_ref)


def _attn_a_tile(qt_ref, k_ref, vt_ref, o_ref, qpad_ref, acc_ref, m_ref, s_ref, smax_ref):
    n_chunks, _, tk = vt_ref.shape
    tq = qt_ref.shape[1]
    qpad_ref[...] = jnp.zeros(qpad_ref.shape, BF16)
    for g in range(A_KV_HEADS):
        for hh in range(A_REP):
            h = g * A_REP + hh
            qpad_ref[g, g * HEAD_DIM:(g + 1) * HEAD_DIM, hh * tq:(hh + 1) * tq] = (
                qt_ref[h * HEAD_DIM:(h + 1) * HEAD_DIM, :])
    m_ref[...] = jnp.full(m_ref.shape, NEG_BIG, F32)
    acc_ref[...] = jnp.zeros(acc_ref.shape, F32)
    ones = jnp.ones((ONES_ROWS, tk), BF16)

    def scores(c, slot, g):
        kc = k_ref[pl.ds(pl.multiple_of(c * tk, tk), tk), :]
        s = jnp.dot(kc, qpad_ref[g], preferred_element_type=F32)
        s_ref[slot, g] = s
        smax_ref[slot, g] = jnp.max(s, axis=0, keepdims=True)

    def consume(c, slot, g):
        m_prev = m_ref[g]
        m_new = jnp.maximum(m_prev, smax_ref[slot, g])
        alpha = jnp.exp2(m_prev - m_new)
        p = jnp.exp2(s_ref[slot, g] - m_new).astype(BF16)
        v_aug = jnp.concatenate([vt_ref[c, g * HEAD_DIM:(g + 1) * HEAD_DIM, :], ones], axis=0)
        acc_ref[g] = acc_ref[g] * alpha + jnp.dot(v_aug, p, preferred_element_type=F32)
        m_ref[g] = m_new

    for g in range(A_KV_HEADS):
        scores(0, 0, g)

    def chunk_pair(c, last):
        for slot in range(2):
            for g in range(A_KV_HEADS):
                if not (last and slot == 1):
                    scores(c + slot + 1, 1 - slot, g)
                consume(c + slot, slot, g)

    body_chunks = 2 * A_PAIRS_PER_BODY

    def loop_body(i, carry):
        for j in range(A_PAIRS_PER_BODY):
            chunk_pair(body_chunks * i + 2 * j, False)
        return carry

    n_loop = (n_chunks - 2) // body_chunks
    lax.fori_loop(0, n_loop, loop_body, 0)
    for c in range(n_loop * body_chunks, n_chunks, 2):
        chunk_pair(c, c == n_chunks - 2)
    for j in range(A_Q_W // V7X_LANES):
        halves = []
        for h in (2 * j, 2 * j + 1):
            g, hh = divmod(h, A_REP)
            cols = slice(hh * tq, (hh + 1) * tq)
            halves.append(acc_ref[g, :HEAD_DIM, cols] / acc_ref[g, HEAD_DIM:HEAD_DIM + 1, cols])
        blk = jnp.concatenate(halves, axis=0)
        o_ref[:, j * V7X_LANES:(j + 1) * V7X_LANES] = blk.T.astype(BF16)


def _attn_a(qat, ka, vat):
    nb, ns, _, chunk = qat.shape
    seq = ka.shape[1]
    nq = TQ_A // chunk
    wide = A_REP * chunk
    return pl.pallas_call(
        _attn_a_kernel,
        grid=(nb, seq // TQ_A),
        in_specs=[
            pl.BlockSpec((None, nq, A_Q_W, chunk), lambda b, s: (b, s, 0, 0)),
            pl.BlockSpec((None, seq, A_KV_W), lambda b, s: (b, 0, 0)),
            pl.BlockSpec((None, ns, A_KV_W, chunk), lambda b, s: (b, 0, 0, 0)),
        ],
        out_specs=pl.BlockSpec((None, TQ_A, A_Q_W), lambda b, s: (b, s, 0)),
        out_shape=jax.ShapeDtypeStruct((nb, seq, A_Q_W), BF16),
        scratch_shapes=[pltpu.VMEM((nq, A_KV_HEADS, A_KV_W, wide), BF16),
                        pltpu.VMEM((nq, A_KV_HEADS, HEAD_DIM + ONES_ROWS, wide), F32),
                        pltpu.VMEM((nq, A_KV_HEADS, 1, wide), F32),
                        pltpu.VMEM((2, A_KV_HEADS, chunk, wide), F32),
                        pltpu.VMEM((2, A_KV_HEADS, 1, wide), F32)],
        compiler_params=_cparams(("arbitrary", "arbitrary")),
        name="attn_a",
    )(qat, ka, vat)


def _attn_b_kernel(q_ref, k_ref, v_ref, o_ref, lse_ref, *, half_window):
    tj = q_ref.shape[0]
    length = k_ref.shape[0]
    j_tile = pl.program_id(2) * tj
    lane = lax.broadcasted_iota(jnp.int32, (1, V7X_LANES), 1)
    first_head = lane < HEAD_DIM
    nt = (((1,), (1,)), ((), ()))
    ones = jnp.ones((KW_B, V7X_LANES), BF16)
    for qi in range(tj // TQ_B):
        j0 = j_tile + qi * TQ_B
        ks = jnp.clip(j0 - half_window, 0, length - KW_B)
        ks = pl.multiple_of(ks, half_window)
        jq = j0 + lax.broadcasted_iota(jnp.int32, (TQ_B, 1), 0)
        jk = ks + lax.broadcasted_iota(jnp.int32, (1, KW_B), 1)
        bias = jnp.where(jnp.abs(jq - jk) <= half_window, 0.0, NEG_BIG)
        bias2 = jnp.concatenate([bias, bias], axis=0)
        for pair in range(q_ref.shape[1] // V7X_LANES):
            cols = slice(pair * V7X_LANES, (pair + 1) * V7X_LANES)
            q2 = q_ref[qi * TQ_B:(qi + 1) * TQ_B, cols]
            k2 = k_ref[pl.ds(ks, KW_B), cols]
            v_aug = jnp.concatenate([v_ref[pl.ds(ks, KW_B), cols], ones], axis=1)
            zero = jnp.zeros_like(q2)
            qs = jnp.concatenate([jnp.where(first_head, q2, zero), jnp.where(first_head, zero, q2)], axis=0)
            s = lax.dot_general(qs, k2, nt, preferred_element_type=F32) + bias2
            m = jnp.max(s, axis=1, keepdims=True)
            p = jnp.exp2(s - m).astype(BF16)
            ol = jnp.dot(p, v_aug, preferred_element_type=F32)
            o_pair = jnp.where(first_head, ol[:TQ_B, :V7X_LANES], ol[TQ_B:, :V7X_LANES])
            l_pair = jnp.where(first_head, ol[:TQ_B, V7X_LANES:], ol[TQ_B:, V7X_LANES:])
            m_pair = jnp.where(first_head, m[:TQ_B], m[TQ_B:])
            o_ref[qi * TQ_B:(qi + 1) * TQ_B, cols] = (o_pair / l_pair).astype(BF16)
            lse_ref[qi * TQ_B:(qi + 1) * TQ_B, cols] = m_pair + jnp.log2(l_pair)


def _attn_b(q, k, v, window, dil):
    nb, length, _ = q.shape
    half_window = (window // 2) // dil
    assert half_window * 2 + TQ_B == KW_B and length >= KW_B
    tj = min(B_ROWS_PER_STEP, length)
    ncls = min(B_ROWS_PER_STEP // tj, dil)
    qspec = pl.BlockSpec((None, tj, ncls * B_W), lambda b, r, j: (b, j, r))
    kvspec = pl.BlockSpec((None, length, ncls * B_W), lambda b, r, j: (b, 0, r))
    return pl.pallas_call(
        functools.partial(_attn_b_kernel, half_window=half_window),
        grid=(nb, dil // ncls, length // tj),
        in_specs=[qspec, kvspec, kvspec],
        out_specs=[qspec, qspec],
        out_shape=[jax.ShapeDtypeStruct((nb, length, dil * B_W), BF16),
                   jax.ShapeDtypeStruct((nb, length, dil * B_W), F32)],
        compiler_params=_cparams(("arbitrary", "arbitrary", "arbitrary")),
        name=f"attn_b_d{dil}",
    )(q, k, v)


def _load_class_major(blk_ref, stage_ref, dil, sub):
    rows = CHUNK // dil
    blk_rows = slice(sub * rows, (sub + 1) * rows)
    if dil == 1:
        return blk_ref[blk_rows, :].astype(F32)
    halves = range(B_W // V7X_LANES)
    for r in range(dil):
        for half in halves:
            col = r * B_W + half * V7X_LANES
            stage_ref[sub, half, pl.ds(r, rows, stride=dil), :] = (
                blk_ref[blk_rows, col:col + V7X_LANES].astype(F32))
    return jnp.concatenate([stage_ref[sub, half] for half in halves], axis=1)


def _mix_mlp_kernel(x_ref, ada_ref, oa_ref, ob0_ref, ob1_ref, ob2_ref, ls0_ref, ls1_ref, ls2_ref,
                    ga_ref, gb_ref, wa_ref, wb_ref, wo_ref, g_ref, w1_ref, w2_ref, xo_ref, *stage_refs):
    dils = [d for _, d in B_GROUPS]
    for sub in range(TM_MIX // CHUNK):
        tok = slice(sub * CHUNK, (sub + 1) * CHUNK)
        ya = jnp.dot(oa_ref[tok, :], wa_ref[...], preferred_element_type=F32)
        lses = [_load_class_major(r, stage_refs[2 * g], dils[g], sub)
                for g, r in enumerate((ls0_ref, ls1_ref, ls2_ref))]
        outs = [_load_class_major(r, stage_refs[2 * g + 1], dils[g], sub)
                for g, r in enumerate((ob0_ref, ob1_ref, ob2_ref))]
        top = jnp.maximum(jnp.maximum(lses[0], lses[1]), lses[2])
        es = [jnp.exp2(v - top) for v in lses]
        den = es[0] + es[1] + es[2]
        ob = (es[0] * outs[0] + es[1] * outs[1] + es[2] * outs[2]) / den
        yb = jnp.dot(ob.astype(BF16), wb_ref[...], preferred_element_type=F32)
        mixed = ga_ref[tok, :].astype(F32) * ya + gb_ref[tok, :].astype(F32) * yb
        upd = jnp.dot(mixed.astype(BF16), wo_ref[...], preferred_element_type=F32)
        xo_ref[tok, :] = x_ref[tok, :] + ada_ref[2:3, :] * upd

    x = xo_ref[...]
    h = _modulated_rms(x, g_ref[...], ada_ref[4:5, :], ada_ref[3:4, :]).astype(BF16)
    acc = jnp.zeros(x.shape, F32)
    for c in range(D_FF // FF_CHUNK):
        a = jnp.dot(h, w1_ref[:, c * FF_CHUNK:(c + 1) * FF_CHUNK], preferred_element_type=F32)
        a = jnp.square(jnp.maximum(a, 0.0)).astype(BF16)
        acc = acc + jnp.dot(a, w2_ref[c * FF_CHUNK:(c + 1) * FF_CHUNK, :], preferred_element_type=F32)
    xo_ref[...] = x + ada_ref[5:6, :] * acc


def _mix_mlp(x, ada_l, oa, obs, lss, ga, gb, wa, wb, wo, g_mlp_l, w1, w2, l):
    nb, seq, _ = x.shape
    tok = lambda w: pl.BlockSpec((None, TM_MIX, w), lambda b, s: (b, s, 0))
    wspec = lambda r, c: pl.BlockSpec((None, r, c), lambda b, s: (l, 0, 0), pipeline_mode=pl.Buffered(1))
    cls = [pl.BlockSpec((None, TM_MIX // d, d * B_W), lambda b, s: (b, s, 0)) for _, d in B_GROUPS]
    return pl.pallas_call(
        _mix_mlp_kernel,
        grid=(nb, seq // TM_MIX),
        in_specs=[tok(D_MODEL), pl.BlockSpec((None, 6, D_MODEL), lambda b, s: (b, 0, 0)), tok(A_Q_W)]
                 + cls + cls + [tok(D_MODEL)] * 2
                 + [wspec(A_Q_W, D_MODEL), wspec(B_W, D_MODEL), wspec(D_MODEL, D_MODEL),
                    pl.BlockSpec((None, 1, D_MODEL), lambda b, s: (l, 0, 0)),
                    wspec(D_MODEL, D_FF), wspec(D_FF, D_MODEL)],
        out_specs=tok(D_MODEL),
        out_shape=jax.ShapeDtypeStruct(x.shape, F32),
        scratch_shapes=[pltpu.VMEM((TM_MIX // CHUNK, B_W // V7X_LANES, CHUNK, V7X_LANES), F32)] * (2 * N_B),
        input_output_aliases={0: 0} if l > 0 else {},
        compiler_params=_cparams(("arbitrary", "arbitrary")),
        name="mix_mlp",
    )(x, ada_l, oa, *obs, *lss, ga, gb, wa, wb, wo, g_mlp_l, w1, w2)


def _rope_tables(seq):
    pos = jnp.arange(seq)
    rows = seq // GRID_W
    row_idx = jnp.broadcast_to(jnp.arange(rows)[:, None], (rows, GRID_W)).reshape(-1)
    col_idx = jnp.broadcast_to(jnp.arange(GRID_W)[None, :], (rows, GRID_W)).reshape(-1)

    def cos_sin(p, dim):
        inv = ROPE_THETA ** (-jnp.arange(0, dim, 2, dtype=F32) / dim)
        ang = p.astype(F32)[:, None] * inv[None, :]
        return jnp.cos(ang), jnp.sin(ang)

    rc, rs = cos_sin(row_idx, HEAD_DIM // 2)
    cc, cs = cos_sin(col_idx, HEAD_DIM // 2)
    sc, ss = cos_sin(pos, HEAD_DIM)
    cos_ax = jnp.concatenate([rc, rc, cc, cc], axis=-1)
    sin_ax = jnp.concatenate([-rs, rs, -cs, cs], axis=-1)
    cos_sq = jnp.concatenate([sc, sc], axis=-1)
    sin_sq = jnp.concatenate([-ss, ss], axis=-1)
    return (cos_ax, sin_ax), (cos_sq, sin_sq)


def _gain_tables(q_norm_a, k_norm_a, q_norm_b, k_norm_b, seq):
    (cos_ax, sin_ax), (cos_sq, sin_sq) = _rope_tables(seq)
    scale = HEAD_DIM ** -0.5 * LOG2_E

    def pair(gain, cos, sin, perm, mult):
        c = gain[:, None, :] * cos[None] * mult
        s = gain[:, perm][:, None, :] * sin[None] * mult
        return [jnp.swapaxes(c, 1, 2), jnp.swapaxes(s, 1, 2)]

    tabs = pair(q_norm_a, cos_ax, sin_ax, PERM_AXIAL, scale) + pair(k_norm_a, cos_ax, sin_ax, PERM_AXIAL, 1.0)
    for g in range(N_B):
        tabs += pair(q_norm_b[:, g], cos_sq, sin_sq, PERM_SEQ, scale)
        tabs += pair(k_norm_b[:, g], cos_sq, sin_sq, PERM_SEQ, 1.0)
    return jnp.concatenate(tabs, axis=1)


def _split_w_in(w_in):
    sizes = [A_Q_W, A_KV_W, A_KV_W] + [B_W] * (3 * N_B) + [D_MODEL, D_MODEL]
    offs = np.concatenate([[0], np.cumsum(sizes)])
    col = lambda i: w_in[:, :, offs[i]:offs[i + 1]]
    names = ["qa", "ka", "va"] + [f"{t}b{g}" for g in range(N_B) for t in "qkv"] + ["ga", "gb"]
    parts = {n: col(i) for i, n in enumerate(names)}
    wn = jnp.concatenate([parts["vb0"], parts["vb1"], parts["vb2"], parts["ga"], parts["gb"]], axis=-1)
    wt = jnp.concatenate([parts[n] for n, _ in T_SEGS], axis=-1)
    return wn.astype(BF16), jnp.swapaxes(wt, 1, 2).astype(BF16)


def kernel(x, c, w_ada, b_ada, g_mix, g_mlp, w_in, q_norm_a, k_norm_a, q_norm_b, k_norm_b,
           w_branch_a, w_branch_b, w_out, w_ff1, w_ff2):
    nb, seq, _ = x.shape
    depth = w_in.shape[0]
    assert seq % GRID_W == 0 and seq % TM_MIX == 0 and seq % TM_IN == 0

    ada = _ada_all(c, w_ada, b_ada).reshape(depth, nb, 6, D_MODEL)
    tabs = _gain_tables(q_norm_a, k_norm_a, q_norm_b, k_norm_b, seq)
    wn, wt = _split_w_in(w_in)
    wa, wb, wo = w_branch_a.astype(BF16), w_branch_b.astype(BF16), w_out.astype(BF16)
    w1, w2 = w_ff1.astype(BF16), w_ff2.astype(BF16)
    g_mix = g_mix.reshape(depth, 1, D_MODEL)
    g_mlp = g_mlp.reshape(depth, 1, D_MODEL)

    for l in range(depth):
        (qat, ka, vat, qb0, kb0, vb0, qb1, kb1, vb1, qb2, kb2, vb2, ga, gb) = _in_proj(
            x, ada[l], g_mix, wn, wt, tabs, l)
        oa = _attn_a(qat, ka, vat)
        obs, lss = [], []
        for (window, dil), q, k, v in zip(B_GROUPS, (qb0, qb1, qb2), (kb0, kb1, kb2), (vb0, vb1, vb2)):
            o, lse = _attn_b(q, k, v, window, dil)
            obs.append(o)
            lss.append(lse)
        x = _mix_mlp(x, ada[l], oa, obs, lss, ga, gb, wa, wb, wo, g_mlp, w1, w2, l)
    return x
```

```python
import functools

import jax
import jax.numpy as jnp
import numpy as np
from jax import lax
from jax.experimental import pallas as pl
from jax.experimental.pallas import tpu as pltpu

D_MODEL = 1024
HEAD_DIM = 64
A_Q_HEADS = 8
A_KV_HEADS = 2
B_GROUPS = ((128, 1), (512, 4), (2048, 16))
B_HEADS = 4
N_B = len(B_GROUPS)
D_FF = 4 * D_MODEL
GRID_W = 64
ROPE_THETA = 10000.0
EPS = 1e-6

A_Q_W = A_Q_HEADS * HEAD_DIM
A_KV_W = A_KV_HEADS * HEAD_DIM
B_W = B_HEADS * HEAD_DIM

V7X_LANES = 128
V7X_VMEM_BYTES = 64 * 1024 * 1024
VMEM_LIMIT = 52 * 1024 * 1024

BF16 = jnp.bfloat16
F32 = jnp.float32

CHUNK = 256
TM_IN = 512
TQ_A = 1024
A_PAIRS_PER_BODY = 3
TM_MIX = 512
FF_CHUNK = 1024
B_ROWS_PER_STEP = 1024
TQ_B = 128
KW_B = 256
NEG_BIG = -1e30
LOG2_E = 1.4426950408889634

PERM_AXIAL = np.concatenate([np.arange(16, 32), np.arange(0, 16), np.arange(48, 64), np.arange(32, 48)])
PERM_SEQ = np.concatenate([np.arange(32, 64), np.arange(0, 32)])

T_SEGS = (("qa", A_Q_W), ("ka", A_KV_W), ("va", A_KV_W),
          ("qb0", B_W), ("kb0", B_W), ("qb1", B_W), ("kb1", B_W), ("qb2", B_W), ("kb2", B_W))
T_ROWS = sum(r for _, r in T_SEGS)
N_COLS = N_B * B_W + 2 * D_MODEL
N_TABS = 16


def _cparams(sem):
    return pltpu.CompilerParams(dimension_semantics=sem, vmem_limit_bytes=VMEM_LIMIT)


def _ada_kernel(c_ref, w_ref, b_ref, o_ref):
    c = c_ref[...]
    c_act = c * (1.0 / (1.0 + jnp.exp(-c)))
    o_ref[...] = jnp.dot(c_act, w_ref[...], preferred_element_type=F32,
                         precision=lax.Precision.HIGHEST) + b_ref[...]


def _ada_all(c, w_ada, b_ada):
    depth = w_ada.shape[0]
    nb = c.shape[0]
    return pl.pallas_call(
        _ada_kernel,
        grid=(depth, 6),
        in_specs=[
            pl.BlockSpec((nb, D_MODEL), lambda l, j: (0, 0)),
            pl.BlockSpec((None, D_MODEL, D_MODEL), lambda l, j: (l, 0, j)),
            pl.BlockSpec((None, 1, D_MODEL), lambda l, j: (l, 0, j)),
        ],
        out_specs=pl.BlockSpec((None, nb, D_MODEL), lambda l, j: (l, 0, j)),
        out_shape=jax.ShapeDtypeStruct((depth, nb, 6 * D_MODEL), F32),
        compiler_params=_cparams(("arbitrary", "arbitrary")),
        name="ada",
    )(c, w_ada, b_ada.reshape(depth, 1, 6 * D_MODEL))


def _modulated_rms(x, g, scale, shift):
    ms = jnp.mean(x * x, axis=-1, keepdims=True)
    return (x * lax.rsqrt(ms + EPS)) * g * (1.0 + scale) + shift


def _norm_rope_t(q, cos_t, sin_t, perm_blocks):
    ss = jnp.sum(q * q, axis=0, keepdims=True)
    r = lax.rsqrt(ss * (1.0 / HEAD_DIM) + EPS)
    qp = jnp.concatenate([q[a:b] for a, b in perm_blocks], axis=0)
    return (q * cos_t + qp * sin_t) * r


AXIAL_BLOCKS = ((16, 32), (0, 16), (48, 64), (32, 48))
SEQ_BLOCKS = ((32, 64), (0, 32))


def _store_class_major(out_ref, stage_ref, tile, dil, sub):
    rows = tile.shape[0] // dil
    out_rows = slice(sub * rows, (sub + 1) * rows)
    if dil == 1:
        out_ref[out_rows, :] = tile.astype(BF16)
        return
    for half in range(B_W // V7X_LANES):
        stage_ref[sub, half] = tile[:, half * V7X_LANES:(half + 1) * V7X_LANES]
    for r in range(dil):
        for half in range(B_W // V7X_LANES):
            col = r * B_W + half * V7X_LANES
            out_ref[out_rows, col:col + V7X_LANES] = (
                stage_ref[sub, half, pl.ds(r, rows, stride=dil), :].astype(BF16))


def _in_kernel(x_ref, ada_ref, g_ref, wn_ref, wt_ref, rope_ref, gain_ref,
               qat_ref, ka_ref, vat_ref,
               qb0_ref, kb0_ref, vb0_ref, qb1_ref, kb1_ref, vb1_ref, qb2_ref, kb2_ref, vb2_ref,
               ga_ref, gb_ref, stage_q_ref, stage_k_ref, stage_v_ref):
    for sub in range(TM_IN // CHUNK):
        _in_subtile(sub, x_ref, ada_ref, g_ref, wn_ref, wt_ref, rope_ref, gain_ref, qat_ref, ka_ref, vat_ref,
                    (qb0_ref, qb1_ref, qb2_ref), (kb0_ref, kb1_ref, kb2_ref), (vb0_ref, vb1_ref, vb2_ref),
                    ga_ref, gb_ref, stage_q_ref, stage_k_ref, stage_v_ref)


def _in_subtile(sub, x_ref, ada_ref, g_ref, wn_ref, wt_ref, rope_ref, gain_ref, qat_ref, ka_ref, vat_ref,
                qb_refs, kb_refs, vb_refs, ga_ref, gb_ref, stage_q_ref, stage_k_ref, stage_v_ref):
    tok = slice(sub * CHUNK, (sub + 1) * CHUNK)
    x = x_ref[tok, :]
    h = _modulated_rms(x, g_ref[...], ada_ref[1:2, :], ada_ref[0:1, :]).astype(BF16)

    nt = (((1,), (1,)), ((), ()))

    def proj_t(row0, rows):
        return lax.dot_general(wt_ref[row0:row0 + rows, :], h, nt, preferred_element_type=F32)

    def gained(tab, base_row):
        gain = gain_ref[tab * HEAD_DIM:(tab + 1) * HEAD_DIM, :]
        gain = jnp.concatenate([gain] * (CHUNK // V7X_LANES), axis=1)
        return rope_ref[base_row * HEAD_DIM:(base_row + 1) * HEAD_DIM, tok] * gain

    def heads_t(rt, n_heads, tab0, blocks):
        base_row = 0 if blocks is AXIAL_BLOCKS else 2
        cos_t = gained(tab0, base_row)
        sin_t = gained(tab0 + 1, base_row + 1)
        return [_norm_rope_t(rt[i * HEAD_DIM:(i + 1) * HEAD_DIM], cos_t, sin_t, blocks)
                for i in range(n_heads)]

    row = 0
    rt = proj_t(row, A_Q_W)
    qa = heads_t(rt, A_Q_HEADS, 0, AXIAL_BLOCKS)
    for i in range(A_Q_HEADS):
        qat_ref[sub, i * HEAD_DIM:(i + 1) * HEAD_DIM, :] = qa[i].astype(BF16)
    row += A_Q_W

    rt = proj_t(row, 2 * A_KV_W)
    ka = heads_t(rt[:A_KV_W], A_KV_HEADS, 2, AXIAL_BLOCKS)
    ka_ref[tok, :] = jnp.concatenate(ka, axis=0).T.astype(BF16)
    vat_ref[sub] = rt[A_KV_W:].astype(BF16)
    row += 2 * A_KV_W

    for g in range(N_B):
        rt = proj_t(row, 2 * B_W)
        qb = heads_t(rt[:B_W], B_HEADS, 4 + 4 * g, SEQ_BLOCKS)
        kb = heads_t(rt[B_W:], B_HEADS, 6 + 4 * g, SEQ_BLOCKS)
        _store_class_major(qb_refs[g], stage_q_ref, jnp.concatenate(qb, axis=0).T, B_GROUPS[g][1], sub)
        _store_class_major(kb_refs[g], stage_k_ref, jnp.concatenate(kb, axis=0).T, B_GROUPS[g][1], sub)
        row += 2 * B_W

    for i, gate_ref in enumerate((ga_ref, gb_ref)):
        lo = N_B * B_W + i * D_MODEL
        z = jnp.dot(h, wn_ref[:, lo:lo + D_MODEL], preferred_element_type=F32)
        gate_ref[tok, :] = (1.0 / (1.0 + jnp.exp(-z))).astype(BF16)
    for g in reversed(range(N_B)):
        v = jnp.dot(h, wn_ref[:, g * B_W:(g + 1) * B_W], preferred_element_type=F32)
        _store_class_major(vb_refs[g], stage_v_ref, v, B_GROUPS[g][1], sub)


def _in_proj(x, ada_l, g_mix_l, wn, wt, rope, gains, l):
    nb, seq, _ = x.shape
    ns = seq // TM_IN
    nsub = TM_IN // CHUNK
    tok = lambda w: pl.BlockSpec((None, TM_IN, w), lambda s, b: (b, s, 0))
    tchunk = lambda r: pl.BlockSpec((None, nsub, r, CHUNK), lambda s, b: (b, s, 0, 0))
    nat = lambda w: jax.ShapeDtypeStruct((nb, seq, w), BF16)
    out_specs = [tchunk(A_Q_W), tok(A_KV_W), tchunk(A_KV_W)]
    out_shape = [jax.ShapeDtypeStruct((nb, seq // CHUNK, A_Q_W, CHUNK), BF16), nat(A_KV_W),
                 jax.ShapeDtypeStruct((nb, seq // CHUNK, A_KV_W, CHUNK), BF16)]
    for _, dil in B_GROUPS:
        out_specs += [pl.BlockSpec((None, TM_IN // dil, dil * B_W), lambda s, b: (b, s, 0))] * 3
        out_shape += [jax.ShapeDtypeStruct((nb, seq // dil, dil * B_W), BF16)] * 3
    out_specs += [tok(D_MODEL)] * 2
    out_shape += [nat(D_MODEL)] * 2
    return pl.pallas_call(
        _in_kernel,
        grid=(ns, nb),
        in_specs=[
            tok(D_MODEL),
            pl.BlockSpec((None, 6, D_MODEL), lambda s, b: (b, 0, 0)),
            pl.BlockSpec((None, 1, D_MODEL), lambda s, b: (l, 0, 0)),
            pl.BlockSpec((None, D_MODEL, N_COLS), lambda s, b: (l, 0, 0), pipeline_mode=pl.Buffered(1)),
            pl.BlockSpec((None, T_ROWS, D_MODEL), lambda s, b: (l, 0, 0), pipeline_mode=pl.Buffered(1)),
            pl.BlockSpec((4 * HEAD_DIM, TM_IN), lambda s, b: (0, s)),
            pl.BlockSpec((None, N_TABS * HEAD_DIM, V7X_LANES), lambda s, b: (l, 0, 0)),
        ],
        out_specs=out_specs,
        out_shape=out_shape,
        scratch_shapes=[pltpu.VMEM((nsub, B_W // V7X_LANES, CHUNK, V7X_LANES), F32)] * 3,
        compiler_params=_cparams(("arbitrary", "arbitrary")),
        name="in_proj",
    )(x, ada_l, g_mix_l, wn, wt, rope, gains)


A_REP = A_Q_HEADS // A_KV_HEADS
ONES_ROWS = 16


def _attn_a_kernel(qt_ref, k_ref, vt_ref, o_ref, qpad_ref, acc_ref, m_ref, s_ref, smax_ref):
    for sub in range(qt_ref.shape[0]):
        _attn_a_tile(qt_ref.at[sub], k_ref, vt_ref, o_ref.at[sub * CHUNK:(sub + 1) * CHUNK],
                     qpad_ref.at[sub], acc_ref.at[sub], m_ref.at[sub], s_ref, smax_ref)


def _attn_a_tile(qt_ref, k_ref, vt_ref, o_ref, qpad_ref, acc_ref, m_ref, s_ref, smax_ref):
    n_chunks, _, tk = vt_ref.shape
    tq = qt_ref.shape[1]
    qpad_ref[...] = jnp.zeros(qpad_ref.shape, BF16)
    for g in range(A_KV_HEADS):
        for hh in range(A_REP):
            h = g * A_REP + hh
            qpad_ref[g, g * HEAD_DIM:(g + 1) * HEAD_DIM, hh * tq:(hh + 1) * tq] = (
                qt_ref[h * HEAD_DIM:(h + 1) * HEAD_DIM, :])
    m_ref[...] = jnp.full(m_ref.shape, NEG_BIG, F32)
    acc_ref[...] = jnp.zeros(acc_ref.shape, F32)
    ones = jnp.ones((ONES_ROWS, tk), BF16)

    def scores(c, slot, g):
        kc = k_ref[pl.ds(pl.multiple_of(c * tk, tk), tk), :]
        s = jnp.dot(kc, qpad_ref[g], preferred_element_type=F32)
        s_ref[slot, g] = s
        smax_ref[slot, g] = jnp.max(s, axis=0, keepdims=True)

    def consume(c, slot, g):
        m_prev = m_ref[g]
        m_new = jnp.maximum(m_prev, smax_ref[slot, g])
        alpha = jnp.exp2(m_prev - m_new)
        p = jnp.exp2(s_ref[slot, g] - m_new).astype(BF16)
        v_aug = jnp.concatenate([vt_ref[c, g * HEAD_DIM:(g + 1) * HEAD_DIM, :], ones], axis=0)
        acc_ref[g] = acc_ref[g] * alpha + jnp.dot(v_aug, p, preferred_element_type=F32)
        m_ref[g] = m_new

    for g in range(A_KV_HEADS):
        scores(0, 0, g)

    def chunk_pair(c, last):
        for slot in range(2):
            for g in range(A_KV_HEADS):
                if not (last and slot == 1):
                    scores(c + slot + 1, 1 - slot, g)
                consume(c + slot, slot, g)

    body_chunks = 2 * A_PAIRS_PER_BODY

    def loop_body(i, carry):
        for j in range(A_PAIRS_PER_BODY):
            chunk_pair(body_chunks * i + 2 * j, False)
        return carry

    n_loop = (n_chunks - 2) // body_chunks
    lax.fori_loop(0, n_loop, loop_body, 0)
    for c in range(n_loop * body_chunks, n_chunks, 2):
        chunk_pair(c, c == n_chunks - 2)
    for j in range(A_Q_W // V7X_LANES):
        halves = []
        for h in (2 * j, 2 * j + 1):
            g, hh = divmod(h, A_REP)
            cols = slice(hh * tq, (hh + 1) * tq)
            halves.append(acc_ref[g, :HEAD_DIM, cols] / acc_ref[g, HEAD_DIM:HEAD_DIM + 1, cols])
        blk = jnp.concatenate(halves, axis=0)
        o_ref[:, j * V7X_LANES:(j + 1) * V7X_LANES] = blk.T.astype(BF16)


def _attn_a(qat, ka, vat):
    nb, ns, _, chunk = qat.shape
    seq = ka.shape[1]
    nq = TQ_A // chunk
    wide = A_REP * chunk
    return pl.pallas_call(
        _attn_a_kernel,
        grid=(nb, seq // TQ_A),
        in_specs=[
            pl.BlockSpec((None, nq, A_Q_W, chunk), lambda b, s: (b, s, 0, 0)),
            pl.BlockSpec((None, seq, A_KV_W), lambda b, s: (b, 0, 0)),
            pl.BlockSpec((None, ns, A_KV_W, chunk), lambda b, s: (b, 0, 0, 0)),
        ],
        out_specs=pl.BlockSpec((None, TQ_A, A_Q_W), lambda b, s: (b, s, 0)),
        out_shape=jax.ShapeDtypeStruct((nb, seq, A_Q_W), BF16),
        scratch_shapes=[pltpu.VMEM((nq, A_KV_HEADS, A_KV_W, wide), BF16),
                        pltpu.VMEM((nq, A_KV_HEADS, HEAD_DIM + ONES_ROWS, wide), F32),
                        pltpu.VMEM((nq, A_KV_HEADS, 1, wide), F32),
                        pltpu.VMEM((2, A_KV_HEADS, chunk, wide), F32),
                        pltpu.VMEM((2, A_KV_HEADS, 1, wide), F32)],
        compiler_params=_cparams(("arbitrary", "arbitrary")),
        name="attn_a",
    )(qat, ka, vat)


def _attn_b_kernel(q_ref, k_ref, v_ref, o_ref, lse_ref, *, half_window):
    tj = q_ref.shape[0]
    length = k_ref.shape[0]
    j_tile = pl.program_id(2) * tj
    lane = lax.broadcasted_iota(jnp.int32, (1, V7X_LANES), 1)
    first_head = lane < HEAD_DIM
    nt = (((1,), (1,)), ((), ()))
    ones = jnp.ones((KW_B, V7X_LANES), BF16)
    for qi in range(tj // TQ_B):
        j0 = j_tile + qi * TQ_B
        ks = jnp.clip(j0 - half_window, 0, length - KW_B)
        ks = pl.multiple_of(ks, half_window)
        jq = j0 + lax.broadcasted_iota(jnp.int32, (TQ_B, 1), 0)
        jk = ks + lax.broadcasted_iota(jnp.int32, (1, KW_B), 1)
        bias = jnp.where(jnp.abs(jq - jk) <= half_window, 0.0, NEG_BIG)
        bias2 = jnp.concatenate([bias, bias], axis=0)
        for pair in range(q_ref.shape[1] // V7X_LANES):
            cols = slice(pair * V7X_LANES, (pair + 1) * V7X_LANES)
            q2 = q_ref[qi * TQ_B:(qi + 1) * TQ_B, cols]
            k2 = k_ref[pl.ds(ks, KW_B), cols]
            v_aug = jnp.concatenate([v_ref[pl.ds(ks, KW_B), cols], ones], axis=1)
            zero = jnp.zeros_like(q2)
            qs = jnp.concatenate([jnp.where(first_head, q2, zero), jnp.where(first_head, zero, q2)], axis=0)
            s = lax.dot_general(qs, k2, nt, preferred_element_type=F32) + bias2
            m = jnp.max(s, axis=1, keepdims=True)
            p = jnp.exp2(s - m).astype(BF16)
            ol = jnp.dot(p, v_aug, preferred_element_type=F32)
            o_pair = jnp.where(first_head, ol[:TQ_B, :V7X_LANES], ol[TQ_B:, :V7X_LANES])
            l_pair = jnp.where(first_head, ol[:TQ_B, V7X_LANES:], ol[TQ_B:, V7X_LANES:])
            m_pair = jnp.where(first_head, m[:TQ_B], m[TQ_B:])
            o_ref[qi * TQ_B:(qi + 1) * TQ_B, cols] = (o_pair / l_pair).astype(BF16)
            lse_ref[qi * TQ_B:(qi + 1) * TQ_B, cols] = m_pair + jnp.log2(l_pair)


def _attn_b(q, k, v, window, dil):
    nb, length, _ = q.shape
    half_window = (window // 2) // dil
    assert half_window * 2 + TQ_B == KW_B and length >= KW_B
    tj = min(B_ROWS_PER_STEP, length)
    ncls = min(B_ROWS_PER_STEP // tj, dil)
    qspec = pl.BlockSpec((None, tj, ncls * B_W), lambda b, r, j: (b, j, r))
    kvspec = pl.BlockSpec((None, length, ncls * B_W), lambda b, r, j: (b, 0, r))
    return pl.pallas_call(
        functools.partial(_attn_b_kernel, half_window=half_window),
        grid=(nb, dil // ncls, length // tj),
        in_specs=[qspec, kvspec, kvspec],
        out_specs=[qspec, qspec],
        out_shape=[jax.ShapeDtypeStruct((nb, length, dil * B_W), BF16),
                   jax.ShapeDtypeStruct((nb, length, dil * B_W), F32)],
        compiler_params=_cparams(("arbitrary", "arbitrary", "arbitrary")),
        name=f"attn_b_d{dil}",
    )(q, k, v)


def _load_class_major(blk_ref, stage_ref, dil, sub):
    rows = CHUNK // dil
    blk_rows = slice(sub * rows, (sub + 1) * rows)
    if dil == 1:
        return blk_ref[blk_rows, :].astype(F32)
    halves = range(B_W // V7X_LANES)
    for r in range(dil):
        for half in halves:
            col = r * B_W + half * V7X_LANES
            stage_ref[sub, half, pl.ds(r, rows, stride=dil), :] = (
                blk_ref[blk_rows, col:col + V7X_LANES].astype(F32))
    return jnp.concatenate([stage_ref[sub, half] for half in halves], axis=1)


def _mix_mlp_kernel(x_ref, ada_ref, oa_ref, ob0_ref, ob1_ref, ob2_ref, ls0_ref, ls1_ref, ls2_ref,
                    ga_ref, gb_ref, wa_ref, wb_ref, wo_ref, g_ref, w1_ref, w2_ref, xo_ref, *stage_refs):
    dils = [d for _, d in B_GROUPS]
    for sub in range(TM_MIX // CHUNK):
        tok = slice(sub * CHUNK, (sub + 1) * CHUNK)
        ya = jnp.dot(oa_ref[tok, :], wa_ref[...], preferred_element_type=F32)
        lses = [_load_class_major(r, stage_refs[2 * g], dils[g], sub)
                for g, r in enumerate((ls0_ref, ls1_ref, ls2_ref))]
        outs = [_load_class_major(r, stage_refs[2 * g + 1], dils[g], sub)
                for g, r in enumerate((ob0_ref, ob1_ref, ob2_ref))]
        top = jnp.maximum(jnp.maximum(lses[0], lses[1]), lses[2])
        es = [jnp.exp2(v - top) for v in lses]
        den = es[0] + es[1] + es[2]
        ob = (es[0] * outs[0] + es[1] * outs[1] + es[2] * outs[2]) / den
        yb = jnp.dot(ob.astype(BF16), wb_ref[...], preferred_element_type=F32)
        mixed = ga_ref[tok, :].astype(F32) * ya + gb_ref[tok, :].astype(F32) * yb
        upd = jnp.dot(mixed.astype(BF16), wo_ref[...], preferred_element_type=F32)
        xo_ref[tok, :] = x_ref[tok, :] + ada_ref[2:3, :] * upd

    x = xo_ref[...]
    h = _modulated_rms(x, g_ref[...], ada_ref[4:5, :], ada_ref[3:4, :]).astype(BF16)
    acc = jnp.zeros(x.shape, F32)
    for c in range(D_FF // FF_CHUNK):
        a = jnp.dot(h, w1_ref[:, c * FF_CHUNK:(c + 1) * FF_CHUNK], preferred_element_type=F32)
        a = jnp.square(jnp.maximum(a, 0.0)).astype(BF16)
        acc = acc + jnp.dot(a, w2_ref[c * FF_CHUNK:(c + 1) * FF_CHUNK, :], preferred_element_type=F32)
    xo_ref[...] = x + ada_ref[5:6, :] * acc


def _mix_mlp(x, ada_l, oa, obs, lss, ga, gb, wa, wb, wo, g_mlp_l, w1, w2, l):
    nb, seq, _ = x.shape
    tok = lambda w: pl.BlockSpec((None, TM_MIX, w), lambda b, s: (b, s, 0))
    wspec = lambda r, c: pl.BlockSpec((None, r, c), lambda b, s: (l, 0, 0), pipeline_mode=pl.Buffered(1))
    cls = [pl.BlockSpec((None, TM_MIX // d, d * B_W), lambda b, s: (b, s, 0)) for _, d in B_GROUPS]
    return pl.pallas_call(
        _mix_mlp_kernel,
        grid=(nb, seq // TM_MIX),
        in_specs=[tok(D_MODEL), pl.BlockSpec((None, 6, D_MODEL), lambda b, s: (b, 0, 0)), tok(A_Q_W)]
                 + cls + cls + [tok(D_MODEL)] * 2
                 + [wspec(A_Q_W, D_MODEL), wspec(B_W, D_MODEL), wspec(D_MODEL, D_MODEL),
                    pl.BlockSpec((None, 1, D_MODEL), lambda b, s: (l, 0, 0)),
                    wspec(D_MODEL, D_FF), wspec(D_FF, D_MODEL)],
        out_specs=tok(D_MODEL),
        out_shape=jax.ShapeDtypeStruct(x.shape, F32),
        scratch_shapes=[pltpu.VMEM((TM_MIX // CHUNK, B_W // V7X_LANES, CHUNK, V7X_LANES), F32)] * (2 * N_B),
        input_output_aliases={0: 0} if l > 0 else {},
        compiler_params=_cparams(("arbitrary", "arbitrary")),
        name="mix_mlp",
    )(x, ada_l, oa, *obs, *lss, ga, gb, wa, wb, wo, g_mlp_l, w1, w2)


def _rope_tables(seq):
    pos = jnp.arange(seq)
    rows = seq // GRID_W
    row_idx = jnp.broadcast_to(jnp.arange(rows)[:, None], (rows, GRID_W)).reshape(-1)
    col_idx = jnp.broadcast_to(jnp.arange(GRID_W)[None, :], (rows, GRID_W)).reshape(-1)

    def cos_sin(p, dim):
        inv = ROPE_THETA ** (-jnp.arange(0, dim, 2, dtype=F32) / dim)
        ang = inv[:, None] * p.astype(F32)[None, :]
        return jnp.cos(ang), jnp.sin(ang)

    rc, rs = cos_sin(row_idx, HEAD_DIM // 2)
    cc, cs = cos_sin(col_idx, HEAD_DIM // 2)
    sc, ss = cos_sin(pos, HEAD_DIM)
    return jnp.concatenate([rc, rc, cc, cc, -rs, rs, -cs, cs, sc, sc, -ss, ss], axis=0)


def _gain_columns(q_norm_a, k_norm_a, q_norm_b, k_norm_b):
    scale = HEAD_DIM ** -0.5 * LOG2_E

    def pair(gain, perm, mult):
        return [gain * mult, gain[:, perm] * mult]

    cols = pair(q_norm_a, PERM_AXIAL, scale) + pair(k_norm_a, PERM_AXIAL, 1.0)
    for g in range(N_B):
        cols += pair(q_norm_b[:, g], PERM_SEQ, scale) + pair(k_norm_b[:, g], PERM_SEQ, 1.0)
    cols = jnp.concatenate(cols, axis=1)
    return jnp.broadcast_to(cols[:, :, None], cols.shape + (V7X_LANES,))


def _split_w_in(w_in):
    sizes = [A_Q_W, A_KV_W, A_KV_W] + [B_W] * (3 * N_B) + [D_MODEL, D_MODEL]
    offs = np.concatenate([[0], np.cumsum(sizes)])
    col = lambda i: w_in[:, :, offs[i]:offs[i + 1]]
    names = ["qa", "ka", "va"] + [f"{t}b{g}" for g in range(N_B) for t in "qkv"] + ["ga", "gb"]
    parts = {n: col(i) for i, n in enumerate(names)}
    wn = jnp.concatenate([parts["vb0"], parts["vb1"], parts["vb2"], parts["ga"], parts["gb"]], axis=-1)
    wt = jnp.concatenate([parts[n] for n, _ in T_SEGS], axis=-1)
    return wn.astype(BF16), jnp.swapaxes(wt, 1, 2).astype(BF16)


def kernel(x, c, w_ada, b_ada, g_mix, g_mlp, w_in, q_norm_a, k_norm_a, q_norm_b, k_norm_b,
           w_branch_a, w_branch_b, w_out, w_ff1, w_ff2):
    nb, seq, _ = x.shape
    depth = w_in.shape[0]
    assert seq % GRID_W == 0 and seq % TM_MIX == 0 and seq % TM_IN == 0

    ada = _ada_all(c, w_ada, b_ada).reshape(depth, nb, 6, D_MODEL)
    rope = _rope_tables(seq)
    gains = _gain_columns(q_norm_a, k_norm_a, q_norm_b, k_norm_b)
    wn, wt = _split_w_in(w_in)
    wa, wb, wo = w_branch_a.astype(BF16), w_branch_b.astype(BF16), w_out.astype(BF16)
    w1, w2 = w_ff1.astype(BF16), w_ff2.astype(BF16)
    g_mix = g_mix.reshape(depth, 1, D_MODEL)
    g_mlp = g_mlp.reshape(depth, 1, D_MODEL)

    for l in range(depth):
        (qat, ka, vat, qb0, kb0, vb0, qb1, kb1, vb1, qb2, kb2, vb2, ga, gb) = _in_proj(
            x, ada[l], g_mix, wn, wt, rope, gains, l)
        oa = _attn_a(qat, ka, vat)
        obs, lss = [], []
        for (window, dil), q, k, v in zip(B_GROUPS, (qb0, qb1, qb2), (kb0, kb1, kb2), (vb0, vb1, vb2)):
            o, lse = _attn_b(q, k, v, window, dil)
            obs.append(o)
            lss.append(lse)
        x = _mix_mlp(x, ada[l], oa, obs, lss, ga, gb, wa, wb, wo, g_mlp, w1, w2, l)
    return x
```

```python
import functools

import jax
import jax.numpy as jnp
import numpy as np
from jax import lax
from jax.experimental import pallas as pl
from jax.experimental.pallas import tpu as pltpu

D_MODEL = 1024
HEAD_DIM = 64
A_Q_HEADS = 8
A_KV_HEADS = 2
B_GROUPS = ((128, 1), (512, 4), (2048, 16))
B_HEADS = 4
N_B = len(B_GROUPS)
D_FF = 4 * D_MODEL
GRID_W = 64
ROPE_THETA = 10000.0
EPS = 1e-6

A_Q_W = A_Q_HEADS * HEAD_DIM
A_KV_W = A_KV_HEADS * HEAD_DIM
B_W = B_HEADS * HEAD_DIM

V7X_LANES = 128
V7X_VMEM_BYTES = 64 * 1024 * 1024
VMEM_LIMIT = 52 * 1024 * 1024

BF16 = jnp.bfloat16
F32 = jnp.float32

CHUNK = 256
TM_IN = 512
TQ_A = 1024
A_PAIRS_PER_BODY = 3
TM_MIX = 512
FF_CHUNK = 1024
B_ROWS_PER_STEP = 1024
TQ_B = 128
KW_B = 256
NEG_BIG = -1e30
LOG2_E = 1.4426950408889634

PERM_AXIAL = np.concatenate([np.arange(16, 32), np.arange(0, 16), np.arange(48, 64), np.arange(32, 48)])
PERM_SEQ = np.concatenate([np.arange(32, 64), np.arange(0, 32)])

T_SEGS = (("qa", A_Q_W), ("ka", A_KV_W), ("va", A_KV_W),
          ("qb0", B_W), ("kb0", B_W), ("qb1", B_W), ("kb1", B_W), ("qb2", B_W), ("kb2", B_W))
T_ROWS = sum(r for _, r in T_SEGS)
N_COLS = N_B * B_W + 2 * D_MODEL
N_TABS = 16


def _cparams(sem):
    return pltpu.CompilerParams(dimension_semantics=sem, vmem_limit_bytes=VMEM_LIMIT)


def _ada_kernel(c_ref, w_ref, b_ref, o_ref):
    c = c_ref[...]
    c_act = c * (1.0 / (1.0 + jnp.exp(-c)))
    o_ref[...] = jnp.dot(c_act, w_ref[...], preferred_element_type=F32,
                         precision=lax.Precision.HIGHEST) + b_ref[...]


def _ada_all(c, w_ada, b_ada):
    depth = w_ada.shape[0]
    nb = c.shape[0]
    return pl.pallas_call(
        _ada_kernel,
        grid=(depth, 6),
        in_specs=[
            pl.BlockSpec((nb, D_MODEL), lambda l, j: (0, 0)),
            pl.BlockSpec((None, D_MODEL, D_MODEL), lambda l, j: (l, 0, j)),
            pl.BlockSpec((None, 1, D_MODEL), lambda l, j: (l, 0, j)),
        ],
        out_specs=pl.BlockSpec((None, nb, D_MODEL), lambda l, j: (l, 0, j)),
        out_shape=jax.ShapeDtypeStruct((depth, nb, 6 * D_MODEL), F32),
        compiler_params=_cparams(("arbitrary", "arbitrary")),
        name="ada",
    )(c, w_ada, b_ada.reshape(depth, 1, 6 * D_MODEL))


def _modulated_rms(x, g, scale, shift):
    ms = jnp.mean(x * x, axis=-1, keepdims=True)
    return (x * lax.rsqrt(ms + EPS)) * g * (1.0 + scale) + shift


def _norm_rope_t(q, cos_t, sin_t, perm_blocks):
    ss = jnp.sum(q * q, axis=0, keepdims=True)
    r = lax.rsqrt(ss * (1.0 / HEAD_DIM) + EPS)
    qp = jnp.concatenate([q[a:b] for a, b in perm_blocks], axis=0)
    return (q * cos_t + qp * sin_t) * r


AXIAL_BLOCKS = ((16, 32), (0, 16), (48, 64), (32, 48))
SEQ_BLOCKS = ((32, 64), (0, 32))


def _store_class_major(out_ref, stage_ref, tile, dil, sub):
    rows = tile.shape[0] // dil
    out_rows = slice(sub * rows, (sub + 1) * rows)
    if dil == 1:
        out_ref[out_rows, :] = tile.astype(BF16)
        return
    for half in range(B_W // V7X_LANES):
        stage_ref[sub, half] = tile[:, half * V7X_LANES:(half + 1) * V7X_LANES]
    for r in range(dil):
        for half in range(B_W // V7X_LANES):
            col = r * B_W + half * V7X_LANES
            out_ref[out_rows, col:col + V7X_LANES] = (
                stage_ref[sub, half, pl.ds(r, rows, stride=dil), :].astype(BF16))


def _in_kernel(x_ref, ada_ref, g_ref, wn_ref, wt_ref, rope_ref, gain_ref,
               qat_ref, ka_ref, vat_ref,
               qb0_ref, kb0_ref, vb0_ref, qb1_ref, kb1_ref, vb1_ref, qb2_ref, kb2_ref, vb2_ref,
               ga_ref, gb_ref, stage_q_ref, stage_k_ref, stage_v_ref):
    for sub in range(TM_IN // CHUNK):
        _in_subtile(sub, x_ref, ada_ref, g_ref, wn_ref, wt_ref, rope_ref, gain_ref, qat_ref, ka_ref, vat_ref,
                    (qb0_ref, qb1_ref, qb2_ref), (kb0_ref, kb1_ref, kb2_ref), (vb0_ref, vb1_ref, vb2_ref),
                    ga_ref, gb_ref, stage_q_ref, stage_k_ref, stage_v_ref)


def _in_subtile(sub, x_ref, ada_ref, g_ref, wn_ref, wt_ref, rope_ref, gain_ref, qat_ref, ka_ref, vat_ref,
                qb_refs, kb_refs, vb_refs, ga_ref, gb_ref, stage_q_ref, stage_k_ref, stage_v_ref):
    tok = slice(sub * CHUNK, (sub + 1) * CHUNK)
    x = x_ref[tok, :]
    h = _modulated_rms(x, g_ref[...], ada_ref[1:2, :], ada_ref[0:1, :]).astype(BF16)

    nt = (((1,), (1,)), ((), ()))

    def proj_t(row0, rows):
        return lax.dot_general(wt_ref[row0:row0 + rows, :], h, nt, preferred_element_type=F32)

    def gained(tab, base_row):
        gain = gain_ref[tab * HEAD_DIM:(tab + 1) * HEAD_DIM, :]
        gain = jnp.concatenate([gain] * (CHUNK // V7X_LANES), axis=1)
        return rope_ref[base_row * HEAD_DIM:(base_row + 1) * HEAD_DIM, tok] * gain

    def heads_t(rt, n_heads, tab0, blocks):
        base_row = 0 if blocks is AXIAL_BLOCKS else 2
        cos_t = gained(tab0, base_row)
        sin_t = gained(tab0 + 1, base_row + 1)
        return [_norm_rope_t(rt[i * HEAD_DIM:(i + 1) * HEAD_DIM], cos_t, sin_t, blocks)
                for i in range(n_heads)]

    row = 0
    rt = proj_t(row, A_Q_W)
    qa = heads_t(rt, A_Q_HEADS, 0, AXIAL_BLOCKS)
    for i in range(A_Q_HEADS):
        qat_ref[sub, i * HEAD_DIM:(i + 1) * HEAD_DIM, :] = qa[i].astype(BF16)
    row += A_Q_W

    rt = proj_t(row, 2 * A_KV_W)
    ka = heads_t(rt[:A_KV_W], A_KV_HEADS, 2, AXIAL_BLOCKS)
    ka_ref[tok, :] = jnp.concatenate(ka, axis=0).T.astype(BF16)
    vat_ref[sub] = rt[A_KV_W:].astype(BF16)
    row += 2 * A_KV_W

    for g in range(N_B):
        rt = proj_t(row, 2 * B_W)
        qb = heads_t(rt[:B_W], B_HEADS, 4 + 4 * g, SEQ_BLOCKS)
        kb = heads_t(rt[B_W:], B_HEADS, 6 + 4 * g, SEQ_BLOCKS)
        _store_class_major(qb_refs[g], stage_q_ref, jnp.concatenate(qb, axis=0).T, B_GROUPS[g][1], sub)
        _store_class_major(kb_refs[g], stage_k_ref, jnp.concatenate(kb, axis=0).T, B_GROUPS[g][1], sub)
        row += 2 * B_W

    for i, gate_ref in enumerate((ga_ref, gb_ref)):
        lo = N_B * B_W + i * D_MODEL
        z = jnp.dot(h, wn_ref[:, lo:lo + D_MODEL], preferred_element_type=F32)
        gate_ref[tok, :] = (1.0 / (1.0 + jnp.exp(-z))).astype(BF16)
    for g in reversed(range(N_B)):
        v = jnp.dot(h, wn_ref[:, g * B_W:(g + 1) * B_W], preferred_element_type=F32)
        _store_class_major(vb_refs[g], stage_v_ref, v, B_GROUPS[g][1], sub)


def _in_proj(x, ada_l, g_mix_l, wn, wt, rope, gains, l):
    nb, seq, _ = x.shape
    ns = seq // TM_IN
    nsub = TM_IN // CHUNK
    tok = lambda w: pl.BlockSpec((None, TM_IN, w), lambda s, b: (b, s, 0))
    tchunk = lambda r: pl.BlockSpec((None, nsub, r, CHUNK), lambda s, b: (b, s, 0, 0))
    nat = lambda w: jax.ShapeDtypeStruct((nb, seq, w), BF16)
    out_specs = [tchunk(A_Q_W), tok(A_KV_W), tchunk(A_KV_W)]
    out_shape = [jax.ShapeDtypeStruct((nb, seq // CHUNK, A_Q_W, CHUNK), BF16), nat(A_KV_W),
                 jax.ShapeDtypeStruct((nb, seq // CHUNK, A_KV_W, CHUNK), BF16)]
    for _, dil in B_GROUPS:
        out_specs += [pl.BlockSpec((None, TM_IN // dil, dil * B_W), lambda s, b: (b, s, 0))] * 3
        out_shape += [jax.ShapeDtypeStruct((nb, seq // dil, dil * B_W), BF16)] * 3
    out_specs += [tok(D_MODEL)] * 2
    out_shape += [nat(D_MODEL)] * 2
    return pl.pallas_call(
        _in_kernel,
        grid=(ns, nb),
        in_specs=[
            tok(D_MODEL),
            pl.BlockSpec((None, 6, D_MODEL), lambda s, b: (b, 0, 0)),
            pl.BlockSpec((None, 1, D_MODEL), lambda s, b: (l, 0, 0)),
            pl.BlockSpec((None, D_MODEL, N_COLS), lambda s, b: (l, 0, 0), pipeline_mode=pl.Buffered(1)),
            pl.BlockSpec((None, T_ROWS, D_MODEL), lambda s, b: (l, 0, 0), pipeline_mode=pl.Buffered(1)),
            pl.BlockSpec((4 * HEAD_DIM, TM_IN), lambda s, b: (0, s)),
            pl.BlockSpec((None, N_TABS * HEAD_DIM, V7X_LANES), lambda s, b: (l, 0, 0)),
        ],
        out_specs=out_specs,
        out_shape=out_shape,
        scratch_shapes=[pltpu.VMEM((nsub, B_W // V7X_LANES, CHUNK, V7X_LANES), F32)] * 3,
        compiler_params=_cparams(("arbitrary", "arbitrary")),
        name="in_proj",
    )(x, ada_l, g_mix_l, wn, wt, rope, gains)


A_REP = A_Q_HEADS // A_KV_HEADS
ONES_ROWS = 16


def _attn_a_kernel(qt_ref, k_ref, vt_ref, o_ref, qpad_ref, acc_ref, m_ref, s_ref, smax_ref):
    for sub in range(qt_ref.shape[0]):
        _attn_a_tile(qt_ref.at[sub], k_ref, vt_ref, o_ref.at[sub * CHUNK:(sub + 1) * CHUNK],
                     qpad_ref.at[sub], acc_ref.at[sub], m_ref.at[sub], s_ref, smax_ref)


def _attn_a_tile(qt_ref, k_ref, vt_ref, o_ref, qpad_ref, acc_ref, m_ref, s_ref, smax_ref):
    n_chunks, _, tk = vt_ref.shape
    tq = qt_ref.shape[1]
    qpad_ref[...] = jnp.zeros(qpad_ref.shape, BF16)
    for g in range(A_KV_HEADS):
        for hh in range(A_REP):
            h = g * A_REP + hh
            qpad_ref[g, g * HEAD_DIM:(g + 1) * HEAD_DIM, hh * tq:(hh + 1) * tq] = (
                qt_ref[h * HEAD_DIM:(h + 1) * HEAD_DIM, :])
    m_ref[...] = jnp.full(m_ref.shape, NEG_BIG, F32)
    acc_ref[...] = jnp.zeros(acc_ref.shape, F32)
    ones = jnp.ones((ONES_ROWS, tk), BF16)

    def scores(c, slot, g):
        kc = k_ref[pl.ds(pl.multiple_of(c * tk, tk), tk), :]
        s = jnp.dot(kc, qpad_ref[g], preferred_element_type=F32)
        s_ref[slot, g] = s
        smax_ref[slot, g] = jnp.max(s, axis=0, keepdims=True)

    def consume(c, slot, g):
        m_prev = m_ref[g]
        m_new = jnp.maximum(m_prev, smax_ref[slot, g])
        alpha = jnp.exp2(m_prev - m_new)
        p = jnp.exp2(s_ref[slot, g] - m_new).astype(BF16)
        v_aug = jnp.concatenate([vt_ref[c, g * HEAD_DIM:(g + 1) * HEAD_DIM, :], ones], axis=0)
        acc_ref[g] = acc_ref[g] * alpha + jnp.dot(v_aug, p, preferred_element_type=F32)
        m_ref[g] = m_new

    for g in range(A_KV_HEADS):
        scores(0, 0, g)

    def chunk_pair(c, last):
        for slot in range(2):
            for g in range(A_KV_HEADS):
                if not (last and slot == 1):
                    scores(c + slot + 1, 1 - slot, g)
                consume(c + slot, slot, g)

    body_chunks = 2 * A_PAIRS_PER_BODY

    def loop_body(i, carry):
        for j in range(A_PAIRS_PER_BODY):
            chunk_pair(body_chunks * i + 2 * j, False)
        return carry

    n_loop = (n_chunks - 2) // body_chunks
    lax.fori_loop(0, n_loop, loop_body, 0)
    for c in range(n_loop * body_chunks, n_chunks, 2):
        chunk_pair(c, c == n_chunks - 2)
    for j in range(A_Q_W // V7X_LANES):
        halves = []
        for h in (2 * j, 2 * j + 1):
            g, hh = divmod(h, A_REP)
            cols = slice(hh * tq, (hh + 1) * tq)
            halves.append(acc_ref[g, :HEAD_DIM, cols] / acc_ref[g, HEAD_DIM:HEAD_DIM + 1, cols])
        blk = jnp.concatenate(halves, axis=0)
        o_ref[:, j * V7X_LANES:(j + 1) * V7X_LANES] = blk.T.astype(BF16)


def _attn_a(qat, ka, vat):
    nb, ns, _, chunk = qat.shape
    seq = ka.shape[1]
    nq = TQ_A // chunk
    wide = A_REP * chunk
    return pl.pallas_call(
        _attn_a_kernel,
        grid=(nb, seq // TQ_A),
        in_specs=[
            pl.BlockSpec((None, nq, A_Q_W, chunk), lambda b, s: (b, s, 0, 0)),
            pl.BlockSpec((None, seq, A_KV_W), lambda b, s: (b, 0, 0)),
            pl.BlockSpec((None, ns, A_KV_W, chunk), lambda b, s: (b, 0, 0, 0)),
        ],
        out_specs=pl.BlockSpec((None, TQ_A, A_Q_W), lambda b, s: (b, s, 0)),
        out_shape=jax.ShapeDtypeStruct((nb, seq, A_Q_W), BF16),
        scratch_shapes=[pltpu.VMEM((nq, A_KV_HEADS, A_KV_W, wide), BF16),
                        pltpu.VMEM((nq, A_KV_HEADS, HEAD_DIM + ONES_ROWS, wide), F32),
                        pltpu.VMEM((nq, A_KV_HEADS, 1, wide), F32),
                        pltpu.VMEM((2, A_KV_HEADS, chunk, wide), F32),
                        pltpu.VMEM((2, A_KV_HEADS, 1, wide), F32)],
        compiler_params=_cparams(("arbitrary", "arbitrary")),
        name="attn_a",
    )(qat, ka, vat)


def _attn_b_kernel(q_ref, k_ref, v_ref, o_ref, lse_ref, *, half_window):
    tj = q_ref.shape[0]
    length = k_ref.shape[0]
    j_tile = pl.program_id(2) * tj
    lane = lax.broadcasted_iota(jnp.int32, (1, V7X_LANES), 1)
    first_head = lane < HEAD_DIM
    nt = (((1,), (1,)), ((), ()))
    ones = jnp.ones((KW_B, V7X_LANES), BF16)
    for qi in range(tj // TQ_B):
        j0 = j_tile + qi * TQ_B
        ks = jnp.clip(j0 - half_window, 0, length - KW_B)
        ks = pl.multiple_of(ks, half_window)
        jq = j0 + lax.broadcasted_iota(jnp.int32, (TQ_B, 1), 0)
        jk = ks + lax.broadcasted_iota(jnp.int32, (1, KW_B), 1)
        bias = jnp.where(jnp.abs(jq - jk) <= half_window, 0.0, NEG_BIG)
        bias2 = jnp.concatenate([bias, bias], axis=0)
        for pair in range(q_ref.shape[1] // V7X_LANES):
            cols = slice(pair * V7X_LANES, (pair + 1) * V7X_LANES)
            q2 = q_ref[qi * TQ_B:(qi + 1) * TQ_B, cols]
            k2 = k_ref[pl.ds(ks, KW_B), cols]
            v_aug = jnp.concatenate([v_ref[pl.ds(ks, KW_B), cols], ones], axis=1)
            zero = jnp.zeros_like(q2)
            qs = jnp.concatenate([jnp.where(first_head, q2, zero), jnp.where(first_head, zero, q2)], axis=0)
            s = lax.dot_general(qs, k2, nt, preferred_element_type=F32) + bias2
            m = jnp.max(s, axis=1, keepdims=True)
            p = jnp.exp2(s - m).astype(BF16)
            ol = jnp.dot(p, v_aug, preferred_element_type=F32)
            o_pair = jnp.where(first_head, ol[:TQ_B, :V7X_LANES], ol[TQ_B:, :V7X_LANES])
            l_pair = jnp.where(first_head, ol[:TQ_B, V7X_LANES:], ol[TQ_B:, V7X_LANES:])
            m_pair = jnp.where(first_head, m[:TQ_B], m[TQ_B:])
            o_ref[qi * TQ_B:(qi + 1) * TQ_B, cols] = (o_pair / l_pair).astype(BF16)
            lse_ref[qi * TQ_B:(qi + 1) * TQ_B, cols] = m_pair + jnp.log2(l_pair)


def _attn_b(q, k, v, window, dil):
    nb, length, _ = q.shape
    half_window = (window // 2) // dil
    assert half_window * 2 + TQ_B == KW_B and length >= KW_B
    tj = min(B_ROWS_PER_STEP, length)
    ncls = min(B_ROWS_PER_STEP // tj, dil)
    qspec = pl.BlockSpec((None, tj, ncls * B_W), lambda b, r, j: (b, j, r))
    kvspec = pl.BlockSpec((None, length, ncls * B_W), lambda b, r, j: (b, 0, r))
    return pl.pallas_call(
        functools.partial(_attn_b_kernel, half_window=half_window),
        grid=(nb, dil // ncls, length // tj),
        in_specs=[qspec, kvspec, kvspec],
        out_specs=[qspec, qspec],
        out_shape=[jax.ShapeDtypeStruct((nb, length, dil * B_W), BF16),
                   jax.ShapeDtypeStruct((nb, length, dil * B_W), F32)],
        compiler_params=_cparams(("arbitrary", "arbitrary", "arbitrary")),
        name=f"attn_b_d{dil}",
    )(q, k, v)


def _load_class_major(blk_ref, stage_ref, dil, sub):
    rows = CHUNK // dil
    blk_rows = slice(sub * rows, (sub + 1) * rows)
    if dil == 1:
        return blk_ref[blk_rows, :].astype(F32)
    halves = range(B_W // V7X_LANES)
    for r in range(dil):
        for half in halves:
            col = r * B_W + half * V7X_LANES
            stage_ref[sub, half, pl.ds(r, rows, stride=dil), :] = (
                blk_ref[blk_rows, col:col + V7X_LANES].astype(F32))
    return jnp.concatenate([stage_ref[sub, half] for half in halves], axis=1)


def _mix_mlp_kernel(x_ref, ada_ref, oa_ref, ob0_ref, ob1_ref, ob2_ref, ls0_ref, ls1_ref, ls2_ref,
                    ga_ref, gb_ref, wa_ref, wb_ref, wo_ref, g_ref, w1_ref, w2_ref, xo_ref, *stage_refs):
    dils = [d for _, d in B_GROUPS]
    for sub in range(TM_MIX // CHUNK):
        tok = slice(sub * CHUNK, (sub + 1) * CHUNK)
        ya = jnp.dot(oa_ref[tok, :], wa_ref[...], preferred_element_type=F32)
        lses = [_load_class_major(r, stage_refs[2 * g], dils[g], sub)
                for g, r in enumerate((ls0_ref, ls1_ref, ls2_ref))]
        outs = [_load_class_major(r, stage_refs[2 * g + 1], dils[g], sub)
                for g, r in enumerate((ob0_ref, ob1_ref, ob2_ref))]
        top = jnp.maximum(jnp.maximum(lses[0], lses[1]), lses[2])
        es = [jnp.exp2(v - top) for v in lses]
        den = es[0] + es[1] + es[2]
        ob = (es[0] * outs[0] + es[1] * outs[1] + es[2] * outs[2]) / den
        yb = jnp.dot(ob.astype(BF16), wb_ref[...], preferred_element_type=F32)
        mixed = ga_ref[tok, :].astype(F32) * ya + gb_ref[tok, :].astype(F32) * yb
        upd = jnp.dot(mixed.astype(BF16), wo_ref[...], preferred_element_type=F32)
        xo_ref[tok, :] = x_ref[tok, :] + ada_ref[2:3, :] * upd

    x = xo_ref[...]
    h = _modulated_rms(x, g_ref[...], ada_ref[4:5, :], ada_ref[3:4, :]).astype(BF16)
    acc = jnp.zeros(x.shape, F32)
    for c in range(D_FF // FF_CHUNK):
        a = jnp.dot(h, w1_ref[:, c * FF_CHUNK:(c + 1) * FF_CHUNK], preferred_element_type=F32)
        a = jnp.square(jnp.maximum(a, 0.0)).astype(BF16)
        acc = acc + jnp.dot(a, w2_ref[c * FF_CHUNK:(c + 1) * FF_CHUNK, :], preferred_element_type=F32)
    xo_ref[...] = x + ada_ref[5:6, :] * acc


def _mix_mlp(x, ada_l, oa, obs, lss, ga, gb, wa, wb, wo, g_mlp_l, w1, w2, l):
    nb, seq, _ = x.shape
    tok = lambda w: pl.BlockSpec((None, TM_MIX, w), lambda b, s: (b, s, 0))
    wspec = lambda r, c: pl.BlockSpec((None, r, c), lambda b, s: (l, 0, 0), pipeline_mode=pl.Buffered(1))
    cls = [pl.BlockSpec((None, TM_MIX // d, d * B_W), lambda b, s: (b, s, 0)) for _, d in B_GROUPS]
    return pl.pallas_call(
        _mix_mlp_kernel,
        grid=(nb, seq // TM_MIX),
        in_specs=[tok(D_MODEL), pl.BlockSpec((None, 6, D_MODEL), lambda b, s: (b, 0, 0)), tok(A_Q_W)]
                 + cls + cls + [tok(D_MODEL)] * 2
                 + [wspec(A_Q_W, D_MODEL), wspec(B_W, D_MODEL), wspec(D_MODEL, D_MODEL),
                    pl.BlockSpec((None, 1, D_MODEL), lambda b, s: (l, 0, 0)),
                    wspec(D_MODEL, D_FF), wspec(D_FF, D_MODEL)],
        out_specs=tok(D_MODEL),
        out_shape=jax.ShapeDtypeStruct(x.shape, F32),
        scratch_shapes=[pltpu.VMEM((TM_MIX // CHUNK, B_W // V7X_LANES, CHUNK, V7X_LANES), F32)] * (2 * N_B),
        compiler_params=_cparams(("arbitrary", "arbitrary")),
        name="mix_mlp",
    )(x, ada_l, oa, *obs, *lss, ga, gb, wa, wb, wo, g_mlp_l, w1, w2)


def _rope_tables(seq):
    pos = jnp.arange(seq)
    rows = seq // GRID_W
    row_idx = jnp.broadcast_to(jnp.arange(rows)[:, None], (rows, GRID_W)).reshape(-1)
    col_idx = jnp.broadcast_to(jnp.arange(GRID_W)[None, :], (rows, GRID_W)).reshape(-1)

    def cos_sin(p, dim):
        inv = ROPE_THETA ** (-jnp.arange(0, dim, 2, dtype=F32) / dim)
        ang = inv[:, None] * p.astype(F32)[None, :]
        return jnp.cos(ang), jnp.sin(ang)

    rc, rs = cos_sin(row_idx, HEAD_DIM // 2)
    cc, cs = cos_sin(col_idx, HEAD_DIM // 2)
    sc, ss = cos_sin(pos, HEAD_DIM)
    return jnp.concatenate([rc, rc, cc, cc, -rs, rs, -cs, cs, sc, sc, -ss, ss], axis=0)


def _gain_columns(q_norm_a, k_norm_a, q_norm_b, k_norm_b):
    scale = HEAD_DIM ** -0.5 * LOG2_E

    def pair(gain, perm, mult):
        return [gain * mult, gain[:, perm] * mult]

    cols = pair(q_norm_a, PERM_AXIAL, scale) + pair(k_norm_a, PERM_AXIAL, 1.0)
    for g in range(N_B):
        cols += pair(q_norm_b[:, g], PERM_SEQ, scale) + pair(k_norm_b[:, g], PERM_SEQ, 1.0)
    cols = jnp.concatenate(cols, axis=1)
    return jnp.broadcast_to(cols[:, :, None], cols.shape + (V7X_LANES,))


def _split_w_in(w_in):
    sizes = [A_Q_W, A_KV_W, A_KV_W] + [B_W] * (3 * N_B) + [D_MODEL, D_MODEL]
    offs = np.concatenate([[0], np.cumsum(sizes)])
    col = lambda i: w_in[:, :, offs[i]:offs[i + 1]]
    names = ["qa", "ka", "va"] + [f"{t}b{g}" for g in range(N_B) for t in "qkv"] + ["ga", "gb"]
    parts = {n: col(i) for i, n in enumerate(names)}
    wn = jnp.concatenate([parts["vb0"], parts["vb1"], parts["vb2"], parts["ga"], parts["gb"]], axis=-1)
    wt = jnp.concatenate([parts[n] for n, _ in T_SEGS], axis=-1)
    return wn.astype(BF16), jnp.swapaxes(wt, 1, 2).astype(BF16)


def kernel(x, c, w_ada, b_ada, g_mix, g_mlp, w_in, q_norm_a, k_norm_a, q_norm_b, k_norm_b,
           w_branch_a, w_branch_b, w_out, w_ff1, w_ff2):
    nb, seq, _ = x.shape
    depth = w_in.shape[0]
    assert seq % GRID_W == 0 and seq % TM_MIX == 0 and seq % TM_IN == 0

    ada = _ada_all(c, w_ada, b_ada).reshape(depth, nb, 6, D_MODEL)
    rope = _rope_tables(seq)
    gains = _gain_columns(q_norm_a, k_norm_a, q_norm_b, k_norm_b)
    wn, wt = _split_w_in(w_in)
    wa, wb, wo = w_branch_a.astype(BF16), w_branch_b.astype(BF16), w_out.astype(BF16)
    w1, w2 = w_ff1.astype(BF16), w_ff2.astype(BF16)
    g_mix = g_mix.reshape(depth, 1, D_MODEL)
    g_mlp = g_mlp.reshape(depth, 1, D_MODEL)

    for l in range(depth):
        (qat, ka, vat, qb0, kb0, vb0, qb1, kb1, vb1, qb2, kb2, vb2, ga, gb) = _in_proj(
            x, ada[l], g_mix, wn, wt, rope, gains, l)
        oa = _attn_a(qat, ka, vat)
        obs, lss = [], []
        for (window, dil), q, k, v in zip(B_GROUPS, (qb0, qb1, qb2), (kb0, kb1, kb2), (vb0, vb1, vb2)):
            o, lse = _attn_b(q, k, v, window, dil)
            obs.append(o)
            lss.append(lse)
        x = _mix_mlp(x, ada[l], oa, obs, lss, ga, gb, wa, wb, wo, g_mlp, w1, w2, l)
    return x
```

```python
import functools

import jax
import jax.numpy as jnp
import numpy as np
from jax import lax
from jax.experimental import pallas as pl
from jax.experimental.pallas import tpu as pltpu

D_MODEL = 1024
HEAD_DIM = 64
A_Q_HEADS = 8
A_KV_HEADS = 2
B_GROUPS = ((128, 1), (512, 4), (2048, 16))
B_HEADS = 4
N_B = len(B_GROUPS)
D_FF = 4 * D_MODEL
GRID_W = 64
ROPE_THETA = 10000.0
EPS = 1e-6

A_Q_W = A_Q_HEADS * HEAD_DIM
A_KV_W = A_KV_HEADS * HEAD_DIM
B_W = B_HEADS * HEAD_DIM

V7X_LANES = 128
V7X_VMEM_BYTES = 64 * 1024 * 1024
VMEM_LIMIT = 52 * 1024 * 1024

BF16 = jnp.bfloat16
F32 = jnp.float32

CHUNK = 256
TM_IN = 512
TQ_A = 1024
A_PAIRS_PER_BODY = 3
TM_MIX = 512
FF_CHUNK = 1024
B_ROWS_PER_STEP = 1024
TQ_B = 128
KW_B = 256
NEG_BIG = -1e30
LOG2_E = 1.4426950408889634

PERM_AXIAL = np.concatenate([np.arange(16, 32), np.arange(0, 16), np.arange(48, 64), np.arange(32, 48)])
PERM_SEQ = np.concatenate([np.arange(32, 64), np.arange(0, 32)])

T_SEGS = (("qa", A_Q_W), ("ka", A_KV_W), ("va", A_KV_W),
          ("qb0", B_W), ("kb0", B_W), ("qb1", B_W), ("kb1", B_W), ("qb2", B_W), ("kb2", B_W))
T_ROWS = sum(r for _, r in T_SEGS)
N_COLS = N_B * B_W + 2 * D_MODEL
N_TABS = 16


def _cparams(sem):
    return pltpu.CompilerParams(dimension_semantics=sem, vmem_limit_bytes=VMEM_LIMIT)


def _ada_kernel(c_ref, w_ref, b_ref, o_ref):
    c = c_ref[...]
    c_act = c * (1.0 / (1.0 + jnp.exp(-c)))
    o_ref[...] = jnp.dot(c_act, w_ref[...], preferred_element_type=F32,
                         precision=lax.Precision.HIGHEST) + b_ref[...]


def _ada_all(c, w_ada, b_ada):
    depth = w_ada.shape[0]
    nb = c.shape[0]
    return pl.pallas_call(
        _ada_kernel,
        grid=(depth, 6),
        in_specs=[
            pl.BlockSpec((nb, D_MODEL), lambda l, j: (0, 0)),
            pl.BlockSpec((None, D_MODEL, D_MODEL), lambda l, j: (l, 0, j)),
            pl.BlockSpec((None, 1, D_MODEL), lambda l, j: (l, 0, j)),
        ],
        out_specs=pl.BlockSpec((None, nb, D_MODEL), lambda l, j: (l, 0, j)),
        out_shape=jax.ShapeDtypeStruct((depth, nb, 6 * D_MODEL), F32),
        compiler_params=_cparams(("arbitrary", "arbitrary")),
        name="ada",
    )(c, w_ada, b_ada.reshape(depth, 1, 6 * D_MODEL))


def _modulated_rms(x, g, scale, shift):
    ms = jnp.mean(x * x, axis=-1, keepdims=True)
    return (x * lax.rsqrt(ms + EPS)) * g * (1.0 + scale) + shift


def _norm_rope_t(q, cos_t, sin_t, perm_blocks):
    ss = jnp.sum(q * q, axis=0, keepdims=True)
    r = lax.rsqrt(ss * (1.0 / HEAD_DIM) + EPS)
    qp = jnp.concatenate([q[a:b] for a, b in perm_blocks], axis=0)
    return (q * cos_t + qp * sin_t) * r


AXIAL_BLOCKS = ((16, 32), (0, 16), (48, 64), (32, 48))
SEQ_BLOCKS = ((32, 64), (0, 32))


def _store_class_major(out_ref, stage_ref, tile, dil, sub):
    rows = tile.shape[0] // dil
    out_rows = slice(sub * rows, (sub + 1) * rows)
    if dil == 1:
        out_ref[out_rows, :] = tile.astype(BF16)
        return
    for half in range(B_W // V7X_LANES):
        stage_ref[sub, half] = tile[:, half * V7X_LANES:(half + 1) * V7X_LANES]
    for r in range(dil):
        for half in range(B_W // V7X_LANES):
            col = r * B_W + half * V7X_LANES
            out_ref[out_rows, col:col + V7X_LANES] = (
                stage_ref[sub, half, pl.ds(r, rows, stride=dil), :].astype(BF16))


def _in_kernel(x_ref, ada_ref, g_ref, wn_ref, wt_ref, rope_ref, gain_ref,
               qat_ref, ka_ref, vat_ref,
               qb0_ref, kb0_ref, vb0_ref, qb1_ref, kb1_ref, vb1_ref, qb2_ref, kb2_ref, vb2_ref,
               ga_ref, gb_ref, stage_q_ref, stage_k_ref, stage_v_ref):
    for sub in range(TM_IN // CHUNK):
        _in_subtile(sub, x_ref, ada_ref, g_ref, wn_ref, wt_ref, rope_ref, gain_ref, qat_ref, ka_ref, vat_ref,
                    (qb0_ref, qb1_ref, qb2_ref), (kb0_ref, kb1_ref, kb2_ref), (vb0_ref, vb1_ref, vb2_ref),
                    ga_ref, gb_ref, stage_q_ref, stage_k_ref, stage_v_ref)


def _in_subtile(sub, x_ref, ada_ref, g_ref, wn_ref, wt_ref, rope_ref, gain_ref, qat_ref, ka_ref, vat_ref,
                qb_refs, kb_refs, vb_refs, ga_ref, gb_ref, stage_q_ref, stage_k_ref, stage_v_ref):
    tok = slice(sub * CHUNK, (sub + 1) * CHUNK)
    x = x_ref[tok, :]
    h = _modulated_rms(x, g_ref[...], ada_ref[1:2, :], ada_ref[0:1, :]).astype(BF16)

    nt = (((1,), (1,)), ((), ()))

    def proj_t(row0, rows):
        return lax.dot_general(wt_ref[row0:row0 + rows, :], h, nt, preferred_element_type=F32)

    def gained(tab, base_row):
        gain = gain_ref[tab * HEAD_DIM:(tab + 1) * HEAD_DIM, :]
        gain = jnp.concatenate([gain] * (CHUNK // V7X_LANES), axis=1)
        return rope_ref[base_row * HEAD_DIM:(base_row + 1) * HEAD_DIM, tok] * gain

    def heads_t(rt, n_heads, tab0, blocks):
        base_row = 0 if blocks is AXIAL_BLOCKS else 2
        cos_t = gained(tab0, base_row)
        sin_t = gained(tab0 + 1, base_row + 1)
        return [_norm_rope_t(rt[i * HEAD_DIM:(i + 1) * HEAD_DIM], cos_t, sin_t, blocks)
                for i in range(n_heads)]

    row = 0
    rt = proj_t(row, A_Q_W)
    qa = heads_t(rt, A_Q_HEADS, 0, AXIAL_BLOCKS)
    for i in range(A_Q_HEADS):
        qat_ref[sub, i * HEAD_DIM:(i + 1) * HEAD_DIM, :] = qa[i].astype(BF16)
    row += A_Q_W

    rt = proj_t(row, 2 * A_KV_W)
    ka = heads_t(rt[:A_KV_W], A_KV_HEADS, 2, AXIAL_BLOCKS)
    ka_ref[tok, :] = jnp.concatenate(ka, axis=0).T.astype(BF16)
    vat_ref[sub] = rt[A_KV_W:].astype(BF16)
    row += 2 * A_KV_W

    for g in range(N_B):
        rt = proj_t(row, 2 * B_W)
        qb = heads_t(rt[:B_W], B_HEADS, 4 + 4 * g, SEQ_BLOCKS)
        kb = heads_t(rt[B_W:], B_HEADS, 6 + 4 * g, SEQ_BLOCKS)
        _store_class_major(qb_refs[g], stage_q_ref, jnp.concatenate(qb, axis=0).T, B_GROUPS[g][1], sub)
        _store_class_major(kb_refs[g], stage_k_ref, jnp.concatenate(kb, axis=0).T, B_GROUPS[g][1], sub)
        row += 2 * B_W

    for i, gate_ref in enumerate((ga_ref, gb_ref)):
        lo = N_B * B_W + i * D_MODEL
        z = jnp.dot(h, wn_ref[:, lo:lo + D_MODEL], preferred_element_type=F32)
        gate_ref[tok, :] = (1.0 / (1.0 + jnp.exp(-z))).astype(BF16)
    for g in reversed(range(N_B)):
        v = jnp.dot(h, wn_ref[:, g * B_W:(g + 1) * B_W], preferred_element_type=F32)
        _store_class_major(vb_refs[g], stage_v_ref, v, B_GROUPS[g][1], sub)


def _in_proj(x, ada_l, g_mix_l, wn, wt, rope, gains, l):
    nb, seq, _ = x.shape
    ns = seq // TM_IN
    nsub = TM_IN // CHUNK
    tok = lambda w: pl.BlockSpec((None, TM_IN, w), lambda s, b: (b, s, 0))
    tchunk = lambda r: pl.BlockSpec((None, nsub, r, CHUNK), lambda s, b: (b, s, 0, 0))
    nat = lambda w: jax.ShapeDtypeStruct((nb, seq, w), BF16)
    out_specs = [tchunk(A_Q_W), tok(A_KV_W), tchunk(A_KV_W)]
    out_shape = [jax.ShapeDtypeStruct((nb, seq // CHUNK, A_Q_W, CHUNK), BF16), nat(A_KV_W),
                 jax.ShapeDtypeStruct((nb, seq // CHUNK, A_KV_W, CHUNK), BF16)]
    for _, dil in B_GROUPS:
        out_specs += [pl.BlockSpec((None, TM_IN // dil, dil * B_W), lambda s, b: (b, s, 0))] * 3
        out_shape += [jax.ShapeDtypeStruct((nb, seq // dil, dil * B_W), BF16)] * 3
    out_specs += [tok(D_MODEL)] * 2
    out_shape += [nat(D_MODEL)] * 2
    return pl.pallas_call(
        _in_kernel,
        grid=(ns, nb),
        in_specs=[
            tok(D_MODEL),
            pl.BlockSpec((None, 6, D_MODEL), lambda s, b: (b, 0, 0)),
            pl.BlockSpec((None, 1, D_MODEL), lambda s, b: (l, 0, 0)),
            pl.BlockSpec((None, D_MODEL, N_COLS), lambda s, b: (l, 0, 0), pipeline_mode=pl.Buffered(1)),
            pl.BlockSpec((None, T_ROWS, D_MODEL), lambda s, b: (l, 0, 0), pipeline_mode=pl.Buffered(1)),
            pl.BlockSpec((4 * HEAD_DIM, TM_IN), lambda s, b: (0, s)),
            pl.BlockSpec((None, N_TABS * HEAD_DIM, V7X_LANES), lambda s, b: (l, 0, 0)),
        ],
        out_specs=out_specs,
        out_shape=out_shape,
        scratch_shapes=[pltpu.VMEM((nsub, B_W // V7X_LANES, CHUNK, V7X_LANES), F32)] * 3,
        compiler_params=_cparams(("arbitrary", "arbitrary")),
        name="in_proj",
    )(x, ada_l, g_mix_l, wn, wt, rope, gains)


A_REP = A_Q_HEADS // A_KV_HEADS
ONES_ROWS = 16


def _attn_a_kernel(qt_ref, k_ref, vt_ref, o_ref, qpad_ref, acc_ref, m_ref, s_ref, smax_ref):
    for sub in range(qt_ref.shape[0]):
        _attn_a_tile(qt_ref.at[sub], k_ref, vt_ref, o_ref.at[sub * CHUNK:(sub + 1) * CHUNK],
                     qpad_ref.at[sub], acc_ref.at[sub], m_ref.at[sub], s_ref, smax_ref)


def _attn_a_tile(qt_ref, k_ref, vt_ref, o_ref, qpad_ref, acc_ref, m_ref, s_ref, smax_ref):
    n_chunks, _, tk = vt_ref.shape
    tq = qt_ref.shape[1]
    qpad_ref[...] = jnp.zeros(qpad_ref.shape, BF16)
    for g in range(A_KV_HEADS):
        for hh in range(A_REP):
            h = g * A_REP + hh
            qpad_ref[g, g * HEAD_DIM:(g + 1) * HEAD_DIM, hh * tq:(hh + 1) * tq] = (
                qt_ref[h * HEAD_DIM:(h + 1) * HEAD_DIM, :])
    m_ref[...] = jnp.full(m_ref.shape, NEG_BIG, F32)
    acc_ref[...] = jnp.zeros(acc_ref.shape, F32)
    ones = jnp.ones((ONES_ROWS, tk), BF16)

    def scores(c, slot, g, hh):
        lanes = slice(hh * tq, (hh + 1) * tq)
        kc = k_ref[pl.ds(pl.multiple_of(c * tk, tk), tk), :]
        s = jnp.dot(kc, qpad_ref[g, :, lanes], preferred_element_type=F32)
        s_ref[slot, g, :, lanes] = s
        smax_ref[slot, g, :, lanes] = jnp.max(s, axis=0, keepdims=True)

    def consume(c, slot, g, hh):
        lanes = slice(hh * tq, (hh + 1) * tq)
        m_prev = m_ref[g, :, lanes]
        m_new = jnp.maximum(m_prev, smax_ref[slot, g, :, lanes])
        alpha = jnp.exp2(m_prev - m_new)
        p = jnp.exp2(s_ref[slot, g, :, lanes] - m_new).astype(BF16)
        v_aug = jnp.concatenate([vt_ref[c, g * HEAD_DIM:(g + 1) * HEAD_DIM, :], ones], axis=0)
        acc_ref[g, :, lanes] = acc_ref[g, :, lanes] * alpha + jnp.dot(v_aug, p, preferred_element_type=F32)
        m_ref[g, :, lanes] = m_new

    for g in range(A_KV_HEADS):
        for hh in range(A_REP):
            scores(0, 0, g, hh)

    def chunk_pair(c, last):
        for slot in range(2):
            for g in range(A_KV_HEADS):
                for hh in range(A_REP):
                    if not (last and slot == 1):
                        scores(c + slot + 1, 1 - slot, g, hh)
                    consume(c + slot, slot, g, hh)

    body_chunks = 2 * A_PAIRS_PER_BODY

    def loop_body(i, carry):
        for j in range(A_PAIRS_PER_BODY):
            chunk_pair(body_chunks * i + 2 * j, False)
        return carry

    n_loop = (n_chunks - 2) // body_chunks
    lax.fori_loop(0, n_loop, loop_body, 0)
    for c in range(n_loop * body_chunks, n_chunks, 2):
        chunk_pair(c, c == n_chunks - 2)
    for j in range(A_Q_W // V7X_LANES):
        halves = []
        for h in (2 * j, 2 * j + 1):
            g, hh = divmod(h, A_REP)
            cols = slice(hh * tq, (hh + 1) * tq)
            halves.append(acc_ref[g, :HEAD_DIM, cols] / acc_ref[g, HEAD_DIM:HEAD_DIM + 1, cols])
        blk = jnp.concatenate(halves, axis=0)
        o_ref[:, j * V7X_LANES:(j + 1) * V7X_LANES] = blk.T.astype(BF16)


def _attn_a(qat, ka, vat):
    nb, ns, _, chunk = qat.shape
    seq = ka.shape[1]
    nq = TQ_A // chunk
    wide = A_REP * chunk
    return pl.pallas_call(
        _attn_a_kernel,
        grid=(nb, seq // TQ_A),
        in_specs=[
            pl.BlockSpec((None, nq, A_Q_W, chunk), lambda b, s: (b, s, 0, 0)),
            pl.BlockSpec((None, seq, A_KV_W), lambda b, s: (b, 0, 0)),
            pl.BlockSpec((None, ns, A_KV_W, chunk), lambda b, s: (b, 0, 0, 0)),
        ],
        out_specs=pl.BlockSpec((None, TQ_A, A_Q_W), lambda b, s: (b, s, 0)),
        out_shape=jax.ShapeDtypeStruct((nb, seq, A_Q_W), BF16),
        scratch_shapes=[pltpu.VMEM((nq, A_KV_HEADS, A_KV_W, wide), BF16),
                        pltpu.VMEM((nq, A_KV_HEADS, HEAD_DIM + ONES_ROWS, wide), F32),
                        pltpu.VMEM((nq, A_KV_HEADS, 1, wide), F32),
                        pltpu.VMEM((2, A_KV_HEADS, chunk, wide), F32),
                        pltpu.VMEM((2, A_KV_HEADS, 1, wide), F32)],
        compiler_params=_cparams(("arbitrary", "arbitrary")),
        name="attn_a",
    )(qat, ka, vat)


def _attn_b_kernel(q_ref, k_ref, v_ref, o_ref, lse_ref, *, half_window):
    tj = q_ref.shape[0]
    length = k_ref.shape[0]
    j_tile = pl.program_id(2) * tj
    lane = lax.broadcasted_iota(jnp.int32, (1, V7X_LANES), 1)
    first_head = lane < HEAD_DIM
    nt = (((1,), (1,)), ((), ()))
    ones = jnp.ones((KW_B, V7X_LANES), BF16)
    for qi in range(tj // TQ_B):
        j0 = j_tile + qi * TQ_B
        ks = jnp.clip(j0 - half_window, 0, length - KW_B)
        ks = pl.multiple_of(ks, half_window)
        jq = j0 + lax.broadcasted_iota(jnp.int32, (TQ_B, 1), 0)
        jk = ks + lax.broadcasted_iota(jnp.int32, (1, KW_B), 1)
        bias = jnp.where(jnp.abs(jq - jk) <= half_window, 0.0, NEG_BIG)
        bias2 = jnp.concatenate([bias, bias], axis=0)
        for pair in range(q_ref.shape[1] // V7X_LANES):
            cols = slice(pair * V7X_LANES, (pair + 1) * V7X_LANES)
            q2 = q_ref[qi * TQ_B:(qi + 1) * TQ_B, cols]
            k2 = k_ref[pl.ds(ks, KW_B), cols]
            v_aug = jnp.concatenate([v_ref[pl.ds(ks, KW_B), cols], ones], axis=1)
            zero = jnp.zeros_like(q2)
            qs = jnp.concatenate([jnp.where(first_head, q2, zero), jnp.where(first_head, zero, q2)], axis=0)
            s = lax.dot_general(qs, k2, nt, preferred_element_type=F32) + bias2
            m = jnp.max(s, axis=1, keepdims=True)
            p = jnp.exp2(s - m).astype(BF16)
            ol = jnp.dot(p, v_aug, preferred_element_type=F32)
            o_pair = jnp.where(first_head, ol[:TQ_B, :V7X_LANES], ol[TQ_B:, :V7X_LANES])
            l_pair = jnp.where(first_head, ol[:TQ_B, V7X_LANES:], ol[TQ_B:, V7X_LANES:])
            m_pair = jnp.where(first_head, m[:TQ_B], m[TQ_B:])
            o_ref[qi * TQ_B:(qi + 1) * TQ_B, cols] = (o_pair / l_pair).astype(BF16)
            lse_ref[qi * TQ_B:(qi + 1) * TQ_B, cols] = m_pair + jnp.log2(l_pair)


def _attn_b(q, k, v, window, dil):
    nb, length, _ = q.shape
    half_window = (window // 2) // dil
    assert half_window * 2 + TQ_B == KW_B and length >= KW_B
    tj = min(B_ROWS_PER_STEP, length)
    ncls = min(B_ROWS_PER_STEP // tj, dil)
    qspec = pl.BlockSpec((None, tj, ncls * B_W), lambda b, r, j: (b, j, r))
    kvspec = pl.BlockSpec((None, length, ncls * B_W), lambda b, r, j: (b, 0, r))
    return pl.pallas_call(
        functools.partial(_attn_b_kernel, half_window=half_window),
        grid=(nb, dil // ncls, length // tj),
        in_specs=[qspec, kvspec, kvspec],
        out_specs=[qspec, qspec],
        out_shape=[jax.ShapeDtypeStruct((nb, length, dil * B_W), BF16),
                   jax.ShapeDtypeStruct((nb, length, dil * B_W), F32)],
        compiler_params=_cparams(("arbitrary", "arbitrary", "arbitrary")),
        name=f"attn_b_d{dil}",
    )(q, k, v)


def _load_class_major(blk_ref, stage_ref, dil, sub):
    rows = CHUNK // dil
    blk_rows = slice(sub * rows, (sub + 1) * rows)
    if dil == 1:
        return blk_ref[blk_rows, :].astype(F32)
    halves = range(B_W // V7X_LANES)
    for r in range(dil):
        for half in halves:
            col = r * B_W + half * V7X_LANES
            stage_ref[sub, half, pl.ds(r, rows, stride=dil), :] = (
                blk_ref[blk_rows, col:col + V7X_LANES].astype(F32))
    return jnp.concatenate([stage_ref[sub, half] for half in halves], axis=1)


def _mix_mlp_kernel(x_ref, ada_ref, oa_ref, ob0_ref, ob1_ref, ob2_ref, ls0_ref, ls1_ref, ls2_ref,
                    ga_ref, gb_ref, wa_ref, wb_ref, wo_ref, g_ref, w1_ref, w2_ref, xo_ref, *stage_refs):
    dils = [d for _, d in B_GROUPS]
    for sub in range(TM_MIX // CHUNK):
        tok = slice(sub * CHUNK, (sub + 1) * CHUNK)
        ya = jnp.dot(oa_ref[tok, :], wa_ref[...], preferred_element_type=F32)
        lses = [_load_class_major(r, stage_refs[2 * g], dils[g], sub)
                for g, r in enumerate((ls0_ref, ls1_ref, ls2_ref))]
        outs = [_load_class_major(r, stage_refs[2 * g + 1], dils[g], sub)
                for g, r in enumerate((ob0_ref, ob1_ref, ob2_ref))]
        top = jnp.maximum(jnp.maximum(lses[0], lses[1]), lses[2])
        es = [jnp.exp2(v - top) for v in lses]
        den = es[0] + es[1] + es[2]
        ob = (es[0] * outs[0] + es[1] * outs[1] + es[2] * outs[2]) / den
        yb = jnp.dot(ob.astype(BF16), wb_ref[...], preferred_element_type=F32)
        mixed = ga_ref[tok, :].astype(F32) * ya + gb_ref[tok, :].astype(F32) * yb
        upd = jnp.dot(mixed.astype(BF16), wo_ref[...], preferred_element_type=F32)
        xo_ref[tok, :] = x_ref[tok, :] + ada_ref[2:3, :] * upd

    x = xo_ref[...]
    h = _modulated_rms(x, g_ref[...], ada_ref[4:5, :], ada_ref[3:4, :]).astype(BF16)
    acc = jnp.zeros(x.shape, F32)
    for c in range(D_FF // FF_CHUNK):
        a = jnp.dot(h, w1_ref[:, c * FF_CHUNK:(c + 1) * FF_CHUNK], preferred_element_type=F32)
        a = jnp.square(jnp.maximum(a, 0.0)).astype(BF16)
        acc = acc + jnp.dot(a, w2_ref[c * FF_CHUNK:(c + 1) * FF_CHUNK, :], preferred_element_type=F32)
    xo_ref[...] = x + ada_ref[5:6, :] * acc


def _mix_mlp(x, ada_l, oa, obs, lss, ga, gb, wa, wb, wo, g_mlp_l, w1, w2, l):
    nb, seq, _ = x.shape
    tok = lambda w: pl.BlockSpec((None, TM_MIX, w), lambda b, s: (b, s, 0))
    wspec = lambda r, c: pl.BlockSpec((None, r, c), lambda b, s: (l, 0, 0), pipeline_mode=pl.Buffered(1))
    cls = [pl.BlockSpec((None, TM_MIX // d, d * B_W), lambda b, s: (b, s, 0)) for _, d in B_GROUPS]
    return pl.pallas_call(
        _mix_mlp_kernel,
        grid=(nb, seq // TM_MIX),
        in_specs=[tok(D_MODEL), pl.BlockSpec((None, 6, D_MODEL), lambda b, s: (b, 0, 0)), tok(A_Q_W)]
                 + cls + cls + [tok(D_MODEL)] * 2
                 + [wspec(A_Q_W, D_MODEL), wspec(B_W, D_MODEL), wspec(D_MODEL, D_MODEL),
                    pl.BlockSpec((None, 1, D_MODEL), lambda b, s: (l, 0, 0)),
                    wspec(D_MODEL, D_FF), wspec(D_FF, D_MODEL)],
        out_specs=tok(D_MODEL),
        out_shape=jax.ShapeDtypeStruct(x.shape, F32),
        scratch_shapes=[pltpu.VMEM((TM_MIX // CHUNK, B_W // V7X_LANES, CHUNK, V7X_LANES), F32)] * (2 * N_B),
        compiler_params=_cparams(("arbitrary", "arbitrary")),
        name="mix_mlp",
    )(x, ada_l, oa, *obs, *lss, ga, gb, wa, wb, wo, g_mlp_l, w1, w2)


def _rope_tables(seq):
    pos = jnp.arange(seq)
    rows = seq // GRID_W
    row_idx = jnp.broadcast_to(jnp.arange(rows)[:, None], (rows, GRID_W)).reshape(-1)
    col_idx = jnp.broadcast_to(jnp.arange(GRID_W)[None, :], (rows, GRID_W)).reshape(-1)

    def cos_sin(p, dim):
        inv = ROPE_THETA ** (-jnp.arange(0, dim, 2, dtype=F32) / dim)
        ang = inv[:, None] * p.astype(F32)[None, :]
        return jnp.cos(ang), jnp.sin(ang)

    rc, rs = cos_sin(row_idx, HEAD_DIM // 2)
    cc, cs = cos_sin(col_idx, HEAD_DIM // 2)
    sc, ss = cos_sin(pos, HEAD_DIM)
    return jnp.concatenate([rc, rc, cc, cc, -rs, rs, -cs, cs, sc, sc, -ss, ss], axis=0)


def _gain_columns(q_norm_a, k_norm_a, q_norm_b, k_norm_b):
    scale = HEAD_DIM ** -0.5 * LOG2_E

    def pair(gain, perm, mult):
        return [gain * mult, gain[:, perm] * mult]

    cols = pair(q_norm_a, PERM_AXIAL, scale) + pair(k_norm_a, PERM_AXIAL, 1.0)
    for g in range(N_B):
        cols += pair(q_norm_b[:, g], PERM_SEQ, scale) + pair(k_norm_b[:, g], PERM_SEQ, 1.0)
    cols = jnp.concatenate(cols, axis=1)
    return jnp.broadcast_to(cols[:, :, None], cols.shape + (V7X_LANES,))


def _split_w_in(w_in):
    sizes = [A_Q_W, A_KV_W, A_KV_W] + [B_W] * (3 * N_B) + [D_MODEL, D_MODEL]
    offs = np.concatenate([[0], np.cumsum(sizes)])
    col = lambda i: w_in[:, :, offs[i]:offs[i + 1]]
    names = ["qa", "ka", "va"] + [f"{t}b{g}" for g in range(N_B) for t in "qkv"] + ["ga", "gb"]
    parts = {n: col(i) for i, n in enumerate(names)}
    wn = jnp.concatenate([parts["vb0"], parts["vb1"], parts["vb2"], parts["ga"], parts["gb"]], axis=-1)
    wt = jnp.concatenate([parts[n] for n, _ in T_SEGS], axis=-1)
    return wn.astype(BF16), jnp.swapaxes(wt, 1, 2).astype(BF16)


def kernel(x, c, w_ada, b_ada, g_mix, g_mlp, w_in, q_norm_a, k_norm_a, q_norm_b, k_norm_b,
           w_branch_a, w_branch_b, w_out, w_ff1, w_ff2):
    nb, seq, _ = x.shape
    depth = w_in.shape[0]
    assert seq % GRID_W == 0 and seq % TM_MIX == 0 and seq % TM_IN == 0

    ada = _ada_all(c, w_ada, b_ada).reshape(depth, nb, 6, D_MODEL)
    rope = _rope_tables(seq)
    gains = _gain_columns(q_norm_a, k_norm_a, q_norm_b, k_norm_b)
    wn, wt = _split_w_in(w_in)
    wa, wb, wo = w_branch_a.astype(BF16), w_branch_b.astype(BF16), w_out.astype(BF16)
    w1, w2 = w_ff1.astype(BF16), w_ff2.astype(BF16)
    g_mix = g_mix.reshape(depth, 1, D_MODEL)
    g_mlp = g_mlp.reshape(depth, 1, D_MODEL)

    for l in range(depth):
        (qat, ka, vat, qb0, kb0, vb0, qb1, kb1, vb1, qb2, kb2, vb2, ga, gb) = _in_proj(
            x, ada[l], g_mix, wn, wt, rope, gains, l)
        oa = _attn_a(qat, ka, vat)
        obs, lss = [], []
        for (window, dil), q, k, v in zip(B_GROUPS, (qb0, qb1, qb2), (kb0, kb1, kb2), (vb0, vb1, vb2)):
            o, lse = _attn_b(q, k, v, window, dil)
            obs.append(o)
            lss.append(lse)
        x = _mix_mlp(x, ada[l], oa, obs, lss, ga, gb, wa, wb, wo, g_mlp, w1, w2, l)
    return x
```

```python
import functools

import jax
import jax.numpy as jnp
import numpy as np
from jax import lax
from jax.experimental import pallas as pl
from jax.experimental.pallas import tpu as pltpu

D_MODEL = 1024
HEAD_DIM = 64
A_Q_HEADS = 8
A_KV_HEADS = 2
B_GROUPS = ((128, 1), (512, 4), (2048, 16))
B_HEADS = 4
N_B = len(B_GROUPS)
D_FF = 4 * D_MODEL
GRID_W = 64
ROPE_THETA = 10000.0
EPS = 1e-6

A_Q_W = A_Q_HEADS * HEAD_DIM
A_KV_W = A_KV_HEADS * HEAD_DIM
B_W = B_HEADS * HEAD_DIM

V7X_LANES = 128
V7X_VMEM_BYTES = 64 * 1024 * 1024
VMEM_LIMIT = 52 * 1024 * 1024

BF16 = jnp.bfloat16
F32 = jnp.float32

CHUNK = 256
TM_IN = 512
TQ_A = 1024
A_PAIRS_PER_BODY = 3
TM_MIX = 512
FF_CHUNK = 1024
B_ROWS_PER_STEP = 1024
TQ_B = 128
KW_B = 256
NEG_BIG = -1e30
LOG2_E = 1.4426950408889634

PERM_AXIAL = np.concatenate([np.arange(16, 32), np.arange(0, 16), np.arange(48, 64), np.arange(32, 48)])
PERM_SEQ = np.concatenate([np.arange(32, 64), np.arange(0, 32)])

T_SEGS = (("qa", A_Q_W), ("ka", A_KV_W), ("va", A_KV_W),
          ("qb0", B_W), ("kb0", B_W), ("qb1", B_W), ("kb1", B_W), ("qb2", B_W), ("kb2", B_W))
T_ROWS = sum(r for _, r in T_SEGS)
N_COLS = N_B * B_W + 2 * D_MODEL
N_TABS = 16


def _cparams(sem):
    return pltpu.CompilerParams(dimension_semantics=sem, vmem_limit_bytes=VMEM_LIMIT)


def _ada_kernel(c_ref, w_ref, b_ref, o_ref):
    c = c_ref[...]
    c_act = c * (1.0 / (1.0 + jnp.exp(-c)))
    o_ref[...] = jnp.dot(c_act, w_ref[...], preferred_element_type=F32,
                         precision=lax.Precision.HIGHEST) + b_ref[...]


def _ada_all(c, w_ada, b_ada):
    depth = w_ada.shape[0]
    nb = c.shape[0]
    return pl.pallas_call(
        _ada_kernel,
        grid=(depth, 6),
        in_specs=[
            pl.BlockSpec((nb, D_MODEL), lambda l, j: (0, 0)),
            pl.BlockSpec((None, D_MODEL, D_MODEL), lambda l, j: (l, 0, j)),
            pl.BlockSpec((None, 1, D_MODEL), lambda l, j: (l, 0, j)),
        ],
        out_specs=pl.BlockSpec((None, nb, D_MODEL), lambda l, j: (l, 0, j)),
        out_shape=jax.ShapeDtypeStruct((depth, nb, 6 * D_MODEL), F32),
        compiler_params=_cparams(("arbitrary", "arbitrary")),
        name="ada",
    )(c, w_ada, b_ada.reshape(depth, 1, 6 * D_MODEL))


def _modulated_rms(x, g, scale, shift):
    ms = jnp.mean(x * x, axis=-1, keepdims=True)
    return (x * lax.rsqrt(ms + EPS)) * g * (1.0 + scale) + shift


def _norm_rope_t(q, cos_t, sin_t, perm_blocks):
    ss = jnp.sum(q * q, axis=0, keepdims=True)
    r = lax.rsqrt(ss * (1.0 / HEAD_DIM) + EPS)
    qp = jnp.concatenate([q[a:b] for a, b in perm_blocks], axis=0)
    return (q * cos_t + qp * sin_t) * r


AXIAL_BLOCKS = ((16, 32), (0, 16), (48, 64), (32, 48))
SEQ_BLOCKS = ((32, 64), (0, 32))


def _store_class_major(out_ref, stage_ref, tile, dil, sub):
    rows = tile.shape[0] // dil
    out_rows = slice(sub * rows, (sub + 1) * rows)
    if dil == 1:
        out_ref[out_rows, :] = tile.astype(BF16)
        return
    for half in range(B_W // V7X_LANES):
        stage_ref[sub, half] = tile[:, half * V7X_LANES:(half + 1) * V7X_LANES]
    for r in range(dil):
        for half in range(B_W // V7X_LANES):
            col = r * B_W + half * V7X_LANES
            out_ref[out_rows, col:col + V7X_LANES] = (
                stage_ref[sub, half, pl.ds(r, rows, stride=dil), :].astype(BF16))


def _in_kernel(x_ref, ada_ref, g_ref, wn_ref, wt_ref, rope_ref, gain_ref,
               qat_ref, ka_ref, vat_ref,
               qb0_ref, kb0_ref, vb0_ref, qb1_ref, kb1_ref, vb1_ref, qb2_ref, kb2_ref, vb2_ref,
               ga_ref, gb_ref, stage_q_ref, stage_k_ref, stage_v_ref):
    for sub in range(TM_IN // CHUNK):
        _in_subtile(sub, x_ref, ada_ref, g_ref, wn_ref, wt_ref, rope_ref, gain_ref, qat_ref, ka_ref, vat_ref,
                    (qb0_ref, qb1_ref, qb2_ref), (kb0_ref, kb1_ref, kb2_ref), (vb0_ref, vb1_ref, vb2_ref),
                    ga_ref, gb_ref, stage_q_ref, stage_k_ref, stage_v_ref)


def _in_subtile(sub, x_ref, ada_ref, g_ref, wn_ref, wt_ref, rope_ref, gain_ref, qat_ref, ka_ref, vat_ref,
                qb_refs, kb_refs, vb_refs, ga_ref, gb_ref, stage_q_ref, stage_k_ref, stage_v_ref):
    tok = slice(sub * CHUNK, (sub + 1) * CHUNK)
    x = x_ref[tok, :]
    h = _modulated_rms(x, g_ref[...], ada_ref[1:2, :], ada_ref[0:1, :]).astype(BF16)

    nt = (((1,), (1,)), ((), ()))

    def proj_t(row0, rows):
        return lax.dot_general(wt_ref[row0:row0 + rows, :], h, nt, preferred_element_type=F32)

    def gained(tab, base_row):
        gain = gain_ref[tab * HEAD_DIM:(tab + 1) * HEAD_DIM, :]
        gain = jnp.concatenate([gain] * (CHUNK // V7X_LANES), axis=1)
        return rope_ref[base_row * HEAD_DIM:(base_row + 1) * HEAD_DIM, tok] * gain

    def heads_t(rt, n_heads, tab0, blocks):
        base_row = 0 if blocks is AXIAL_BLOCKS else 2
        cos_t = gained(tab0, base_row)
        sin_t = gained(tab0 + 1, base_row + 1)
        return [_norm_rope_t(rt[i * HEAD_DIM:(i + 1) * HEAD_DIM], cos_t, sin_t, blocks)
                for i in range(n_heads)]

    row = 0
    rt = proj_t(row, A_Q_W)
    qa = heads_t(rt, A_Q_HEADS, 0, AXIAL_BLOCKS)
    for i in range(A_Q_HEADS):
        qat_ref[sub, i * HEAD_DIM:(i + 1) * HEAD_DIM, :] = qa[i].astype(BF16)
    row += A_Q_W

    rt = proj_t(row, 2 * A_KV_W)
    ka = heads_t(rt[:A_KV_W], A_KV_HEADS, 2, AXIAL_BLOCKS)
    ka_ref[tok, :] = jnp.concatenate(ka, axis=0).T.astype(BF16)
    vat_ref[sub] = rt[A_KV_W:].astype(BF16)
    row += 2 * A_KV_W

    for g in range(N_B):
        rt = proj_t(row, 2 * B_W)
        qb = heads_t(rt[:B_W], B_HEADS, 4 + 4 * g, SEQ_BLOCKS)
        kb = heads_t(rt[B_W:], B_HEADS, 6 + 4 * g, SEQ_BLOCKS)
        _store_class_major(qb_refs[g], stage_q_ref, jnp.concatenate(qb, axis=0).T, B_GROUPS[g][1], sub)
        _store_class_major(kb_refs[g], stage_k_ref, jnp.concatenate(kb, axis=0).T, B_GROUPS[g][1], sub)
        row += 2 * B_W

    for i, gate_ref in enumerate((ga_ref, gb_ref)):
        lo = N_B * B_W + i * D_MODEL
        z = jnp.dot(h, wn_ref[:, lo:lo + D_MODEL], preferred_element_type=F32)
        gate_ref[tok, :] = (1.0 / (1.0 + jnp.exp(-z))).astype(BF16)
    for g in reversed(range(N_B)):
        v = jnp.dot(h, wn_ref[:, g * B_W:(g + 1) * B_W], preferred_element_type=F32)
        _store_class_major(vb_refs[g], stage_v_ref, v, B_GROUPS[g][1], sub)


def _in_proj(x, ada_l, g_mix_l, wn, wt, rope, gains, l):
    nb, seq, _ = x.shape
    ns = seq // TM_IN
    nsub = TM_IN // CHUNK
    tok = lambda w: pl.BlockSpec((None, TM_IN, w), lambda s, b: (b, s, 0))
    tchunk = lambda r: pl.BlockSpec((None, nsub, r, CHUNK), lambda s, b: (b, s, 0, 0))
    nat = lambda w: jax.ShapeDtypeStruct((nb, seq, w), BF16)
    out_specs = [tchunk(A_Q_W), tok(A_KV_W), tchunk(A_KV_W)]
    out_shape = [jax.ShapeDtypeStruct((nb, seq // CHUNK, A_Q_W, CHUNK), BF16), nat(A_KV_W),
                 jax.ShapeDtypeStruct((nb, seq // CHUNK, A_KV_W, CHUNK), BF16)]
    for _, dil in B_GROUPS:
        out_specs += [pl.BlockSpec((None, TM_IN // dil, dil * B_W), lambda s, b: (b, s, 0))] * 3
        out_shape += [jax.ShapeDtypeStruct((nb, seq // dil, dil * B_W), BF16)] * 3
    out_specs += [tok(D_MODEL)] * 2
    out_shape += [nat(D_MODEL)] * 2
    return pl.pallas_call(
        _in_kernel,
        grid=(ns, nb),
        in_specs=[
            tok(D_MODEL),
            pl.BlockSpec((None, 6, D_MODEL), lambda s, b: (b, 0, 0)),
            pl.BlockSpec((None, 1, D_MODEL), lambda s, b: (l, 0, 0)),
            pl.BlockSpec((None, D_MODEL, N_COLS), lambda s, b: (l, 0, 0), pipeline_mode=pl.Buffered(1)),
            pl.BlockSpec((None, T_ROWS, D_MODEL), lambda s, b: (l, 0, 0), pipeline_mode=pl.Buffered(1)),
            pl.BlockSpec((4 * HEAD_DIM, TM_IN), lambda s, b: (0, s)),
            pl.BlockSpec((None, N_TABS * HEAD_DIM, V7X_LANES), lambda s, b: (l, 0, 0)),
        ],
        out_specs=out_specs,
        out_shape=out_shape,
        scratch_shapes=[pltpu.VMEM((nsub, B_W // V7X_LANES, CHUNK, V7X_LANES), F32)] * 3,
        compiler_params=_cparams(("arbitrary", "arbitrary")),
        name="in_proj",
    )(x, ada_l, g_mix_l, wn, wt, rope, gains)


A_REP = A_Q_HEADS // A_KV_HEADS
ONES_ROWS = 16


def _attn_a_kernel(qt_ref, k_ref, vt_ref, o_ref, qpad_ref, acc_ref, m_ref, s_ref, smax_ref):
    for sub in range(qt_ref.shape[0]):
        _attn_a_tile(qt_ref.at[sub], k_ref, vt_ref, o_ref.at[sub * CHUNK:(sub + 1) * CHUNK],
                     qpad_ref.at[sub], acc_ref.at[sub], m_ref.at[sub], s_ref, smax_ref)


def _attn_a_tile(qt_ref, k_ref, vt_ref, o_ref, qpad_ref, acc_ref, m_ref, s_ref, smax_ref):
    n_chunks, _, tk = vt_ref.shape
    tq = qt_ref.shape[1]
    qpad_ref[...] = jnp.zeros(qpad_ref.shape, BF16)
    for g in range(A_KV_HEADS):
        for hh in range(A_REP):
            h = g * A_REP + hh
            qpad_ref[g, g * HEAD_DIM:(g + 1) * HEAD_DIM, hh * tq:(hh + 1) * tq] = (
                qt_ref[h * HEAD_DIM:(h + 1) * HEAD_DIM, :])
    m_ref[...] = jnp.full(m_ref.shape, NEG_BIG, F32)
    acc_ref[...] = jnp.zeros(acc_ref.shape, F32)
    ones = jnp.ones((ONES_ROWS, tk), BF16)

    def scores(c, slot, g, hh):
        lanes = slice(hh * tq, (hh + 1) * tq)
        kc = k_ref[pl.ds(pl.multiple_of(c * tk, tk), tk), :]
        s = jnp.dot(kc, qpad_ref[g, :, lanes], preferred_element_type=F32)
        s_ref[slot, g, :, lanes] = s
        smax_ref[slot, g, :, lanes] = jnp.max(s, axis=0, keepdims=True)

    def consume(c, slot, g, hh):
        lanes = slice(hh * tq, (hh + 1) * tq)
        m_prev = m_ref[g, :, lanes]
        m_new = jnp.maximum(m_prev, smax_ref[slot, g, :, lanes])
        alpha = jnp.exp2(m_prev - m_new)
        p = jnp.exp2(s_ref[slot, g, :, lanes] - m_new).astype(BF16)
        v_aug = jnp.concatenate([vt_ref[c, g * HEAD_DIM:(g + 1) * HEAD_DIM, :], ones], axis=0)
        acc_ref[g, :, lanes] = acc_ref[g, :, lanes] * alpha + jnp.dot(v_aug, p, preferred_element_type=F32)
        m_ref[g, :, lanes] = m_new

    for g in range(A_KV_HEADS):
        for hh in range(A_REP):
            scores(0, 0, g, hh)

    def chunk_pair(c, last):
        for slot in range(2):
            for g in range(A_KV_HEADS):
                for hh in range(A_REP):
                    if not (last and slot == 1):
                        scores(c + slot + 1, 1 - slot, g, hh)
                    consume(c + slot, slot, g, hh)

    body_chunks = 2 * A_PAIRS_PER_BODY

    def loop_body(i, carry):
        for j in range(A_PAIRS_PER_BODY):
            chunk_pair(body_chunks * i + 2 * j, False)
        return carry

    n_loop = (n_chunks - 2) // body_chunks
    lax.fori_loop(0, n_loop, loop_body, 0)
    for c in range(n_loop * body_chunks, n_chunks, 2):
        chunk_pair(c, c == n_chunks - 2)
    for j in range(A_Q_W // V7X_LANES):
        halves = []
        for h in (2 * j, 2 * j + 1):
            g, hh = divmod(h, A_REP)
            cols = slice(hh * tq, (hh + 1) * tq)
            halves.append(acc_ref[g, :HEAD_DIM, cols] / acc_ref[g, HEAD_DIM:HEAD_DIM + 1, cols])
        blk = jnp.concatenate(halves, axis=0)
        o_ref[:, j * V7X_LANES:(j + 1) * V7X_LANES] = blk.T.astype(BF16)


def _attn_a(qat, ka, vat):
    nb, ns, _, chunk = qat.shape
    seq = ka.shape[1]
    nq = TQ_A // chunk
    wide = A_REP * chunk
    return pl.pallas_call(
        _attn_a_kernel,
        grid=(nb, seq // TQ_A),
        in_specs=[
            pl.BlockSpec((None, nq, A_Q_W, chunk), lambda b, s: (b, s, 0, 0)),
            pl.BlockSpec((None, seq, A_KV_W), lambda b, s: (b, 0, 0)),
            pl.BlockSpec((None, ns, A_KV_W, chunk), lambda b, s: (b, 0, 0, 0)),
        ],
        out_specs=pl.BlockSpec((None, TQ_A, A_Q_W), lambda b, s: (b, s, 0)),
        out_shape=jax.ShapeDtypeStruct((nb, seq, A_Q_W), BF16),
        scratch_shapes=[pltpu.VMEM((nq, A_KV_HEADS, A_KV_W, wide), BF16),
                        pltpu.VMEM((nq, A_KV_HEADS, HEAD_DIM + ONES_ROWS, wide), F32),
                        pltpu.VMEM((nq, A_KV_HEADS, 1, wide), F32),
                        pltpu.VMEM((2, A_KV_HEADS, chunk, wide), F32),
                        pltpu.VMEM((2, A_KV_HEADS, 1, wide), F32)],
        compiler_params=_cparams(("arbitrary", "arbitrary")),
        name="attn_a",
    )(qat, ka, vat)


def _attn_b_kernel(q_ref, k_ref, v_ref, o_ref, lse_ref, *, half_window):
    tj = q_ref.shape[0]
    length = k_ref.shape[0]
    j_tile = pl.program_id(2) * tj
    lane = lax.broadcasted_iota(jnp.int32, (1, V7X_LANES), 1)
    first_head = lane < HEAD_DIM
    nt = (((1,), (1,)), ((), ()))
    ones = jnp.ones((KW_B, V7X_LANES), BF16)
    n_pairs = q_ref.shape[1] // V7X_LANES
    units = [(qi, pair) for qi in range(tj // TQ_B) for pair in range(n_pairs)]
    window = {}

    def key_window(qi):
        if qi not in window:
            j0 = j_tile + qi * TQ_B
            ks = pl.multiple_of(jnp.clip(j0 - half_window, 0, length - KW_B), half_window)
            jq = j0 + lax.broadcasted_iota(jnp.int32, (TQ_B, 1), 0)
            jk = ks + lax.broadcasted_iota(jnp.int32, (1, KW_B), 1)
            bias = jnp.where(jnp.abs(jq - jk) <= half_window, 0.0, NEG_BIG)
            window[qi] = ks, jnp.concatenate([bias, bias], axis=0)
        return window[qi]

    def logits(qi, pair):
        ks, bias2 = key_window(qi)
        cols = slice(pair * V7X_LANES, (pair + 1) * V7X_LANES)
        q2 = q_ref[qi * TQ_B:(qi + 1) * TQ_B, cols]
        k2 = k_ref[pl.ds(ks, KW_B), cols]
        zero = jnp.zeros_like(q2)
        qs = jnp.concatenate([jnp.where(first_head, q2, zero), jnp.where(first_head, zero, q2)], axis=0)
        return lax.dot_general(qs, k2, nt, preferred_element_type=F32) + bias2

    def finish(qi, pair, s):
        ks, _ = key_window(qi)
        cols = slice(pair * V7X_LANES, (pair + 1) * V7X_LANES)
        v_aug = jnp.concatenate([v_ref[pl.ds(ks, KW_B), cols], ones], axis=1)
        m = jnp.max(s, axis=1, keepdims=True)
        p = jnp.exp2(s - m).astype(BF16)
        ol = jnp.dot(p, v_aug, preferred_element_type=F32)
        o_pair = jnp.where(first_head, ol[:TQ_B, :V7X_LANES], ol[TQ_B:, :V7X_LANES])
        l_pair = jnp.where(first_head, ol[:TQ_B, V7X_LANES:], ol[TQ_B:, V7X_LANES:])
        m_pair = jnp.where(first_head, m[:TQ_B], m[TQ_B:])
        o_ref[qi * TQ_B:(qi + 1) * TQ_B, cols] = (o_pair / l_pair).astype(BF16)
        lse_ref[qi * TQ_B:(qi + 1) * TQ_B, cols] = m_pair + jnp.log2(l_pair)

    s_cur = logits(*units[0])
    for i, unit in enumerate(units):
        s_next = logits(*units[i + 1]) if i + 1 < len(units) else None
        finish(*unit, s_cur)
        s_cur = s_next


def _attn_b(q, k, v, window, dil):
    nb, length, _ = q.shape
    half_window = (window // 2) // dil
    assert half_window * 2 + TQ_B == KW_B and length >= KW_B
    tj = min(B_ROWS_PER_STEP, length)
    ncls = min(B_ROWS_PER_STEP // tj, dil)
    qspec = pl.BlockSpec((None, tj, ncls * B_W), lambda b, r, j: (b, j, r))
    kvspec = pl.BlockSpec((None, length, ncls * B_W), lambda b, r, j: (b, 0, r))
    return pl.pallas_call(
        functools.partial(_attn_b_kernel, half_window=half_window),
        grid=(nb, dil // ncls, length // tj),
        in_specs=[qspec, kvspec, kvspec],
        out_specs=[qspec, qspec],
        out_shape=[jax.ShapeDtypeStruct((nb, length, dil * B_W), BF16),
                   jax.ShapeDtypeStruct((nb, length, dil * B_W), F32)],
        compiler_params=_cparams(("arbitrary", "arbitrary", "arbitrary")),
        name=f"attn_b_d{dil}",
    )(q, k, v)


def _load_class_major(blk_ref, stage_ref, dil, sub):
    rows = CHUNK // dil
    blk_rows = slice(sub * rows, (sub + 1) * rows)
    if dil == 1:
        return blk_ref[blk_rows, :].astype(F32)
    halves = range(B_W // V7X_LANES)
    for r in range(dil):
        for half in halves:
            col = r * B_W + half * V7X_LANES
            stage_ref[sub, half, pl.ds(r, rows, stride=dil), :] = (
                blk_ref[blk_rows, col:col + V7X_LANES].astype(F32))
    return jnp.concatenate([stage_ref[sub, half] for half in halves], axis=1)


def _mix_mlp_kernel(x_ref, ada_ref, oa_ref, ob0_ref, ob1_ref, ob2_ref, ls0_ref, ls1_ref, ls2_ref,
                    ga_ref, gb_ref, wa_ref, wb_ref, wo_ref, g_ref, w1_ref, w2_ref, xo_ref, *stage_refs):
    dils = [d for _, d in B_GROUPS]
    for sub in range(TM_MIX // CHUNK):
        tok = slice(sub * CHUNK, (sub + 1) * CHUNK)
        ya = jnp.dot(oa_ref[tok, :], wa_ref[...], preferred_element_type=F32)
        lses = [_load_class_major(r, stage_refs[2 * g], dils[g], sub)
                for g, r in enumerate((ls0_ref, ls1_ref, ls2_ref))]
        outs = [_load_class_major(r, stage_refs[2 * g + 1], dils[g], sub)
                for g, r in enumerate((ob0_ref, ob1_ref, ob2_ref))]
        top = jnp.maximum(jnp.maximum(lses[0], lses[1]), lses[2])
        es = [jnp.exp2(v - top) for v in lses]
        den = es[0] + es[1] + es[2]
        ob = (es[0] * outs[0] + es[1] * outs[1] + es[2] * outs[2]) / den
        yb = jnp.dot(ob.astype(BF16), wb_ref[...], preferred_element_type=F32)
        mixed = ga_ref[tok, :].astype(F32) * ya + gb_ref[tok, :].astype(F32) * yb
        upd = jnp.dot(mixed.astype(BF16), wo_ref[...], preferred_element_type=F32)
        xo_ref[tok, :] = x_ref[tok, :] + ada_ref[2:3, :] * upd

    x = xo_ref[...]
    h = _modulated_rms(x, g_ref[...], ada_ref[4:5, :], ada_ref[3:4, :]).astype(BF16)
    acc = jnp.zeros(x.shape, F32)
    for c in range(D_FF // FF_CHUNK):
        a = jnp.dot(h, w1_ref[:, c * FF_CHUNK:(c + 1) * FF_CHUNK], preferred_element_type=F32)
        a = jnp.square(jnp.maximum(a, 0.0)).astype(BF16)
        acc = acc + jnp.dot(a, w2_ref[c * FF_CHUNK:(c + 1) * FF_CHUNK, :], preferred_element_type=F32)
    xo_ref[...] = x + ada_ref[5:6, :] * acc


def _mix_mlp(x, ada_l, oa, obs, lss, ga, gb, wa, wb, wo, g_mlp_l, w1, w2, l):
    nb, seq, _ = x.shape
    tok = lambda w: pl.BlockSpec((None, TM_MIX, w), lambda b, s: (b, s, 0))
    wspec = lambda r, c: pl.BlockSpec((None, r, c), lambda b, s: (l, 0, 0), pipeline_mode=pl.Buffered(1))
    cls = [pl.BlockSpec((None, TM_MIX // d, d * B_W), lambda b, s: (b, s, 0)) for _, d in B_GROUPS]
    return pl.pallas_call(
        _mix_mlp_kernel,
        grid=(nb, seq // TM_MIX),
        in_specs=[tok(D_MODEL), pl.BlockSpec((None, 6, D_MODEL), lambda b, s: (b, 0, 0)), tok(A_Q_W)]
                 + cls + cls + [tok(D_MODEL)] * 2
                 + [wspec(A_Q_W, D_MODEL), wspec(B_W, D_MODEL), wspec(D_MODEL, D_MODEL),
                    pl.BlockSpec((None, 1, D_MODEL), lambda b, s: (l, 0, 0)),
                    wspec(D_MODEL, D_FF), wspec(D_FF, D_MODEL)],
        out_specs=tok(D_MODEL),
        out_shape=jax.ShapeDtypeStruct(x.shape, F32),
        scratch_shapes=[pltpu.VMEM((TM_MIX // CHUNK, B_W // V7X_LANES, CHUNK, V7X_LANES), F32)] * (2 * N_B),
        compiler_params=_cparams(("arbitrary", "arbitrary")),
        name="mix_mlp",
    )(x, ada_l, oa, *obs, *lss, ga, gb, wa, wb, wo, g_mlp_l, w1, w2)


def _rope_tables(seq):
    pos = jnp.arange(seq)
    rows = seq // GRID_W
    row_idx = jnp.broadcast_to(jnp.arange(rows)[:, None], (rows, GRID_W)).reshape(-1)
    col_idx = jnp.broadcast_to(jnp.arange(GRID_W)[None, :], (rows, GRID_W)).reshape(-1)

    def cos_sin(p, dim):
        inv = ROPE_THETA ** (-jnp.arange(0, dim, 2, dtype=F32) / dim)
        ang = inv[:, None] * p.astype(F32)[None, :]
        return jnp.cos(ang), jnp.sin(ang)

    rc, rs = cos_sin(row_idx, HEAD_DIM // 2)
    cc, cs = cos_sin(col_idx, HEAD_DIM // 2)
    sc, ss = cos_sin(pos, HEAD_DIM)
    return jnp.concatenate([rc, rc, cc, cc, -rs, rs, -cs, cs, sc, sc, -ss, ss], axis=0)


def _gain_columns(q_norm_a, k_norm_a, q_norm_b, k_norm_b):
    scale = HEAD_DIM ** -0.5 * LOG2_E

    def pair(gain, perm, mult):
        return [gain * mult, gain[:, perm] * mult]

    cols = pair(q_norm_a, PERM_AXIAL, scale) + pair(k_norm_a, PERM_AXIAL, 1.0)
    for g in range(N_B):
        cols += pair(q_norm_b[:, g], PERM_SEQ, scale) + pair(k_norm_b[:, g], PERM_SEQ, 1.0)
    cols = jnp.concatenate(cols, axis=1)
    return jnp.broadcast_to(cols[:, :, None], cols.shape + (V7X_LANES,))


def _split_w_in(w_in):
    sizes = [A_Q_W, A_KV_W, A_KV_W] + [B_W] * (3 * N_B) + [D_MODEL, D_MODEL]
    offs = np.concatenate([[0], np.cumsum(sizes)])
    col = lambda i: w_in[:, :, offs[i]:offs[i + 1]]
    names = ["qa", "ka", "va"] + [f"{t}b{g}" for g in range(N_B) for t in "qkv"] + ["ga", "gb"]
    parts = {n: col(i) for i, n in enumerate(names)}
    wn = jnp.concatenate([parts["vb0"], parts["vb1"], parts["vb2"], parts["ga"], parts["gb"]], axis=-1)
    wt = jnp.concatenate([parts[n] for n, _ in T_SEGS], axis=-1)
    return wn.astype(BF16), jnp.swapaxes(wt, 1, 2).astype(BF16)


def kernel(x, c, w_ada, b_ada, g_mix, g_mlp, w_in, q_norm_a, k_norm_a, q_norm_b, k_norm_b,
           w_branch_a, w_branch_b, w_out, w_ff1, w_ff2):
    nb, seq, _ = x.shape
    depth = w_in.shape[0]
    assert seq % GRID_W == 0 and seq % TM_MIX == 0 and seq % TM_IN == 0

    ada = _ada_all(c, w_ada, b_ada).reshape(depth, nb, 6, D_MODEL)
    rope = _rope_tables(seq)
    gains = _gain_columns(q_norm_a, k_norm_a, q_norm_b, k_norm_b)
    wn, wt = _split_w_in(w_in)
    wa, wb, wo = w_branch_a.astype(BF16), w_branch_b.astype(BF16), w_out.astype(BF16)
    w1, w2 = w_ff1.astype(BF16), w_ff2.astype(BF16)
    g_mix = g_mix.reshape(depth, 1, D_MODEL)
    g_mlp = g_mlp.reshape(depth, 1, D_MODEL)

    for l in range(depth):
        (qat, ka, vat, qb0, kb0, vb0, qb1, kb1, vb1, qb2, kb2, vb2, ga, gb) = _in_proj(
            x, ada[l], g_mix, wn, wt, rope, gains, l)
        oa = _attn_a(qat, ka, vat)
        obs, lss = [], []
        for (window, dil), q, k, v in zip(B_GROUPS, (qb0, qb1, qb2), (kb0, kb1, kb2), (vb0, vb1, vb2)):
            o, lse = _attn_b(q, k, v, window, dil)
            obs.append(o)
            lss.append(lse)
        x = _mix_mlp(x, ada[l], oa, obs, lss, ga, gb, wa, wb, wo, g_mlp, w1, w2, l)
    return x
```

```python
import functools

import jax
import jax.numpy as jnp
import numpy as np
from jax import lax
from jax.experimental import pallas as pl
from jax.experimental.pallas import tpu as pltpu

D_MODEL = 1024
HEAD_DIM = 64
A_Q_HEADS = 8
A_KV_HEADS = 2
B_GROUPS = ((128, 1), (512, 4), (2048, 16))
B_HEADS = 4
N_B = len(B_GROUPS)
D_FF = 4 * D_MODEL
GRID_W = 64
ROPE_THETA = 10000.0
EPS = 1e-6

A_Q_W = A_Q_HEADS * HEAD_DIM
A_KV_W = A_KV_HEADS * HEAD_DIM
B_W = B_HEADS * HEAD_DIM

V7X_LANES = 128
V7X_VMEM_BYTES = 64 * 1024 * 1024
VMEM_LIMIT = 52 * 1024 * 1024

BF16 = jnp.bfloat16
F32 = jnp.float32

CHUNK = 256
TM_IN = 512
TQ_A = 1024
A_PAIRS_PER_BODY = 3
TM_MIX = 512
FF_CHUNK = 1024
B_ROWS_PER_STEP = 1024
TQ_B = 128
KW_B = 256
NEG_BIG = -1e30
LOG2_E = 1.4426950408889634

PERM_AXIAL = np.concatenate([np.arange(16, 32), np.arange(0, 16), np.arange(48, 64), np.arange(32, 48)])
PERM_SEQ = np.concatenate([np.arange(32, 64), np.arange(0, 32)])

T_SEGS = (("qa", A_Q_W), ("ka", A_KV_W), ("va", A_KV_W),
          ("qb0", B_W), ("kb0", B_W), ("qb1", B_W), ("kb1", B_W), ("qb2", B_W), ("kb2", B_W))
T_ROWS = sum(r for _, r in T_SEGS)
IN_W = A_Q_W + 2 * A_KV_W + 3 * N_B * B_W + 2 * D_MODEL
VB_COL0 = A_Q_W + 2 * A_KV_W + 2 * B_W
GATE_COL0 = A_Q_W + 2 * A_KV_W + 3 * N_B * B_W
N_TABS = 16


def _cparams(sem):
    return pltpu.CompilerParams(dimension_semantics=sem, vmem_limit_bytes=VMEM_LIMIT)


def _ada_kernel(c_ref, w_ref, b_ref, o_ref):
    c = c_ref[...]
    c_act = c * (1.0 / (1.0 + jnp.exp(-c)))
    o_ref[...] = jnp.dot(c_act, w_ref[...], preferred_element_type=F32,
                         precision=lax.Precision.HIGHEST) + b_ref[...]


def _ada_all(c, w_ada, b_ada):
    depth = w_ada.shape[0]
    nb = c.shape[0]
    return pl.pallas_call(
        _ada_kernel,
        grid=(depth, 6),
        in_specs=[
            pl.BlockSpec((nb, D_MODEL), lambda l, j: (0, 0)),
            pl.BlockSpec((None, D_MODEL, D_MODEL), lambda l, j: (l, 0, j)),
            pl.BlockSpec((None, 1, D_MODEL), lambda l, j: (l, 0, j)),
        ],
        out_specs=pl.BlockSpec((None, nb, D_MODEL), lambda l, j: (l, 0, j)),
        out_shape=jax.ShapeDtypeStruct((depth, nb, 6 * D_MODEL), F32),
        compiler_params=_cparams(("arbitrary", "arbitrary")),
        name="ada",
    )(c, w_ada, b_ada.reshape(depth, 1, 6 * D_MODEL))


def _modulated_rms(x, g, scale, shift):
    ms = jnp.mean(x * x, axis=-1, keepdims=True)
    return (x * lax.rsqrt(ms + EPS)) * g * (1.0 + scale) + shift


def _norm_rope_t(q, cos_t, sin_t, perm_blocks):
    ss = jnp.sum(q * q, axis=0, keepdims=True)
    r = lax.rsqrt(ss * (1.0 / HEAD_DIM) + EPS)
    qp = jnp.concatenate([q[a:b] for a, b in perm_blocks], axis=0)
    return (q * cos_t + qp * sin_t) * r


AXIAL_BLOCKS = ((16, 32), (0, 16), (48, 64), (32, 48))
SEQ_BLOCKS = ((32, 64), (0, 32))


def _store_class_major(out_ref, stage_ref, tile, dil, sub):
    rows = tile.shape[0] // dil
    out_rows = slice(sub * rows, (sub + 1) * rows)
    if dil == 1:
        out_ref[out_rows, :] = tile.astype(BF16)
        return
    for half in range(B_W // V7X_LANES):
        stage_ref[sub, half] = tile[:, half * V7X_LANES:(half + 1) * V7X_LANES]
    for r in range(dil):
        for half in range(B_W // V7X_LANES):
            col = r * B_W + half * V7X_LANES
            out_ref[out_rows, col:col + V7X_LANES] = (
                stage_ref[sub, half, pl.ds(r, rows, stride=dil), :].astype(BF16))


def _in_kernel(x_ref, ada_ref, g_ref, wn_ref, wt_ref, rope_ref, gain_ref,
               qat_ref, ka_ref, vat_ref,
               qb0_ref, kb0_ref, vb0_ref, qb1_ref, kb1_ref, vb1_ref, qb2_ref, kb2_ref, vb2_ref,
               ga_ref, gb_ref, stage_q_ref, stage_k_ref, stage_v_ref):
    for sub in range(TM_IN // CHUNK):
        _in_subtile(sub, x_ref, ada_ref, g_ref, wn_ref, wt_ref, rope_ref, gain_ref, qat_ref, ka_ref, vat_ref,
                    (qb0_ref, qb1_ref, qb2_ref), (kb0_ref, kb1_ref, kb2_ref), (vb0_ref, vb1_ref, vb2_ref),
                    ga_ref, gb_ref, stage_q_ref, stage_k_ref, stage_v_ref)


def _in_subtile(sub, x_ref, ada_ref, g_ref, wn_ref, wt_ref, rope_ref, gain_ref, qat_ref, ka_ref, vat_ref,
                qb_refs, kb_refs, vb_refs, ga_ref, gb_ref, stage_q_ref, stage_k_ref, stage_v_ref):
    tok = slice(sub * CHUNK, (sub + 1) * CHUNK)
    x = x_ref[tok, :]
    h = _modulated_rms(x, g_ref[...], ada_ref[1:2, :], ada_ref[0:1, :]).astype(BF16)

    nt = (((1,), (1,)), ((), ()))

    def proj_t(row0, rows):
        return lax.dot_general(wt_ref[row0:row0 + rows, :], h, nt, preferred_element_type=F32)

    def gained(tab, base_row):
        gain = gain_ref[tab * HEAD_DIM:(tab + 1) * HEAD_DIM, :]
        gain = jnp.concatenate([gain] * (CHUNK // V7X_LANES), axis=1)
        return rope_ref[base_row * HEAD_DIM:(base_row + 1) * HEAD_DIM, tok] * gain

    def heads_t(rt, n_heads, tab0, blocks):
        base_row = 0 if blocks is AXIAL_BLOCKS else 2
        cos_t = gained(tab0, base_row)
        sin_t = gained(tab0 + 1, base_row + 1)
        return [_norm_rope_t(rt[i * HEAD_DIM:(i + 1) * HEAD_DIM], cos_t, sin_t, blocks)
                for i in range(n_heads)]

    row = 0
    rt = proj_t(row, A_Q_W)
    qa = heads_t(rt, A_Q_HEADS, 0, AXIAL_BLOCKS)
    for i in range(A_Q_HEADS):
        qat_ref[sub, i * HEAD_DIM:(i + 1) * HEAD_DIM, :] = qa[i].astype(BF16)
    row += A_Q_W

    rt = proj_t(row, 2 * A_KV_W)
    ka = heads_t(rt[:A_KV_W], A_KV_HEADS, 2, AXIAL_BLOCKS)
    ka_ref[tok, :] = jnp.concatenate(ka, axis=0).T.astype(BF16)
    vat_ref[sub] = rt[A_KV_W:].astype(BF16)
    row += 2 * A_KV_W

    for g in range(N_B):
        rt = proj_t(row, 2 * B_W)
        qb = heads_t(rt[:B_W], B_HEADS, 4 + 4 * g, SEQ_BLOCKS)
        kb = heads_t(rt[B_W:], B_HEADS, 6 + 4 * g, SEQ_BLOCKS)
        _store_class_major(qb_refs[g], stage_q_ref, jnp.concatenate(qb, axis=0).T, B_GROUPS[g][1], sub)
        _store_class_major(kb_refs[g], stage_k_ref, jnp.concatenate(kb, axis=0).T, B_GROUPS[g][1], sub)
        row += 2 * B_W

    for i, gate_ref in enumerate((ga_ref, gb_ref)):
        lo = GATE_COL0 + i * D_MODEL
        z = jnp.dot(h, wn_ref[:, lo:lo + D_MODEL], preferred_element_type=F32)
        gate_ref[tok, :] = (1.0 / (1.0 + jnp.exp(-z))).astype(BF16)
    for g in reversed(range(N_B)):
        lo = VB_COL0 + 3 * B_W * g
        v = jnp.dot(h, wn_ref[:, lo:lo + B_W], preferred_element_type=F32)
        _store_class_major(vb_refs[g], stage_v_ref, v, B_GROUPS[g][1], sub)


def _in_proj(x, ada_l, g_mix_l, wn, wt, rope, gains, l):
    nb, seq, _ = x.shape
    ns = seq // TM_IN
    nsub = TM_IN // CHUNK
    tok = lambda w: pl.BlockSpec((None, TM_IN, w), lambda s, b: (b, s, 0))
    tchunk = lambda r: pl.BlockSpec((None, nsub, r, CHUNK), lambda s, b: (b, s, 0, 0))
    nat = lambda w: jax.ShapeDtypeStruct((nb, seq, w), BF16)
    out_specs = [tchunk(A_Q_W), tok(A_KV_W), tchunk(A_KV_W)]
    out_shape = [jax.ShapeDtypeStruct((nb, seq // CHUNK, A_Q_W, CHUNK), BF16), nat(A_KV_W),
                 jax.ShapeDtypeStruct((nb, seq // CHUNK, A_KV_W, CHUNK), BF16)]
    for _, dil in B_GROUPS:
        out_specs += [pl.BlockSpec((None, TM_IN // dil, dil * B_W), lambda s, b: (b, s, 0))] * 3
        out_shape += [jax.ShapeDtypeStruct((nb, seq // dil, dil * B_W), BF16)] * 3
    out_specs += [tok(D_MODEL)] * 2
    out_shape += [nat(D_MODEL)] * 2
    return pl.pallas_call(
        _in_kernel,
        grid=(ns, nb),
        in_specs=[
            tok(D_MODEL),
            pl.BlockSpec((None, 6, D_MODEL), lambda s, b: (b, 0, 0)),
            pl.BlockSpec((None, 1, D_MODEL), lambda s, b: (l, 0, 0)),
            pl.BlockSpec((None, D_MODEL, IN_W), lambda s, b: (l, 0, 0), pipeline_mode=pl.Buffered(1)),
            pl.BlockSpec((None, T_ROWS, D_MODEL), lambda s, b: (l, 0, 0), pipeline_mode=pl.Buffered(1)),
            pl.BlockSpec((4 * HEAD_DIM, TM_IN), lambda s, b: (0, s)),
            pl.BlockSpec((None, N_TABS * HEAD_DIM, V7X_LANES), lambda s, b: (l, 0, 0)),
        ],
        out_specs=out_specs,
        out_shape=out_shape,
        scratch_shapes=[pltpu.VMEM((nsub, B_W // V7X_LANES, CHUNK, V7X_LANES), F32)] * 3,
        compiler_params=_cparams(("arbitrary", "arbitrary")),
        name="in_proj",
    )(x, ada_l, g_mix_l, wn, wt, rope, gains)


A_REP = A_Q_HEADS // A_KV_HEADS
ONES_ROWS = 16


def _attn_a_kernel(qt_ref, k_ref, vt_ref, o_ref, qpad_ref, acc_ref, m_ref, s_ref, smax_ref):
    for sub in range(qt_ref.shape[0]):
        _attn_a_tile(qt_ref.at[sub], k_ref, vt_ref, o_ref.at[sub * CHUNK:(sub + 1) * CHUNK],
                     qpad_ref.at[sub], acc_ref.at[sub], m_ref.at[sub], s_ref, smax_ref)


def _attn_a_tile(qt_ref, k_ref, vt_ref, o_ref, qpad_ref, acc_ref, m_ref, s_ref, smax_ref):
    n_chunks, _, tk = vt_ref.shape
    tq = qt_ref.shape[1]
    qpad_ref[...] = jnp.zeros(qpad_ref.shape, BF16)
    for g in range(A_KV_HEADS):
        for hh in range(A_REP):
            h = g * A_REP + hh
            qpad_ref[g, g * HEAD_DIM:(g + 1) * HEAD_DIM, hh * tq:(hh + 1) * tq] = (
                qt_ref[h * HEAD_DIM:(h + 1) * HEAD_DIM, :])
    m_ref[...] = jnp.full(m_ref.shape, NEG_BIG, F32)
    acc_ref[...] = jnp.zeros(acc_ref.shape, F32)
    ones = jnp.ones((ONES_ROWS, tk), BF16)

    def scores(c, slot, g, hh):
        lanes = slice(hh * tq, (hh + 1) * tq)
        kc = k_ref[pl.ds(pl.multiple_of(c * tk, tk), tk), :]
        s = jnp.dot(kc, qpad_ref[g, :, lanes], preferred_element_type=F32)
        s_ref[slot, g, :, lanes] = s
        smax_ref[slot, g, :, lanes] = jnp.max(s, axis=0, keepdims=True)

    def consume(c, slot, g, hh):
        lanes = slice(hh * tq, (hh + 1) * tq)
        m_prev = m_ref[g, :, lanes]
        m_new = jnp.maximum(m_prev, smax_ref[slot, g, :, lanes])
        alpha = jnp.exp2(m_prev - m_new)
        p = jnp.exp2(s_ref[slot, g, :, lanes] - m_new).astype(BF16)
        v_aug = jnp.concatenate([vt_ref[c, g * HEAD_DIM:(g + 1) * HEAD_DIM, :], ones], axis=0)
        acc_ref[g, :, lanes] = acc_ref[g, :, lanes] * alpha + jnp.dot(v_aug, p, preferred_element_type=F32)
        m_ref[g, :, lanes] = m_new

    for g in range(A_KV_HEADS):
        for hh in range(A_REP):
            scores(0, 0, g, hh)

    def chunk_pair(c, last):
        for slot in range(2):
            for g in range(A_KV_HEADS):
                for hh in range(A_REP):
                    if not (last and slot == 1):
                        scores(c + slot + 1, 1 - slot, g, hh)
                    consume(c + slot, slot, g, hh)

    body_chunks = 2 * A_PAIRS_PER_BODY

    def loop_body(i, carry):
        for j in range(A_PAIRS_PER_BODY):
            chunk_pair(body_chunks * i + 2 * j, False)
        return carry

    n_loop = (n_chunks - 2) // body_chunks
    lax.fori_loop(0, n_loop, loop_body, 0)
    for c in range(n_loop * body_chunks, n_chunks, 2):
        chunk_pair(c, c == n_chunks - 2)
    for j in range(A_Q_W // V7X_LANES):
        halves = []
        for h in (2 * j, 2 * j + 1):
            g, hh = divmod(h, A_REP)
            cols = slice(hh * tq, (hh + 1) * tq)
            halves.append(acc_ref[g, :HEAD_DIM, cols] / acc_ref[g, HEAD_DIM:HEAD_DIM + 1, cols])
        blk = jnp.concatenate(halves, axis=0)
        o_ref[:, j * V7X_LANES:(j + 1) * V7X_LANES] = blk.T.astype(BF16)


def _attn_a(qat, ka, vat):
    nb, ns, _, chunk = qat.shape
    seq = ka.shape[1]
    nq = TQ_A // chunk
    wide = A_REP * chunk
    return pl.pallas_call(
        _attn_a_kernel,
        grid=(nb, seq // TQ_A),
        in_specs=[
            pl.BlockSpec((None, nq, A_Q_W, chunk), lambda b, s: (b, s, 0, 0)),
            pl.BlockSpec((None, seq, A_KV_W), lambda b, s: (b, 0, 0)),
            pl.BlockSpec((None, ns, A_KV_W, chunk), lambda b, s: (b, 0, 0, 0)),
        ],
        out_specs=pl.BlockSpec((None, TQ_A, A_Q_W), lambda b, s: (b, s, 0)),
        out_shape=jax.ShapeDtypeStruct((nb, seq, A_Q_W), BF16),
        scratch_shapes=[pltpu.VMEM((nq, A_KV_HEADS, A_KV_W, wide), BF16),
                        pltpu.VMEM((nq, A_KV_HEADS, HEAD_DIM + ONES_ROWS, wide), F32),
                        pltpu.VMEM((nq, A_KV_HEADS, 1, wide), F32),
                        pltpu.VMEM((2, A_KV_HEADS, chunk, wide), F32),
                        pltpu.VMEM((2, A_KV_HEADS, 1, wide), F32)],
        compiler_params=_cparams(("arbitrary", "arbitrary")),
        name="attn_a",
    )(qat, ka, vat)


def _attn_b_kernel(q_ref, k_ref, v_ref, o_ref, lse_ref, *, half_window):
    tj = q_ref.shape[0]
    length = k_ref.shape[0]
    j_tile = pl.program_id(2) * tj
    lane = lax.broadcasted_iota(jnp.int32, (1, V7X_LANES), 1)
    first_head = lane < HEAD_DIM
    nt = (((1,), (1,)), ((), ()))
    ones = jnp.ones((KW_B, V7X_LANES), BF16)
    n_pairs = q_ref.shape[1] // V7X_LANES
    units = [(qi, pair) for qi in range(tj // TQ_B) for pair in range(n_pairs)]
    window = {}

    def key_window(qi):
        if qi not in window:
            j0 = j_tile + qi * TQ_B
            ks = pl.multiple_of(jnp.clip(j0 - half_window, 0, length - KW_B), half_window)
            jq = j0 + lax.broadcasted_iota(jnp.int32, (TQ_B, 1), 0)
            jk = ks + lax.broadcasted_iota(jnp.int32, (1, KW_B), 1)
            bias = jnp.where(jnp.abs(jq - jk) <= half_window, 0.0, NEG_BIG)
            window[qi] = ks, jnp.concatenate([bias, bias], axis=0)
        return window[qi]

    def logits(qi, pair):
        ks, bias2 = key_window(qi)
        cols = slice(pair * V7X_LANES, (pair + 1) * V7X_LANES)
        q2 = q_ref[qi * TQ_B:(qi + 1) * TQ_B, cols]
        k2 = k_ref[pl.ds(ks, KW_B), cols]
        zero = jnp.zeros_like(q2)
        qs = jnp.concatenate([jnp.where(first_head, q2, zero), jnp.where(first_head, zero, q2)], axis=0)
        return lax.dot_general(qs, k2, nt, preferred_element_type=F32) + bias2

    def finish(qi, pair, s):
        ks, _ = key_window(qi)
        cols = slice(pair * V7X_LANES, (pair + 1) * V7X_LANES)
        v_aug = jnp.concatenate([v_ref[pl.ds(ks, KW_B), cols], ones], axis=1)
        m = jnp.max(s, axis=1, keepdims=True)
        p = jnp.exp2(s - m).astype(BF16)
        ol = jnp.dot(p, v_aug, preferred_element_type=F32)
        o_pair = jnp.where(first_head, ol[:TQ_B, :V7X_LANES], ol[TQ_B:, :V7X_LANES])
        l_pair = jnp.where(first_head, ol[:TQ_B, V7X_LANES:], ol[TQ_B:, V7X_LANES:])
        m_pair = jnp.where(first_head, m[:TQ_B], m[TQ_B:])
        o_ref[qi * TQ_B:(qi + 1) * TQ_B, cols] = (o_pair / l_pair).astype(BF16)
        lse_ref[qi * TQ_B:(qi + 1) * TQ_B, cols] = m_pair + jnp.log2(l_pair)

    s_cur = logits(*units[0])
    for i, unit in enumerate(units):
        s_next = logits(*units[i + 1]) if i + 1 < len(units) else None
        finish(*unit, s_cur)
        s_cur = s_next


def _attn_b(q, k, v, window, dil):
    nb, length, _ = q.shape
    half_window = (window // 2) // dil
    assert half_window * 2 + TQ_B == KW_B and length >= KW_B
    tj = min(B_ROWS_PER_STEP, length)
    ncls = min(B_ROWS_PER_STEP // tj, dil)
    qspec = pl.BlockSpec((None, tj, ncls * B_W), lambda b, r, j: (b, j, r))
    kvspec = pl.BlockSpec((None, length, ncls * B_W), lambda b, r, j: (b, 0, r))
    return pl.pallas_call(
        functools.partial(_attn_b_kernel, half_window=half_window),
        grid=(nb, dil // ncls, length // tj),
        in_specs=[qspec, kvspec, kvspec],
        out_specs=[qspec, qspec],
        out_shape=[jax.ShapeDtypeStruct((nb, length, dil * B_W), BF16),
                   jax.ShapeDtypeStruct((nb, length, dil * B_W), F32)],
        compiler_params=_cparams(("arbitrary", "arbitrary", "arbitrary")),
        name=f"attn_b_d{dil}",
    )(q, k, v)


def _load_class_major(blk_ref, stage_ref, dil, sub):
    rows = CHUNK // dil
    blk_rows = slice(sub * rows, (sub + 1) * rows)
    if dil == 1:
        return blk_ref[blk_rows, :].astype(F32)
    halves = range(B_W // V7X_LANES)
    for r in range(dil):
        for half in halves:
            col = r * B_W + half * V7X_LANES
            stage_ref[sub, half, pl.ds(r, rows, stride=dil), :] = (
                blk_ref[blk_rows, col:col + V7X_LANES].astype(F32))
    return jnp.concatenate([stage_ref[sub, half] for half in halves], axis=1)


def _mix_mlp_kernel(x_ref, ada_ref, oa_ref, ob0_ref, ob1_ref, ob2_ref, ls0_ref, ls1_ref, ls2_ref,
                    ga_ref, gb_ref, wa_ref, wb_ref, wo_ref, g_ref, w1_ref, w2_ref, xo_ref, *stage_refs):
    dils = [d for _, d in B_GROUPS]
    for sub in range(TM_MIX // CHUNK):
        tok = slice(sub * CHUNK, (sub + 1) * CHUNK)
        ya = jnp.dot(oa_ref[tok, :], wa_ref[...], preferred_element_type=F32)
        lses = [_load_class_major(r, stage_refs[2 * g], dils[g], sub)
                for g, r in enumerate((ls0_ref, ls1_ref, ls2_ref))]
        outs = [_load_class_major(r, stage_refs[2 * g + 1], dils[g], sub)
                for g, r in enumerate((ob0_ref, ob1_ref, ob2_ref))]
        top = jnp.maximum(jnp.maximum(lses[0], lses[1]), lses[2])
        es = [jnp.exp2(v - top) for v in lses]
        den = es[0] + es[1] + es[2]
        ob = (es[0] * outs[0] + es[1] * outs[1] + es[2] * outs[2]) / den
        yb = jnp.dot(ob.astype(BF16), wb_ref[...], preferred_element_type=F32)
        mixed = ga_ref[tok, :].astype(F32) * ya + gb_ref[tok, :].astype(F32) * yb
        upd = jnp.dot(mixed.astype(BF16), wo_ref[...], preferred_element_type=F32)
        xo_ref[tok, :] = x_ref[tok, :] + ada_ref[2:3, :] * upd

    x = xo_ref[...]
    h = _modulated_rms(x, g_ref[...], ada_ref[4:5, :], ada_ref[3:4, :]).astype(BF16)
    acc = jnp.zeros(x.shape, F32)
    for c in range(D_FF // FF_CHUNK):
        a = jnp.dot(h, w1_ref[:, c * FF_CHUNK:(c + 1) * FF_CHUNK], preferred_element_type=F32)
        a = jnp.square(jnp.maximum(a, 0.0)).astype(BF16)
        acc = acc + jnp.dot(a, w2_ref[c * FF_CHUNK:(c + 1) * FF_CHUNK, :], preferred_element_type=F32)
    xo_ref[...] = x + ada_ref[5:6, :] * acc


def _mix_mlp(x, ada_l, oa, obs, lss, ga, gb, wa, wb, wo, g_mlp_l, w1, w2, l):
    nb, seq, _ = x.shape
    tok = lambda w: pl.BlockSpec((None, TM_MIX, w), lambda b, s: (b, s, 0))
    wspec = lambda r, c: pl.BlockSpec((None, r, c), lambda b, s: (l, 0, 0), pipeline_mode=pl.Buffered(1))
    cls = [pl.BlockSpec((None, TM_MIX // d, d * B_W), lambda b, s: (b, s, 0)) for _, d in B_GROUPS]
    return pl.pallas_call(
        _mix_mlp_kernel,
        grid=(nb, seq // TM_MIX),
        in_specs=[tok(D_MODEL), pl.BlockSpec((None, 6, D_MODEL), lambda b, s: (b, 0, 0)), tok(A_Q_W)]
                 + cls + cls + [tok(D_MODEL)] * 2
                 + [wspec(A_Q_W, D_MODEL), wspec(B_W, D_MODEL), wspec(D_MODEL, D_MODEL),
                    pl.BlockSpec((None, 1, D_MODEL), lambda b, s: (l, 0, 0)),
                    wspec(D_MODEL, D_FF), wspec(D_FF, D_MODEL)],
        out_specs=tok(D_MODEL),
        out_shape=jax.ShapeDtypeStruct(x.shape, F32),
        scratch_shapes=[pltpu.VMEM((TM_MIX // CHUNK, B_W // V7X_LANES, CHUNK, V7X_LANES), F32)] * (2 * N_B),
        compiler_params=_cparams(("arbitrary", "arbitrary")),
        name="mix_mlp",
    )(x, ada_l, oa, *obs, *lss, ga, gb, wa, wb, wo, g_mlp_l, w1, w2)


def _rope_tables(seq):
    pos = jnp.arange(seq)
    rows = seq // GRID_W
    row_idx = jnp.broadcast_to(jnp.arange(rows)[:, None], (rows, GRID_W)).reshape(-1)
    col_idx = jnp.broadcast_to(jnp.arange(GRID_W)[None, :], (rows, GRID_W)).reshape(-1)

    def cos_sin(p, dim):
        inv = ROPE_THETA ** (-jnp.arange(0, dim, 2, dtype=F32) / dim)
        ang = inv[:, None] * p.astype(F32)[None, :]
        return jnp.cos(ang), jnp.sin(ang)

    rc, rs = cos_sin(row_idx, HEAD_DIM // 2)
    cc, cs = cos_sin(col_idx, HEAD_DIM // 2)
    sc, ss = cos_sin(pos, HEAD_DIM)
    return jnp.concatenate([rc, rc, cc, cc, -rs, rs, -cs, cs, sc, sc, -ss, ss], axis=0)


def _gain_columns(q_norm_a, k_norm_a, q_norm_b, k_norm_b):
    scale = HEAD_DIM ** -0.5 * LOG2_E

    def pair(gain, perm, mult):
        return [gain * mult, gain[:, perm] * mult]

    cols = pair(q_norm_a, PERM_AXIAL, scale) + pair(k_norm_a, PERM_AXIAL, 1.0)
    for g in range(N_B):
        cols += pair(q_norm_b[:, g], PERM_SEQ, scale) + pair(k_norm_b[:, g], PERM_SEQ, 1.0)
    cols = jnp.concatenate(cols, axis=1)
    return jnp.broadcast_to(cols[:, :, None], cols.shape + (V7X_LANES,))


def _split_w_in(w_in):
    sizes = [A_Q_W, A_KV_W, A_KV_W] + [B_W] * (3 * N_B) + [D_MODEL, D_MODEL]
    offs = np.concatenate([[0], np.cumsum(sizes)])
    names = ["qa", "ka", "va"] + [f"{t}b{g}" for g in range(N_B) for t in "qkv"] + ["ga", "gb"]
    w_bf = w_in.astype(BF16)
    parts = {n: w_bf[:, :, offs[i]:offs[i + 1]] for i, n in enumerate(names)}
    wt = jnp.concatenate([parts[n] for n, _ in T_SEGS], axis=-1)
    return w_bf, jnp.swapaxes(wt, 1, 2)


def kernel(x, c, w_ada, b_ada, g_mix, g_mlp, w_in, q_norm_a, k_norm_a, q_norm_b, k_norm_b,
           w_branch_a, w_branch_b, w_out, w_ff1, w_ff2):
    nb, seq, _ = x.shape
    depth = w_in.shape[0]
    assert seq % GRID_W == 0 and seq % TM_MIX == 0 and seq % TM_IN == 0

    ada = _ada_all(c, w_ada, b_ada).reshape(depth, nb, 6, D_MODEL)
    rope = _rope_tables(seq)
    gains = _gain_columns(q_norm_a, k_norm_a, q_norm_b, k_norm_b)
    wn, wt = _split_w_in(w_in)
    wa, wb, wo = w_branch_a.astype(BF16), w_branch_b.astype(BF16), w_out.astype(BF16)
    w1, w2 = w_ff1.astype(BF16), w_ff2.astype(BF16)
    g_mix = g_mix.reshape(depth, 1, D_MODEL)
    g_mlp = g_mlp.reshape(depth, 1, D_MODEL)

    for l in range(depth):
        (qat, ka, vat, qb0, kb0, vb0, qb1, kb1, vb1, qb2, kb2, vb2, ga, gb) = _in_proj(
            x, ada[l], g_mix, wn, wt, rope, gains, l)
        oa = _attn_a(qat, ka, vat)
        obs, lss = [], []
        for (window, dil), q, k, v in zip(B_GROUPS, (qb0, qb1, qb2), (kb0, kb1, kb2), (vb0, vb1, vb2)):
            o, lse = _attn_b(q, k, v, window, dil)
            obs.append(o)
            lss.append(lse)
        x = _mix_mlp(x, ada[l], oa, obs, lss, ga, gb, wa, wb, wo, g_mlp, w1, w2, l)
    return x
```

```python
import functools

import jax
import jax.numpy as jnp
import numpy as np
from jax import lax
from jax.experimental import pallas as pl
from jax.experimental.pallas import tpu as pltpu

D_MODEL = 1024
HEAD_DIM = 64
A_Q_HEADS = 8
A_KV_HEADS = 2
B_GROUPS = ((128, 1), (512, 4), (2048, 16))
B_HEADS = 4
N_B = len(B_GROUPS)
D_FF = 4 * D_MODEL
GRID_W = 64
ROPE_THETA = 10000.0
EPS = 1e-6

A_Q_W = A_Q_HEADS * HEAD_DIM
A_KV_W = A_KV_HEADS * HEAD_DIM
B_W = B_HEADS * HEAD_DIM

V7X_LANES = 128
V7X_VMEM_BYTES = 64 * 1024 * 1024
VMEM_LIMIT = V7X_VMEM_BYTES * 13 // 16

BF16 = jnp.bfloat16
F32 = jnp.float32

CHUNK = 256
TM_IN = 512
TQ_A = 1024
A_PAIRS_PER_BODY = 3
TM_MIX = 512
FF_CHUNK = 1024
B_ROWS_PER_STEP = 4096
TQ_B = 128
KW_B = 256
NEG_BIG = -1e30
LOG2_E = 1.4426950408889634

PERM_AXIAL = np.concatenate([np.arange(16, 32), np.arange(0, 16), np.arange(48, 64), np.arange(32, 48)])
PERM_SEQ = np.concatenate([np.arange(32, 64), np.arange(0, 32)])

T_SEGS = (("qa", A_Q_W), ("ka", A_KV_W), ("va", A_KV_W),
          ("qb0", B_W), ("kb0", B_W), ("qb1", B_W), ("kb1", B_W), ("qb2", B_W), ("kb2", B_W))
T_ROWS = sum(r for _, r in T_SEGS)
N_COLS = N_B * B_W + 2 * D_MODEL
N_TABS = 16


def _cparams(sem):
    return pltpu.CompilerParams(dimension_semantics=sem, vmem_limit_bytes=VMEM_LIMIT)


def _ada_kernel(c_ref, w_ref, b_ref, o_ref):
    c = c_ref[...]
    c_act = c * (1.0 / (1.0 + jnp.exp(-c)))
    o_ref[...] = jnp.dot(c_act, w_ref[...], preferred_element_type=F32,
                         precision=lax.Precision.HIGHEST) + b_ref[...]


def _ada_all(c, w_ada, b_ada):
    depth = w_ada.shape[0]
    nb = c.shape[0]
    return pl.pallas_call(
        _ada_kernel,
        grid=(depth, 6),
        in_specs=[
            pl.BlockSpec((nb, D_MODEL), lambda l, j: (0, 0)),
            pl.BlockSpec((None, D_MODEL, D_MODEL), lambda l, j: (l, 0, j)),
            pl.BlockSpec((None, 1, D_MODEL), lambda l, j: (l, 0, j)),
        ],
        out_specs=pl.BlockSpec((None, nb, D_MODEL), lambda l, j: (l, 0, j)),
        out_shape=jax.ShapeDtypeStruct((depth, nb, 6 * D_MODEL), F32),
        compiler_params=_cparams(("arbitrary", "arbitrary")),
        name="ada",
    )(c, w_ada, b_ada.reshape(depth, 1, 6 * D_MODEL))


def _modulated_rms(x, g, scale, shift):
    ms = jnp.mean(x * x, axis=-1, keepdims=True)
    return (x * lax.rsqrt(ms + EPS)) * g * (1.0 + scale) + shift


def _norm_rope_t(q, cos_t, sin_t, perm_blocks):
    ss = jnp.sum(q * q, axis=0, keepdims=True)
    r = lax.rsqrt(ss * (1.0 / HEAD_DIM) + EPS)
    qp = jnp.concatenate([q[a:b] for a, b in perm_blocks], axis=0)
    return (q * cos_t + qp * sin_t) * r


AXIAL_BLOCKS = ((16, 32), (0, 16), (48, 64), (32, 48))
SEQ_BLOCKS = ((32, 64), (0, 32))


def _store_class_major(out_ref, stage_ref, tile, dil, sub):
    rows = tile.shape[0] // dil
    out_rows = slice(sub * rows, (sub + 1) * rows)
    if dil == 1:
        out_ref[out_rows, :] = tile.astype(BF16)
        return
    for half in range(B_W // V7X_LANES):
        stage_ref[sub, half] = tile[:, half * V7X_LANES:(half + 1) * V7X_LANES]
    for r in range(dil):
        for half in range(B_W // V7X_LANES):
            col = r * B_W + half * V7X_LANES
            out_ref[out_rows, col:col + V7X_LANES] = (
                stage_ref[sub, half, pl.ds(r, rows, stride=dil), :].astype(BF16))


def _in_kernel(x_ref, ada_ref, g_ref, wn_ref, wt_ref, rope_ref, gain_ref,
               qat_ref, ka_ref, vat_ref,
               qb0_ref, kb0_ref, vb0_ref, qb1_ref, kb1_ref, vb1_ref, qb2_ref, kb2_ref, vb2_ref,
               ga_ref, gb_ref, stage_q_ref, stage_k_ref, stage_v_ref):
    for sub in range(TM_IN // CHUNK):
        _in_subtile(sub, x_ref, ada_ref, g_ref, wn_ref, wt_ref, rope_ref, gain_ref, qat_ref, ka_ref, vat_ref,
                    (qb0_ref, qb1_ref, qb2_ref), (kb0_ref, kb1_ref, kb2_ref), (vb0_ref, vb1_ref, vb2_ref),
                    ga_ref, gb_ref, stage_q_ref, stage_k_ref, stage_v_ref)


def _in_subtile(sub, x_ref, ada_ref, g_ref, wn_ref, wt_ref, rope_ref, gain_ref, qat_ref, ka_ref, vat_ref,
                qb_refs, kb_refs, vb_refs, ga_ref, gb_ref, stage_q_ref, stage_k_ref, stage_v_ref):
    tok = slice(sub * CHUNK, (sub + 1) * CHUNK)
    x = x_ref[tok, :]
    h = _modulated_rms(x, g_ref[...], ada_ref[1:2, :], ada_ref[0:1, :]).astype(BF16)

    nt = (((1,), (1,)), ((), ()))

    def proj_t(row0, rows):
        return lax.dot_general(wt_ref[row0:row0 + rows, :], h, nt, preferred_element_type=F32)

    def gained(tab, base_row):
        gain = gain_ref[tab * HEAD_DIM:(tab + 1) * HEAD_DIM, :]
        gain = jnp.concatenate([gain] * (CHUNK // V7X_LANES), axis=1)
        return rope_ref[base_row * HEAD_DIM:(base_row + 1) * HEAD_DIM, tok] * gain

    def heads_t(rt, n_heads, tab0, blocks):
        base_row = 0 if blocks is AXIAL_BLOCKS else 2
        cos_t = gained(tab0, base_row)
        sin_t = gained(tab0 + 1, base_row + 1)
        return [_norm_rope_t(rt[i * HEAD_DIM:(i + 1) * HEAD_DIM], cos_t, sin_t, blocks)
                for i in range(n_heads)]

    row = 0
    rt = proj_t(row, A_Q_W)
    qa = heads_t(rt, A_Q_HEADS, 0, AXIAL_BLOCKS)
    for i in range(A_Q_HEADS):
        qat_ref[sub, i * HEAD_DIM:(i + 1) * HEAD_DIM, :] = qa[i].astype(BF16)
    row += A_Q_W

    rt = proj_t(row, 2 * A_KV_W)
    ka = heads_t(rt[:A_KV_W], A_KV_HEADS, 2, AXIAL_BLOCKS)
    ka_ref[tok, :] = jnp.concatenate(ka, axis=0).T.astype(BF16)
    vat_ref[sub] = rt[A_KV_W:].astype(BF16)
    row += 2 * A_KV_W

    for g in range(N_B):
        rt = proj_t(row, 2 * B_W)
        qb = heads_t(rt[:B_W], B_HEADS, 4 + 4 * g, SEQ_BLOCKS)
        kb = heads_t(rt[B_W:], B_HEADS, 6 + 4 * g, SEQ_BLOCKS)
        _store_class_major(qb_refs[g], stage_q_ref, jnp.concatenate(qb, axis=0).T, B_GROUPS[g][1], sub)
        _store_class_major(kb_refs[g], stage_k_ref, jnp.concatenate(kb, axis=0).T, B_GROUPS[g][1], sub)
        row += 2 * B_W

    for i, gate_ref in enumerate((ga_ref, gb_ref)):
        lo = N_B * B_W + i * D_MODEL
        z = jnp.dot(h, wn_ref[:, lo:lo + D_MODEL], preferred_element_type=F32)
        gate_ref[tok, :] = (1.0 / (1.0 + jnp.exp(-z))).astype(BF16)
    for g in reversed(range(N_B)):
        v = jnp.dot(h, wn_ref[:, g * B_W:(g + 1) * B_W], preferred_element_type=F32)
        _store_class_major(vb_refs[g], stage_v_ref, v, B_GROUPS[g][1], sub)


def _in_proj(x, ada_l, g_mix_l, wn, wt, rope, gains, l):
    nb, seq, _ = x.shape
    ns = seq // TM_IN
    nsub = TM_IN // CHUNK
    tok = lambda w: pl.BlockSpec((None, TM_IN, w), lambda s, b: (b, s, 0))
    tchunk = lambda r: pl.BlockSpec((None, nsub, r, CHUNK), lambda s, b: (b, s, 0, 0))
    nat = lambda w: jax.ShapeDtypeStruct((nb, seq, w), BF16)
    out_specs = [tchunk(A_Q_W), tok(A_KV_W), tchunk(A_KV_W)]
    out_shape = [jax.ShapeDtypeStruct((nb, seq // CHUNK, A_Q_W, CHUNK), BF16), nat(A_KV_W),
                 jax.ShapeDtypeStruct((nb, seq // CHUNK, A_KV_W, CHUNK), BF16)]
    for _, dil in B_GROUPS:
        out_specs += [pl.BlockSpec((None, TM_IN // dil, dil * B_W), lambda s, b: (b, s, 0))] * 3
        out_shape += [jax.ShapeDtypeStruct((nb, seq // dil, dil * B_W), BF16)] * 3
    out_specs += [tok(D_MODEL)] * 2
    out_shape += [nat(D_MODEL)] * 2
    return pl.pallas_call(
        _in_kernel,
        grid=(ns, nb),
        in_specs=[
            tok(D_MODEL),
            pl.BlockSpec((None, 6, D_MODEL), lambda s, b: (b, 0, 0)),
            pl.BlockSpec((None, 1, D_MODEL), lambda s, b: (l, 0, 0)),
            pl.BlockSpec((None, D_MODEL, N_COLS), lambda s, b: (l, 0, 0), pipeline_mode=pl.Buffered(1)),
            pl.BlockSpec((None, T_ROWS, D_MODEL), lambda s, b: (l, 0, 0), pipeline_mode=pl.Buffered(1)),
            pl.BlockSpec((4 * HEAD_DIM, TM_IN), lambda s, b: (0, s)),
            pl.BlockSpec((None, N_TABS * HEAD_DIM, V7X_LANES), lambda s, b: (l, 0, 0)),
        ],
        out_specs=out_specs,
        out_shape=out_shape,
        scratch_shapes=[pltpu.VMEM((nsub, B_W // V7X_LANES, CHUNK, V7X_LANES), F32)] * 3,
        compiler_params=_cparams(("arbitrary", "arbitrary")),
        name="in_proj",
    )(x, ada_l, g_mix_l, wn, wt, rope, gains)


A_REP = A_Q_HEADS // A_KV_HEADS
ONES_ROWS = 16


def _attn_a_kernel(qt_ref, k_ref, vt_ref, o_ref, qpad_ref, acc_ref, m_ref, s_ref, smax_ref):
    for sub in range(qt_ref.shape[0]):
        _attn_a_tile(qt_ref.at[sub], k_ref, vt_ref, o_ref.at[sub * CHUNK:(sub + 1) * CHUNK],
                     qpad_ref.at[sub], acc_ref.at[sub], m_ref.at[sub], s_ref, smax_ref)


def _attn_a_tile(qt_ref, k_ref, vt_ref, o_ref, qpad_ref, acc_ref, m_ref, s_ref, smax_ref):
    n_chunks, _, tk = vt_ref.shape
    tq = qt_ref.shape[1]
    qpad_ref[...] = jnp.zeros(qpad_ref.shape, BF16)
    for g in range(A_KV_HEADS):
        for hh in range(A_REP):
            h = g * A_REP + hh
            qpad_ref[g, g * HEAD_DIM:(g + 1) * HEAD_DIM, hh * tq:(hh + 1) * tq] = (
                qt_ref[h * HEAD_DIM:(h + 1) * HEAD_DIM, :])
    m_ref[...] = jnp.full(m_ref.shape, NEG_BIG, F32)
    acc_ref[...] = jnp.zeros(acc_ref.shape, F32)
    ones = jnp.ones((ONES_ROWS, tk), BF16)

    def scores(c, slot, g, hh):
        lanes = slice(hh * tq, (hh + 1) * tq)
        kc = k_ref[pl.ds(pl.multiple_of(c * tk, tk), tk), :]
        s = jnp.dot(kc, qpad_ref[g, :, lanes], preferred_element_type=F32)
        s_ref[slot, g, :, lanes] = s
        smax_ref[slot, g, :, lanes] = jnp.max(s, axis=0, keepdims=True)

    def consume(c, slot, g, hh):
        lanes = slice(hh * tq, (hh + 1) * tq)
        m_prev = m_ref[g, :, lanes]
        m_new = jnp.maximum(m_prev, smax_ref[slot, g, :, lanes])
        alpha = jnp.exp2(m_prev - m_new)
        p = jnp.exp2(s_ref[slot, g, :, lanes] - m_new).astype(BF16)
        v_aug = jnp.concatenate([vt_ref[c, g * HEAD_DIM:(g + 1) * HEAD_DIM, :], ones], axis=0)
        acc_ref[g, :, lanes] = acc_ref[g, :, lanes] * alpha + jnp.dot(v_aug, p, preferred_element_type=F32)
        m_ref[g, :, lanes] = m_new

    for g in range(A_KV_HEADS):
        for hh in range(A_REP):
            scores(0, 0, g, hh)

    def chunk_pair(c, last):
        for slot in range(2):
            for g in range(A_KV_HEADS):
                for hh in range(A_REP):
                    if not (last and slot == 1):
                        scores(c + slot + 1, 1 - slot, g, hh)
                    consume(c + slot, slot, g, hh)

    body_chunks = 2 * A_PAIRS_PER_BODY

    def loop_body(i, carry):
        for j in range(A_PAIRS_PER_BODY):
            chunk_pair(body_chunks * i + 2 * j, False)
        return carry

    n_loop = (n_chunks - 2) // body_chunks
    lax.fori_loop(0, n_loop, loop_body, 0)
    for c in range(n_loop * body_chunks, n_chunks, 2):
        chunk_pair(c, c == n_chunks - 2)
    for j in range(A_Q_W // V7X_LANES):
        halves = []
        for h in (2 * j, 2 * j + 1):
            g, hh = divmod(h, A_REP)
            cols = slice(hh * tq, (hh + 1) * tq)
            halves.append(acc_ref[g, :HEAD_DIM, cols] / acc_ref[g, HEAD_DIM:HEAD_DIM + 1, cols])
        blk = jnp.concatenate(halves, axis=0)
        o_ref[:, j * V7X_LANES:(j + 1) * V7X_LANES] = blk.T.astype(BF16)


def _attn_a(qat, ka, vat):
    nb, ns, _, chunk = qat.shape
    seq = ka.shape[1]
    nq = TQ_A // chunk
    wide = A_REP * chunk
    return pl.pallas_call(
        _attn_a_kernel,
        grid=(nb, seq // TQ_A),
        in_specs=[
            pl.BlockSpec((None, nq, A_Q_W, chunk), lambda b, s: (b, s, 0, 0)),
            pl.BlockSpec((None, seq, A_KV_W), lambda b, s: (b, 0, 0)),
            pl.BlockSpec((None, ns, A_KV_W, chunk), lambda b, s: (b, 0, 0, 0)),
        ],
        out_specs=pl.BlockSpec((None, TQ_A, A_Q_W), lambda b, s: (b, s, 0)),
        out_shape=jax.ShapeDtypeStruct((nb, seq, A_Q_W), BF16),
        scratch_shapes=[pltpu.VMEM((nq, A_KV_HEADS, A_KV_W, wide), BF16),
                        pltpu.VMEM((nq, A_KV_HEADS, HEAD_DIM + ONES_ROWS, wide), F32),
                        pltpu.VMEM((nq, A_KV_HEADS, 1, wide), F32),
                        pltpu.VMEM((2, A_KV_HEADS, chunk, wide), F32),
                        pltpu.VMEM((2, A_KV_HEADS, 1, wide), F32)],
        compiler_params=_cparams(("arbitrary", "arbitrary")),
        name="attn_a",
    )(qat, ka, vat)


def _attn_b_kernel(q_ref, k_ref, v_ref, o_ref, lse_ref, *, half_window):
    tj = q_ref.shape[0]
    length = k_ref.shape[0]
    j_tile = pl.program_id(2) * tj
    lane = lax.broadcasted_iota(jnp.int32, (1, V7X_LANES), 1)
    first_head = lane < HEAD_DIM
    nt = (((1,), (1,)), ((), ()))
    ones = jnp.ones((KW_B, V7X_LANES), BF16)
    n_pairs = q_ref.shape[1] // V7X_LANES
    units = [(qi, pair) for qi in range(tj // TQ_B) for pair in range(n_pairs)]
    window = {}

    def key_window(qi):
        if qi not in window:
            j0 = j_tile + qi * TQ_B
            ks = pl.multiple_of(jnp.clip(j0 - half_window, 0, length - KW_B), half_window)
            jq = j0 + lax.broadcasted_iota(jnp.int32, (TQ_B, 1), 0)
            jk = ks + lax.broadcasted_iota(jnp.int32, (1, KW_B), 1)
            bias = jnp.where(jnp.abs(jq - jk) <= half_window, 0.0, NEG_BIG)
            window[qi] = ks, jnp.concatenate([bias, bias], axis=0)
        return window[qi]

    def logits(qi, pair):
        ks, bias2 = key_window(qi)
        cols = slice(pair * V7X_LANES, (pair + 1) * V7X_LANES)
        q2 = q_ref[qi * TQ_B:(qi + 1) * TQ_B, cols]
        k2 = k_ref[pl.ds(ks, KW_B), cols]
        zero = jnp.zeros_like(q2)
        qs = jnp.concatenate([jnp.where(first_head, q2, zero), jnp.where(first_head, zero, q2)], axis=0)
        return lax.dot_general(qs, k2, nt, preferred_element_type=F32) + bias2

    def finish(qi, pair, s):
        ks, _ = key_window(qi)
        cols = slice(pair * V7X_LANES, (pair + 1) * V7X_LANES)
        v_aug = jnp.concatenate([v_ref[pl.ds(ks, KW_B), cols], ones], axis=1)
        m = jnp.max(s, axis=1, keepdims=True)
        p = jnp.exp2(s - m).astype(BF16)
        ol = jnp.dot(p, v_aug, preferred_element_type=F32)
        o_pair = jnp.where(first_head, ol[:TQ_B, :V7X_LANES], ol[TQ_B:, :V7X_LANES])
        l_pair = jnp.where(first_head, ol[:TQ_B, V7X_LANES:], ol[TQ_B:, V7X_LANES:])
        m_pair = jnp.where(first_head, m[:TQ_B], m[TQ_B:])
        o_ref[qi * TQ_B:(qi + 1) * TQ_B, cols] = (o_pair / l_pair).astype(BF16)
        lse_ref[qi * TQ_B:(qi + 1) * TQ_B, cols] = m_pair + jnp.log2(l_pair)

    s_cur = logits(*units[0])
    for i, unit in enumerate(units):
        s_next = logits(*units[i + 1]) if i + 1 < len(units) else None
        finish(*unit, s_cur)
        s_cur = s_next


def _attn_b(q, k, v, window, dil):
    nb, length, _ = q.shape
    half_window = (window // 2) // dil
    assert half_window * 2 + TQ_B == KW_B and length >= KW_B
    tj = min(B_ROWS_PER_STEP, length)
    ncls = min(B_ROWS_PER_STEP // tj, dil)
    qspec = pl.BlockSpec((None, tj, ncls * B_W), lambda b, r, j: (b, j, r))
    kvspec = pl.BlockSpec((None, length, ncls * B_W), lambda b, r, j: (b, 0, r))
    return pl.pallas_call(
        functools.partial(_attn_b_kernel, half_window=half_window),
        grid=(nb, dil // ncls, length // tj),
        in_specs=[qspec, kvspec, kvspec],
        out_specs=[qspec, qspec],
        out_shape=[jax.ShapeDtypeStruct((nb, length, dil * B_W), BF16),
                   jax.ShapeDtypeStruct((nb, length, dil * B_W), F32)],
        compiler_params=_cparams(("arbitrary", "arbitrary", "arbitrary")),
        name=f"attn_b_d{dil}",
    )(q, k, v)


def _load_class_major(blk_ref, stage_ref, dil, sub):
    rows = CHUNK // dil
    blk_rows = slice(sub * rows, (sub + 1) * rows)
    if dil == 1:
        return blk_ref[blk_rows, :].astype(F32)
    halves = range(B_W // V7X_LANES)
    for r in range(dil):
        for half in halves:
            col = r * B_W + half * V7X_LANES
            stage_ref[sub, half, pl.ds(r, rows, stride=dil), :] = (
                blk_ref[blk_rows, col:col + V7X_LANES].astype(F32))
    return jnp.concatenate([stage_ref[sub, half] for half in halves], axis=1)


def _mix_mlp_kernel(x_ref, ada_ref, oa_ref, ob0_ref, ob1_ref, ob2_ref, ls0_ref, ls1_ref, ls2_ref,
                    ga_ref, gb_ref, wa_ref, wb_ref, wo_ref, g_ref, w1_ref, w2_ref, xo_ref, *stage_refs):
    dils = [d for _, d in B_GROUPS]
    for sub in range(TM_MIX // CHUNK):
        tok = slice(sub * CHUNK, (sub + 1) * CHUNK)
        ya = jnp.dot(oa_ref[tok, :], wa_ref[...], preferred_element_type=F32)
        lses = [_load_class_major(r, stage_refs[2 * g], dils[g], sub)
                for g, r in enumerate((ls0_ref, ls1_ref, ls2_ref))]
        outs = [_load_class_major(r, stage_refs[2 * g + 1], dils[g], sub)
                for g, r in enumerate((ob0_ref, ob1_ref, ob2_ref))]
        top = jnp.maximum(jnp.maximum(lses[0], lses[1]), lses[2])
        es = [jnp.exp2(v - top) for v in lses]
        den = es[0] + es[1] + es[2]
        ob = (es[0] * outs[0] + es[1] * outs[1] + es[2] * outs[2]) / den
        yb = jnp.dot(ob.astype(BF16), wb_ref[...], preferred_element_type=F32)
        mixed = ga_ref[tok, :].astype(F32) * ya + gb_ref[tok, :].astype(F32) * yb
        upd = jnp.dot(mixed.astype(BF16), wo_ref[...], preferred_element_type=F32)
        xo_ref[tok, :] = x_ref[tok, :] + ada_ref[2:3, :] * upd

    x = xo_ref[...]
    h = _modulated_rms(x, g_ref[...], ada_ref[4:5, :], ada_ref[3:4, :]).astype(BF16)
    acc = jnp.zeros(x.shape, F32)
    for c in range(D_FF // FF_CHUNK):
        a = jnp.dot(h, w1_ref[:, c * FF_CHUNK:(c + 1) * FF_CHUNK], preferred_element_type=F32)
        a = jnp.square(jnp.maximum(a, 0.0)).astype(BF16)
        acc = acc + jnp.dot(a, w2_ref[c * FF_CHUNK:(c + 1) * FF_CHUNK, :], preferred_element_type=F32)
    xo_ref[...] = x + ada_ref[5:6, :] * acc


def _mix_mlp(x, ada_l, oa, obs, lss, ga, gb, wa, wb, wo, g_mlp_l, w1, w2, l):
    nb, seq, _ = x.shape
    tok = lambda w: pl.BlockSpec((None, TM_MIX, w), lambda b, s: (b, s, 0))
    wspec = lambda r, c: pl.BlockSpec((None, r, c), lambda b, s: (l, 0, 0), pipeline_mode=pl.Buffered(1))
    cls = [pl.BlockSpec((None, TM_MIX // d, d * B_W), lambda b, s: (b, s, 0)) for _, d in B_GROUPS]
    return pl.pallas_call(
        _mix_mlp_kernel,
        grid=(nb, seq // TM_MIX),
        in_specs=[tok(D_MODEL), pl.BlockSpec((None, 6, D_MODEL), lambda b, s: (b, 0, 0)), tok(A_Q_W)]
                 + cls + cls + [tok(D_MODEL)] * 2
                 + [wspec(A_Q_W, D_MODEL), wspec(B_W, D_MODEL), wspec(D_MODEL, D_MODEL),
                    pl.BlockSpec((None, 1, D_MODEL), lambda b, s: (l, 0, 0)),
                    wspec(D_MODEL, D_FF), wspec(D_FF, D_MODEL)],
        out_specs=tok(D_MODEL),
        out_shape=jax.ShapeDtypeStruct(x.shape, F32),
        scratch_shapes=[pltpu.VMEM((TM_MIX // CHUNK, B_W // V7X_LANES, CHUNK, V7X_LANES), F32)] * (2 * N_B),
        compiler_params=_cparams(("arbitrary", "arbitrary")),
        name="mix_mlp",
    )(x, ada_l, oa, *obs, *lss, ga, gb, wa, wb, wo, g_mlp_l, w1, w2)


def _rope_tables(seq):
    pos = jnp.arange(seq)
    rows = seq // GRID_W
    row_idx = jnp.broadcast_to(jnp.arange(rows)[:, None], (rows, GRID_W)).reshape(-1)
    col_idx = jnp.broadcast_to(jnp.arange(GRID_W)[None, :], (rows, GRID_W)).reshape(-1)

    def cos_sin(p, dim):
        inv = ROPE_THETA ** (-jnp.arange(0, dim, 2, dtype=F32) / dim)
        ang = inv[:, None] * p.astype(F32)[None, :]
        return jnp.cos(ang), jnp.sin(ang)

    rc, rs = cos_sin(row_idx, HEAD_DIM // 2)
    cc, cs = cos_sin(col_idx, HEAD_DIM // 2)
    sc, ss = cos_sin(pos, HEAD_DIM)
    return jnp.concatenate([rc, rc, cc, cc, -rs, rs, -cs, cs, sc, sc, -ss, ss], axis=0)


def _gain_columns(q_norm_a, k_norm_a, q_norm_b, k_norm_b):
    scale = HEAD_DIM ** -0.5 * LOG2_E

    def pair(gain, perm, mult):
        return [gain * mult, gain[:, perm] * mult]

    cols = pair(q_norm_a, PERM_AXIAL, scale) + pair(k_norm_a, PERM_AXIAL, 1.0)
    for g in range(N_B):
        cols += pair(q_norm_b[:, g], PERM_SEQ, scale) + pair(k_norm_b[:, g], PERM_SEQ, 1.0)
    cols = jnp.concatenate(cols, axis=1)
    return jnp.broadcast_to(cols[:, :, None], cols.shape + (V7X_LANES,))


def _split_w_in(w_in):
    sizes = [A_Q_W, A_KV_W, A_KV_W] + [B_W] * (3 * N_B) + [D_MODEL, D_MODEL]
    offs = np.concatenate([[0], np.cumsum(sizes)])
    col = lambda i: w_in[:, :, offs[i]:offs[i + 1]]
    names = ["qa", "ka", "va"] + [f"{t}b{g}" for g in range(N_B) for t in "qkv"] + ["ga", "gb"]
    parts = {n: col(i) for i, n in enumerate(names)}
    wn = jnp.concatenate([parts["vb0"], parts["vb1"], parts["vb2"], parts["ga"], parts["gb"]], axis=-1)
    wt = jnp.concatenate([parts[n] for n, _ in T_SEGS], axis=-1)
    return wn.astype(BF16), jnp.swapaxes(wt, 1, 2).astype(BF16)


def kernel(x, c, w_ada, b_ada, g_mix, g_mlp, w_in, q_norm_a, k_norm_a, q_norm_b, k_norm_b,
           w_branch_a, w_branch_b, w_out, w_ff1, w_ff2):
    nb, seq, _ = x.shape
    depth = w_in.shape[0]
    assert seq % GRID_W == 0 and seq % TM_MIX == 0 and seq % TM_IN == 0

    ada = _ada_all(c, w_ada, b_ada).reshape(depth, nb, 6, D_MODEL)
    rope = _rope_tables(seq)
    gains = _gain_columns(q_norm_a, k_norm_a, q_norm_b, k_norm_b)
    wn, wt = _split_w_in(w_in)
    wa, wb, wo = w_branch_a.astype(BF16), w_branch_b.astype(BF16), w_out.astype(BF16)
    w1, w2 = w_ff1.astype(BF16), w_ff2.astype(BF16)
    g_mix = g_mix.reshape(depth, 1, D_MODEL)
    g_mlp = g_mlp.reshape(depth, 1, D_MODEL)

    for l in range(depth):
        (qat, ka, vat, qb0, kb0, vb0, qb1, kb1, vb1, qb2, kb2, vb2, ga, gb) = _in_proj(
            x, ada[l], g_mix, wn, wt, rope, gains, l)
        oa = _attn_a(qat, ka, vat)
        obs, lss = [], []
        for (window, dil), q, k, v in zip(B_GROUPS, (qb0, qb1, qb2), (kb0, kb1, kb2), (vb0, vb1, vb2)):
            o, lse = _attn_b(q, k, v, window, dil)
            obs.append(o)
            lss.append(lse)
        x = _mix_mlp(x, ada[l], oa, obs, lss, ga, gb, wa, wb, wo, g_mlp, w1, w2, l)
    return x
```

```python
import functools

import jax
import jax.numpy as jnp
import numpy as np
from jax import lax
from jax.experimental import pallas as pl
from jax.experimental.pallas import tpu as pltpu

D_MODEL = 1024
HEAD_DIM = 64
A_Q_HEADS = 8
A_KV_HEADS = 2
B_GROUPS = ((128, 1), (512, 4), (2048, 16))
B_HEADS = 4
N_B = len(B_GROUPS)
D_FF = 4 * D_MODEL
GRID_W = 64
ROPE_THETA = 10000.0
EPS = 1e-6

A_Q_W = A_Q_HEADS * HEAD_DIM
A_KV_W = A_KV_HEADS * HEAD_DIM
B_W = B_HEADS * HEAD_DIM

V7X_LANES = 128
V7X_VMEM_BYTES = 64 * 1024 * 1024
VMEM_LIMIT = V7X_VMEM_BYTES * 13 // 16

BF16 = jnp.bfloat16
F32 = jnp.float32

CHUNK = 256
TM_IN = 512
TQ_A = 1024
A_PAIRS_PER_BODY = 3
TM_MIX = 512
FF_CHUNK = 1024
B_ROWS_PER_STEP = 4096
TQ_B = 128
KW_B = 256
NEG_BIG = -1e30
LOG2_E = 1.4426950408889634

PERM_AXIAL = np.concatenate([np.arange(16, 32), np.arange(0, 16), np.arange(48, 64), np.arange(32, 48)])
PERM_SEQ = np.concatenate([np.arange(32, 64), np.arange(0, 32)])

T_SEGS = (("qa", A_Q_W), ("ka", A_KV_W), ("va", A_KV_W),
          ("qb0", B_W), ("kb0", B_W), ("qb1", B_W), ("kb1", B_W), ("qb2", B_W), ("kb2", B_W))
T_ROWS = sum(r for _, r in T_SEGS)
N_COLS = N_B * B_W + 2 * D_MODEL
N_TABS = 16


def _cparams(sem):
    return pltpu.CompilerParams(dimension_semantics=sem, vmem_limit_bytes=VMEM_LIMIT)


def _ada_kernel(c_ref, w_ref, b_ref, o_ref):
    c = c_ref[...]
    c_act = c * (1.0 / (1.0 + jnp.exp(-c)))
    o_ref[...] = jnp.dot(c_act, w_ref[...], preferred_element_type=F32,
                         precision=lax.Precision.HIGHEST) + b_ref[...]


def _ada_all(c, w_ada, b_ada):
    depth = w_ada.shape[0]
    nb = c.shape[0]
    return pl.pallas_call(
        _ada_kernel,
        grid=(depth, 6),
        in_specs=[
            pl.BlockSpec((nb, D_MODEL), lambda l, j: (0, 0)),
            pl.BlockSpec((None, D_MODEL, D_MODEL), lambda l, j: (l, 0, j)),
            pl.BlockSpec((None, 1, D_MODEL), lambda l, j: (l, 0, j)),
        ],
        out_specs=pl.BlockSpec((None, nb, D_MODEL), lambda l, j: (l, 0, j)),
        out_shape=jax.ShapeDtypeStruct((depth, nb, 6 * D_MODEL), F32),
        compiler_params=_cparams(("arbitrary", "arbitrary")),
        name="ada",
    )(c, w_ada, b_ada.reshape(depth, 1, 6 * D_MODEL))


def _modulated_rms(x, g, scale, shift):
    ms = jnp.mean(x * x, axis=-1, keepdims=True)
    return (x * lax.rsqrt(ms + EPS)) * g * (1.0 + scale) + shift


def _norm_rope_t(q, cos_t, sin_t, perm_blocks):
    ss = jnp.sum(q * q, axis=0, keepdims=True)
    r = lax.rsqrt(ss * (1.0 / HEAD_DIM) + EPS)
    qp = jnp.concatenate([q[a:b] for a, b in perm_blocks], axis=0)
    return (q * cos_t + qp * sin_t) * r


AXIAL_BLOCKS = ((16, 32), (0, 16), (48, 64), (32, 48))
SEQ_BLOCKS = ((32, 64), (0, 32))


def _store_class_major(out_ref, stage_ref, tile, dil, sub):
    rows = tile.shape[0] // dil
    out_rows = slice(sub * rows, (sub + 1) * rows)
    if dil == 1:
        out_ref[out_rows, :] = tile.astype(BF16)
        return
    for half in range(B_W // V7X_LANES):
        stage_ref[sub, half] = tile[:, half * V7X_LANES:(half + 1) * V7X_LANES]
    for r in range(dil):
        for half in range(B_W // V7X_LANES):
            col = r * B_W + half * V7X_LANES
            out_ref[out_rows, col:col + V7X_LANES] = (
                stage_ref[sub, half, pl.ds(r, rows, stride=dil), :].astype(BF16))


def _in_kernel(x_ref, ada_ref, g_ref, wn_ref, wt_ref, rope_ref, gain_ref,
               qat_ref, ka_ref, vat_ref,
               qb0_ref, kb0_ref, vb0_ref, qb1_ref, kb1_ref, vb1_ref, qb2_ref, kb2_ref, vb2_ref,
               ga_ref, gb_ref, stage_q_ref, stage_k_ref, stage_v_ref):
    for sub in range(TM_IN // CHUNK):
        _in_subtile(sub, x_ref, ada_ref, g_ref, wn_ref, wt_ref, rope_ref, gain_ref, qat_ref, ka_ref, vat_ref,
                    (qb0_ref, qb1_ref, qb2_ref), (kb0_ref, kb1_ref, kb2_ref), (vb0_ref, vb1_ref, vb2_ref),
                    ga_ref, gb_ref, stage_q_ref, stage_k_ref, stage_v_ref)


def _in_subtile(sub, x_ref, ada_ref, g_ref, wn_ref, wt_ref, rope_ref, gain_ref, qat_ref, ka_ref, vat_ref,
                qb_refs, kb_refs, vb_refs, ga_ref, gb_ref, stage_q_ref, stage_k_ref, stage_v_ref):
    tok = slice(sub * CHUNK, (sub + 1) * CHUNK)
    x = x_ref[tok, :]
    h = _modulated_rms(x, g_ref[...], ada_ref[1:2, :], ada_ref[0:1, :]).astype(BF16)

    nt = (((1,), (1,)), ((), ()))

    def proj_t(row0, rows):
        return lax.dot_general(wt_ref[row0:row0 + rows, :], h, nt, preferred_element_type=F32)

    def gained(tab, base_row):
        gain = gain_ref[tab * HEAD_DIM:(tab + 1) * HEAD_DIM, :]
        gain = jnp.concatenate([gain] * (CHUNK // V7X_LANES), axis=1)
        return rope_ref[base_row * HEAD_DIM:(base_row + 1) * HEAD_DIM, tok] * gain

    def heads_t(rt, n_heads, tab0, blocks):
        base_row = 0 if blocks is AXIAL_BLOCKS else 2
        cos_t = gained(tab0, base_row)
        sin_t = gained(tab0 + 1, base_row + 1)
        return [_norm_rope_t(rt[i * HEAD_DIM:(i + 1) * HEAD_DIM], cos_t, sin_t, blocks)
                for i in range(n_heads)]

    row = 0
    rt = proj_t(row, A_Q_W)
    qa = heads_t(rt, A_Q_HEADS, 0, AXIAL_BLOCKS)
    for i in range(A_Q_HEADS):
        qat_ref[sub, i * HEAD_DIM:(i + 1) * HEAD_DIM, :] = qa[i].astype(BF16)
    row += A_Q_W

    rt = proj_t(row, 2 * A_KV_W)
    ka = heads_t(rt[:A_KV_W], A_KV_HEADS, 2, AXIAL_BLOCKS)
    ka_ref[tok, :] = jnp.concatenate(ka, axis=0).T.astype(BF16)
    vat_ref[sub] = rt[A_KV_W:].astype(BF16)
    row += 2 * A_KV_W

    for g in range(N_B):
        rt = proj_t(row, 2 * B_W)
        qb = heads_t(rt[:B_W], B_HEADS, 4 + 4 * g, SEQ_BLOCKS)
        kb = heads_t(rt[B_W:], B_HEADS, 6 + 4 * g, SEQ_BLOCKS)
        _store_class_major(qb_refs[g], stage_q_ref, jnp.concatenate(qb, axis=0).T, B_GROUPS[g][1], sub)
        _store_class_major(kb_refs[g], stage_k_ref, jnp.concatenate(kb, axis=0).T, B_GROUPS[g][1], sub)
        row += 2 * B_W

    for i, gate_ref in enumerate((ga_ref, gb_ref)):
        lo = N_B * B_W + i * D_MODEL
        z = jnp.dot(h, wn_ref[:, lo:lo + D_MODEL], preferred_element_type=F32)
        gate_ref[tok, :] = (1.0 / (1.0 + jnp.exp(-z))).astype(BF16)
    for g in reversed(range(N_B)):
        v = jnp.dot(h, wn_ref[:, g * B_W:(g + 1) * B_W], preferred_element_type=F32)
        _store_class_major(vb_refs[g], stage_v_ref, v, B_GROUPS[g][1], sub)


def _in_proj(x, ada_l, g_mix_l, wn, wt, rope, gains, l):
    nb, seq, _ = x.shape
    ns = seq // TM_IN
    nsub = TM_IN // CHUNK
    tok = lambda w: pl.BlockSpec((None, TM_IN, w), lambda s, b: (b, s, 0))
    tchunk = lambda r: pl.BlockSpec((None, nsub, r, CHUNK), lambda s, b: (b, s, 0, 0))
    nat = lambda w: jax.ShapeDtypeStruct((nb, seq, w), BF16)
    out_specs = [tchunk(A_Q_W), tok(A_KV_W), tchunk(A_KV_W)]
    out_shape = [jax.ShapeDtypeStruct((nb, seq // CHUNK, A_Q_W, CHUNK), BF16), nat(A_KV_W),
                 jax.ShapeDtypeStruct((nb, seq // CHUNK, A_KV_W, CHUNK), BF16)]
    for _, dil in B_GROUPS:
        out_specs += [pl.BlockSpec((None, TM_IN // dil, dil * B_W), lambda s, b: (b, s, 0))] * 3
        out_shape += [jax.ShapeDtypeStruct((nb, seq // dil, dil * B_W), BF16)] * 3
    out_specs += [tok(D_MODEL)] * 2
    out_shape += [nat(D_MODEL)] * 2
    return pl.pallas_call(
        _in_kernel,
        grid=(ns, nb),
        in_specs=[
            tok(D_MODEL),
            pl.BlockSpec((None, 6, D_MODEL), lambda s, b: (b, 0, 0)),
            pl.BlockSpec((None, 1, D_MODEL), lambda s, b: (l, 0, 0)),
            pl.BlockSpec((None, D_MODEL, N_COLS), lambda s, b: (l, 0, 0), pipeline_mode=pl.Buffered(1)),
            pl.BlockSpec((None, T_ROWS, D_MODEL), lambda s, b: (l, 0, 0), pipeline_mode=pl.Buffered(1)),
            pl.BlockSpec((4 * HEAD_DIM, TM_IN), lambda s, b: (0, s)),
            pl.BlockSpec((None, N_TABS * HEAD_DIM, V7X_LANES), lambda s, b: (l, 0, 0)),
        ],
        out_specs=out_specs,
        out_shape=out_shape,
        scratch_shapes=[pltpu.VMEM((nsub, B_W // V7X_LANES, CHUNK, V7X_LANES), F32)] * 3,
        compiler_params=_cparams(("arbitrary", "arbitrary")),
        name="in_proj",
    )(x, ada_l, g_mix_l, wn, wt, rope, gains)


A_REP = A_Q_HEADS // A_KV_HEADS
ONES_ROWS = 16


def _attn_a_kernel(qt_ref, k_ref, vt_ref, o_ref, qpad_ref, acc_ref, m_ref, s_ref, smax_ref):
    for sub in range(qt_ref.shape[0]):
        _attn_a_tile(qt_ref.at[sub], k_ref, vt_ref, o_ref.at[sub * CHUNK:(sub + 1) * CHUNK],
                     qpad_ref.at[sub], acc_ref.at[sub], m_ref.at[sub], s_ref, smax_ref)


def _attn_a_tile(qt_ref, k_ref, vt_ref, o_ref, qpad_ref, acc_ref, m_ref, s_ref, smax_ref):
    n_chunks, _, tk = vt_ref.shape
    tq = qt_ref.shape[1]
    qpad_ref[...] = jnp.zeros(qpad_ref.shape, BF16)
    for g in range(A_KV_HEADS):
        for hh in range(A_REP):
            h = g * A_REP + hh
            qpad_ref[g, g * HEAD_DIM:(g + 1) * HEAD_DIM, hh * tq:(hh + 1) * tq] = (
                qt_ref[h * HEAD_DIM:(h + 1) * HEAD_DIM, :])
    m_ref[...] = jnp.full(m_ref.shape, NEG_BIG, F32)
    acc_ref[...] = jnp.zeros(acc_ref.shape, F32)
    ones = jnp.ones((ONES_ROWS, tk), BF16)

    def scores(c, slot, g, hh):
        lanes = slice(hh * tq, (hh + 1) * tq)
        kc = k_ref[pl.ds(pl.multiple_of(c * tk, tk), tk), :]
        s = jnp.dot(kc, qpad_ref[g, :, lanes], preferred_element_type=F32)
        s_ref[slot, g, :, lanes] = s
        smax_ref[slot, g, :, lanes] = jnp.max(s, axis=0, keepdims=True)

    def consume(c, slot, g, hh):
        lanes = slice(hh * tq, (hh + 1) * tq)
        m_prev = m_ref[g, :, lanes]
        m_new = jnp.maximum(m_prev, smax_ref[slot, g, :, lanes])
        alpha = jnp.exp2(m_prev - m_new)
        p = jnp.exp2(s_ref[slot, g, :, lanes] - m_new).astype(BF16)
        v_aug = jnp.concatenate([vt_ref[c, g * HEAD_DIM:(g + 1) * HEAD_DIM, :], ones], axis=0)
        acc_ref[g, :, lanes] = acc_ref[g, :, lanes] * alpha + jnp.dot(v_aug, p, preferred_element_type=F32)
        m_ref[g, :, lanes] = m_new

    for g in range(A_KV_HEADS):
        for hh in range(A_REP):
            scores(0, 0, g, hh)

    def chunk_pair(c, last):
        for slot in range(2):
            for g in range(A_KV_HEADS):
                for hh in range(A_REP):
                    if not (last and slot == 1):
                        scores(c + slot + 1, 1 - slot, g, hh)
                    consume(c + slot, slot, g, hh)

    body_chunks = 2 * A_PAIRS_PER_BODY

    def loop_body(i, carry):
        for j in range(A_PAIRS_PER_BODY):
            chunk_pair(body_chunks * i + 2 * j, False)
        return carry

    n_loop = (n_chunks - 2) // body_chunks
    lax.fori_loop(0, n_loop, loop_body, 0)
    for c in range(n_loop * body_chunks, n_chunks, 2):
        chunk_pair(c, c == n_chunks - 2)
    for j in range(A_Q_W // V7X_LANES):
        halves = []
        for h in (2 * j, 2 * j + 1):
            g, hh = divmod(h, A_REP)
            cols = slice(hh * tq, (hh + 1) * tq)
            halves.append(acc_ref[g, :HEAD_DIM, cols] / acc_ref[g, HEAD_DIM:HEAD_DIM + 1, cols])
        blk = jnp.concatenate(halves, axis=0)
        o_ref[:, j * V7X_LANES:(j + 1) * V7X_LANES] = blk.T.astype(BF16)


def _attn_a(qat, ka, vat):
    nb, ns, _, chunk = qat.shape
    seq = ka.shape[1]
    nq = TQ_A // chunk
    wide = A_REP * chunk
    return pl.pallas_call(
        _attn_a_kernel,
        grid=(nb, seq // TQ_A),
        in_specs=[
            pl.BlockSpec((None, nq, A_Q_W, chunk), lambda b, s: (b, s, 0, 0)),
            pl.BlockSpec((None, seq, A_KV_W), lambda b, s: (b, 0, 0)),
            pl.BlockSpec((None, ns, A_KV_W, chunk), lambda b, s: (b, 0, 0, 0)),
        ],
        out_specs=pl.BlockSpec((None, TQ_A, A_Q_W), lambda b, s: (b, s, 0)),
        out_shape=jax.ShapeDtypeStruct((nb, seq, A_Q_W), BF16),
        scratch_shapes=[pltpu.VMEM((nq, A_KV_HEADS, A_KV_W, wide), BF16),
                        pltpu.VMEM((nq, A_KV_HEADS, HEAD_DIM + ONES_ROWS, wide), F32),
                        pltpu.VMEM((nq, A_KV_HEADS, 1, wide), F32),
                        pltpu.VMEM((2, A_KV_HEADS, chunk, wide), F32),
                        pltpu.VMEM((2, A_KV_HEADS, 1, wide), F32)],
        compiler_params=_cparams(("arbitrary", "arbitrary")),
        name="attn_a",
    )(qat, ka, vat)


def _attn_b_kernel(q_ref, k_ref, v_ref, o_ref, lse_ref, *, half_window):
    tj = q_ref.shape[0]
    length = k_ref.shape[0]
    j_tile = pl.program_id(2) * tj
    lane = lax.broadcasted_iota(jnp.int32, (1, V7X_LANES), 1)
    first_head = lane < HEAD_DIM
    nt = (((1,), (1,)), ((), ()))
    ones = jnp.ones((KW_B, V7X_LANES), BF16)
    n_pairs = q_ref.shape[1] // V7X_LANES
    units = [(qi, pair) for qi in range(tj // TQ_B) for pair in range(n_pairs)]
    window = {}

    def key_window(qi):
        if qi not in window:
            j0 = j_tile + qi * TQ_B
            ks = pl.multiple_of(jnp.clip(j0 - half_window, 0, length - KW_B), half_window)
            jq = j0 + lax.broadcasted_iota(jnp.int32, (TQ_B, 1), 0)
            jk = ks + lax.broadcasted_iota(jnp.int32, (1, KW_B), 1)
            bias = jnp.where(jnp.abs(jq - jk) <= half_window, 0.0, NEG_BIG)
            window[qi] = ks, jnp.concatenate([bias, bias], axis=0)
        return window[qi]

    def logits(qi, pair):
        ks, bias2 = key_window(qi)
        cols = slice(pair * V7X_LANES, (pair + 1) * V7X_LANES)
        q2 = q_ref[qi * TQ_B:(qi + 1) * TQ_B, cols]
        k2 = k_ref[pl.ds(ks, KW_B), cols]
        zero = jnp.zeros_like(q2)
        qs = jnp.concatenate([jnp.where(first_head, q2, zero), jnp.where(first_head, zero, q2)], axis=0)
        return lax.dot_general(qs, k2, nt, preferred_element_type=F32) + bias2

    def finish(qi, pair, s):
        ks, _ = key_window(qi)
        cols = slice(pair * V7X_LANES, (pair + 1) * V7X_LANES)
        v_aug = jnp.concatenate([v_ref[pl.ds(ks, KW_B), cols], ones], axis=1)
        m = jnp.max(s, axis=1, keepdims=True)
        p = jnp.exp2(s - m).astype(BF16)
        ol = jnp.dot(p, v_aug, preferred_element_type=F32)
        o_pair = jnp.where(first_head, ol[:TQ_B, :V7X_LANES], ol[TQ_B:, :V7X_LANES])
        l_pair = jnp.where(first_head, ol[:TQ_B, V7X_LANES:], ol[TQ_B:, V7X_LANES:])
        m_pair = jnp.where(first_head, m[:TQ_B], m[TQ_B:])
        o_ref[qi * TQ_B:(qi + 1) * TQ_B, cols] = (o_pair / l_pair).astype(BF16)
        lse_ref[qi * TQ_B:(qi + 1) * TQ_B, cols] = m_pair + jnp.log2(l_pair)

    s_cur = logits(*units[0])
    for i, unit in enumerate(units):
        s_next = logits(*units[i + 1]) if i + 1 < len(units) else None
        finish(*unit, s_cur)
        s_cur = s_next


def _attn_b(q, k, v, window, dil):
    nb, length, _ = q.shape
    half_window = (window // 2) // dil
    assert half_window * 2 + TQ_B == KW_B and length >= KW_B
    tj = min(B_ROWS_PER_STEP, length)
    ncls = min(B_ROWS_PER_STEP // tj, dil)
    qspec = pl.BlockSpec((None, tj, ncls * B_W), lambda b, r, j: (b, j, r))
    kvspec = pl.BlockSpec((None, length, ncls * B_W), lambda b, r, j: (b, 0, r))
    return pl.pallas_call(
        functools.partial(_attn_b_kernel, half_window=half_window),
        grid=(nb, dil // ncls, length // tj),
        in_specs=[qspec, kvspec, kvspec],
        out_specs=[qspec, qspec],
        out_shape=[jax.ShapeDtypeStruct((nb, length, dil * B_W), BF16),
                   jax.ShapeDtypeStruct((nb, length, dil * B_W), F32)],
        compiler_params=_cparams(("arbitrary", "arbitrary", "arbitrary")),
        name=f"attn_b_d{dil}",
    )(q, k, v)


def _load_class_major(blk_ref, stage_ref, dil, sub):
    rows = CHUNK // dil
    blk_rows = slice(sub * rows, (sub + 1) * rows)
    if dil == 1:
        return blk_ref[blk_rows, :].astype(F32)
    halves = range(B_W // V7X_LANES)
    for r in range(dil):
        for half in halves:
            col = r * B_W + half * V7X_LANES
            stage_ref[sub, half, pl.ds(r, rows, stride=dil), :] = (
                blk_ref[blk_rows, col:col + V7X_LANES].astype(F32))
    return jnp.concatenate([stage_ref[sub, half] for half in halves], axis=1)


def _mix_mlp_kernel(x_ref, ada_ref, oa_ref, ob0_ref, ob1_ref, ob2_ref, ls0_ref, ls1_ref, ls2_ref,
                    ga_ref, gb_ref, wa_ref, wb_ref, wo_ref, g_ref, w1_ref, w2_ref, xo_ref, *stage_refs):
    dils = [d for _, d in B_GROUPS]
    for sub in range(TM_MIX // CHUNK):
        tok = slice(sub * CHUNK, (sub + 1) * CHUNK)
        ya = jnp.dot(oa_ref[tok, :], wa_ref[...], preferred_element_type=F32)
        lses = [_load_class_major(r, stage_refs[2 * g], dils[g], sub)
                for g, r in enumerate((ls0_ref, ls1_ref, ls2_ref))]
        outs = [_load_class_major(r, stage_refs[2 * g + 1], dils[g], sub)
                for g, r in enumerate((ob0_ref, ob1_ref, ob2_ref))]
        top = jnp.maximum(jnp.maximum(lses[0], lses[1]), lses[2])
        es = [jnp.exp2(v - top) for v in lses]
        den = es[0] + es[1] + es[2]
        ob = (es[0] * outs[0] + es[1] * outs[1] + es[2] * outs[2]) / den
        yb = jnp.dot(ob.astype(BF16), wb_ref[...], preferred_element_type=F32)
        mixed = ga_ref[tok, :].astype(F32) * ya + gb_ref[tok, :].astype(F32) * yb
        upd = jnp.dot(mixed.astype(BF16), wo_ref[...], preferred_element_type=F32)
        xo_ref[tok, :] = x_ref[tok, :] + ada_ref[2:3, :] * upd

    x = xo_ref[...]
    h = _modulated_rms(x, g_ref[...], ada_ref[4:5, :], ada_ref[3:4, :]).astype(BF16)
    acc = jnp.zeros(x.shape, F32)
    for c in range(D_FF // FF_CHUNK):
        a = jnp.dot(h, w1_ref[:, c * FF_CHUNK:(c + 1) * FF_CHUNK], preferred_element_type=F32)
        a = jnp.square(jnp.maximum(a, 0.0)).astype(BF16)
        acc = acc + jnp.dot(a, w2_ref[c * FF_CHUNK:(c + 1) * FF_CHUNK, :], preferred_element_type=F32)
    xo_ref[...] = x + ada_ref[5:6, :] * acc


def _mix_mlp(x, ada_l, oa, obs, lss, ga, gb, wa, wb, wo, g_mlp_l, w1, w2, l):
    nb, seq, _ = x.shape
    tok = lambda w: pl.BlockSpec((None, TM_MIX, w), lambda b, s: (b, s, 0))
    wspec = lambda r, c: pl.BlockSpec((None, r, c), lambda b, s: (l, 0, 0), pipeline_mode=pl.Buffered(1))
    cls = [pl.BlockSpec((None, TM_MIX // d, d * B_W), lambda b, s: (b, s, 0)) for _, d in B_GROUPS]
    return pl.pallas_call(
        _mix_mlp_kernel,
        grid=(nb, seq // TM_MIX),
        in_specs=[tok(D_MODEL), pl.BlockSpec((None, 6, D_MODEL), lambda b, s: (b, 0, 0)), tok(A_Q_W)]
                 + cls + cls + [tok(D_MODEL)] * 2
                 + [wspec(A_Q_W, D_MODEL), wspec(B_W, D_MODEL), wspec(D_MODEL, D_MODEL),
                    pl.BlockSpec((None, 1, D_MODEL), lambda b, s: (l, 0, 0)),
                    wspec(D_MODEL, D_FF), wspec(D_FF, D_MODEL)],
        out_specs=tok(D_MODEL),
        out_shape=jax.ShapeDtypeStruct(x.shape, F32),
        scratch_shapes=[pltpu.VMEM((TM_MIX // CHUNK, B_W // V7X_LANES, CHUNK, V7X_LANES), F32)] * (2 * N_B),
        compiler_params=_cparams(("arbitrary", "arbitrary")),
        name="mix_mlp",
    )(x, ada_l, oa, *obs, *lss, ga, gb, wa, wb, wo, g_mlp_l, w1, w2)


def _rope_tables(seq):
    pos = jnp.arange(seq)
    rows = seq // GRID_W
    row_idx = jnp.broadcast_to(jnp.arange(rows)[:, None], (rows, GRID_W)).reshape(-1)
    col_idx = jnp.broadcast_to(jnp.arange(GRID_W)[None, :], (rows, GRID_W)).reshape(-1)

    def cos_sin(p, dim):
        inv = ROPE_THETA ** (-jnp.arange(0, dim, 2, dtype=F32) / dim)
        ang = inv[:, None] * p.astype(F32)[None, :]
        return jnp.cos(ang), jnp.sin(ang)

    rc, rs = cos_sin(row_idx, HEAD_DIM // 2)
    cc, cs = cos_sin(col_idx, HEAD_DIM // 2)
    sc, ss = cos_sin(pos, HEAD_DIM)
    return jnp.concatenate([rc, rc, cc, cc, -rs, rs, -cs, cs, sc, sc, -ss, ss], axis=0)


def _gain_columns(q_norm_a, k_norm_a, q_norm_b, k_norm_b):
    scale = HEAD_DIM ** -0.5 * LOG2_E

    def pair(gain, perm, mult):
        return [gain * mult, gain[:, perm] * mult]

    cols = pair(q_norm_a, PERM_AXIAL, scale) + pair(k_norm_a, PERM_AXIAL, 1.0)
    for g in range(N_B):
        cols += pair(q_norm_b[:, g], PERM_SEQ, scale) + pair(k_norm_b[:, g], PERM_SEQ, 1.0)
    cols = jnp.concatenate(cols, axis=1)
    return jnp.broadcast_to(cols[:, :, None], cols.shape + (V7X_LANES,))


def _wt_kernel(w_ref, o_ref):
    o_ref[...] = w_ref[...].T.astype(BF16)


def _wt_col_block(j):
    return j + (j >= 5).astype(jnp.int32) + (j >= 7).astype(jnp.int32)


def _split_w_in(w_in):
    depth = w_in.shape[0]
    sizes = [A_Q_W, A_KV_W, A_KV_W] + [B_W] * (3 * N_B) + [D_MODEL, D_MODEL]
    offs = np.concatenate([[0], np.cumsum(sizes)])
    names = ["qa", "ka", "va"] + [f"{t}b{g}" for g in range(N_B) for t in "qkv"] + ["ga", "gb"]
    parts = {n: w_in[:, :, offs[i]:offs[i + 1]] for i, n in enumerate(names)}
    wn = jnp.concatenate([parts["vb0"], parts["vb1"], parts["vb2"], parts["ga"], parts["gb"]], axis=-1)
    wt = pl.pallas_call(
        _wt_kernel,
        grid=(depth, T_ROWS // B_W),
        in_specs=[pl.BlockSpec((None, D_MODEL, B_W), lambda l, j: (l, 0, _wt_col_block(j)))],
        out_specs=pl.BlockSpec((None, B_W, D_MODEL), lambda l, j: (l, j, 0)),
        out_shape=jax.ShapeDtypeStruct((depth, T_ROWS, D_MODEL), BF16),
        compiler_params=_cparams(("arbitrary", "arbitrary")),
        name="w_in_t",
    )(w_in)
    return wn.astype(BF16), wt


def kernel(x, c, w_ada, b_ada, g_mix, g_mlp, w_in, q_norm_a, k_norm_a, q_norm_b, k_norm_b,
           w_branch_a, w_branch_b, w_out, w_ff1, w_ff2):
    nb, seq, _ = x.shape
    depth = w_in.shape[0]
    assert seq % GRID_W == 0 and seq % TM_MIX == 0 and seq % TM_IN == 0

    ada = _ada_all(c, w_ada, b_ada).reshape(depth, nb, 6, D_MODEL)
    rope = _rope_tables(seq)
    gains = _gain_columns(q_norm_a, k_norm_a, q_norm_b, k_norm_b)
    wn, wt = _split_w_in(w_in)
    wa, wb, wo = w_branch_a.astype(BF16), w_branch_b.astype(BF16), w_out.astype(BF16)
    w1, w2 = w_ff1.astype(BF16), w_ff2.astype(BF16)
    g_mix = g_mix.reshape(depth, 1, D_MODEL)
    g_mlp = g_mlp.reshape(depth, 1, D_MODEL)

    for l in range(depth):
        (qat, ka, vat, qb0, kb0, vb0, qb1, kb1, vb1, qb2, kb2, vb2, ga, gb) = _in_proj(
            x, ada[l], g_mix, wn, wt, rope, gains, l)
        oa = _attn_a(qat, ka, vat)
        obs, lss = [], []
        for (window, dil), q, k, v in zip(B_GROUPS, (qb0, qb1, qb2), (kb0, kb1, kb2), (vb0, vb1, vb2)):
            o, lse = _attn_b(q, k, v, window, dil)
            obs.append(o)
            lss.append(lse)
        x = _mix_mlp(x, ada[l], oa, obs, lss, ga, gb, wa, wb, wo, g_mlp, w1, w2, l)
    return x
```

```python
import functools

import jax
import jax.numpy as jnp
import numpy as np
from jax import lax
from jax.experimental import pallas as pl
from jax.experimental.pallas import tpu as pltpu

D_MODEL = 1024
HEAD_DIM = 64
A_Q_HEADS = 8
A_KV_HEADS = 2
B_GROUPS = ((128, 1), (512, 4), (2048, 16))
B_HEADS = 4
N_B = len(B_GROUPS)
D_FF = 4 * D_MODEL
GRID_W = 64
ROPE_THETA = 10000.0
EPS = 1e-6

A_Q_W = A_Q_HEADS * HEAD_DIM
A_KV_W = A_KV_HEADS * HEAD_DIM
B_W = B_HEADS * HEAD_DIM

V7X_LANES = 128
V7X_VMEM_BYTES = 64 * 1024 * 1024
VMEM_LIMIT = V7X_VMEM_BYTES * 13 // 16

BF16 = jnp.bfloat16
F32 = jnp.float32

CHUNK = 256
TM_IN = 1024
TQ_A = 2048
A_PAIRS_PER_BODY = 3
TM_MIX = 512
FF_CHUNK = 1024
B_ROWS_PER_STEP = 4096
TQ_B = 128
KW_B = 256
NEG_BIG = -1e30
LOG2_E = 1.4426950408889634

PERM_AXIAL = np.concatenate([np.arange(16, 32), np.arange(0, 16), np.arange(48, 64), np.arange(32, 48)])
PERM_SEQ = np.concatenate([np.arange(32, 64), np.arange(0, 32)])

T_SEGS = (("qa", A_Q_W), ("ka", A_KV_W), ("va", A_KV_W),
          ("qb0", B_W), ("kb0", B_W), ("qb1", B_W), ("kb1", B_W), ("qb2", B_W), ("kb2", B_W))
T_ROWS = sum(r for _, r in T_SEGS)
N_COLS = N_B * B_W + 2 * D_MODEL
N_TABS = 16


def _cparams(sem):
    return pltpu.CompilerParams(dimension_semantics=sem, vmem_limit_bytes=VMEM_LIMIT)


def _ada_kernel(c_ref, w_ref, b_ref, o_ref):
    c = c_ref[...]
    c_act = c * (1.0 / (1.0 + jnp.exp(-c)))
    o_ref[...] = jnp.dot(c_act, w_ref[...], preferred_element_type=F32,
                         precision=lax.Precision.HIGHEST) + b_ref[...]


def _ada_all(c, w_ada, b_ada):
    depth = w_ada.shape[0]
    nb = c.shape[0]
    return pl.pallas_call(
        _ada_kernel,
        grid=(depth, 6),
        in_specs=[
            pl.BlockSpec((nb, D_MODEL), lambda l, j: (0, 0)),
            pl.BlockSpec((None, D_MODEL, D_MODEL), lambda l, j: (l, 0, j)),
            pl.BlockSpec((None, 1, D_MODEL), lambda l, j: (l, 0, j)),
        ],
        out_specs=pl.BlockSpec((None, nb, D_MODEL), lambda l, j: (l, 0, j)),
        out_shape=jax.ShapeDtypeStruct((depth, nb, 6 * D_MODEL), F32),
        compiler_params=_cparams(("arbitrary", "arbitrary")),
        name="ada",
    )(c, w_ada, b_ada.reshape(depth, 1, 6 * D_MODEL))


def _modulated_rms(x, g, scale, shift):
    ms = jnp.mean(x * x, axis=-1, keepdims=True)
    return (x * lax.rsqrt(ms + EPS)) * g * (1.0 + scale) + shift


def _norm_rope_t(q, cos_t, sin_t, perm_blocks):
    ss = jnp.sum(q * q, axis=0, keepdims=True)
    r = lax.rsqrt(ss * (1.0 / HEAD_DIM) + EPS)
    qp = jnp.concatenate([q[a:b] for a, b in perm_blocks], axis=0)
    return (q * cos_t + qp * sin_t) * r


AXIAL_BLOCKS = ((16, 32), (0, 16), (48, 64), (32, 48))
SEQ_BLOCKS = ((32, 64), (0, 32))


def _store_class_major(out_ref, stage_ref, tile, dil, sub):
    rows = tile.shape[0] // dil
    out_rows = slice(sub * rows, (sub + 1) * rows)
    if dil == 1:
        out_ref[out_rows, :] = tile.astype(BF16)
        return
    for half in range(B_W // V7X_LANES):
        stage_ref[sub, half] = tile[:, half * V7X_LANES:(half + 1) * V7X_LANES]
    for r in range(dil):
        for half in range(B_W // V7X_LANES):
            col = r * B_W + half * V7X_LANES
            out_ref[out_rows, col:col + V7X_LANES] = (
                stage_ref[sub, half, pl.ds(r, rows, stride=dil), :].astype(BF16))


def _in_kernel(x_ref, ada_ref, g_ref, wn_ref, wt_ref, rope_ref, gain_ref,
               qat_ref, ka_ref, vat_ref,
               qb0_ref, kb0_ref, vb0_ref, qb1_ref, kb1_ref, vb1_ref, qb2_ref, kb2_ref, vb2_ref,
               ga_ref, gb_ref, stage_q_ref, stage_k_ref, stage_v_ref):
    for sub in range(TM_IN // CHUNK):
        _in_subtile(sub, x_ref, ada_ref, g_ref, wn_ref, wt_ref, rope_ref, gain_ref, qat_ref, ka_ref, vat_ref,
                    (qb0_ref, qb1_ref, qb2_ref), (kb0_ref, kb1_ref, kb2_ref), (vb0_ref, vb1_ref, vb2_ref),
                    ga_ref, gb_ref, stage_q_ref, stage_k_ref, stage_v_ref)


def _in_subtile(sub, x_ref, ada_ref, g_ref, wn_ref, wt_ref, rope_ref, gain_ref, qat_ref, ka_ref, vat_ref,
                qb_refs, kb_refs, vb_refs, ga_ref, gb_ref, stage_q_ref, stage_k_ref, stage_v_ref):
    tok = slice(sub * CHUNK, (sub + 1) * CHUNK)
    x = x_ref[tok, :]
    h = _modulated_rms(x, g_ref[...], ada_ref[1:2, :], ada_ref[0:1, :]).astype(BF16)

    nt = (((1,), (1,)), ((), ()))

    def proj_t(row0, rows):
        return lax.dot_general(wt_ref[row0:row0 + rows, :], h, nt, preferred_element_type=F32)

    def gained(tab, base_row):
        gain = gain_ref[tab * HEAD_DIM:(tab + 1) * HEAD_DIM, :]
        gain = jnp.concatenate([gain] * (CHUNK // V7X_LANES), axis=1)
        return rope_ref[base_row * HEAD_DIM:(base_row + 1) * HEAD_DIM, tok] * gain

    def heads_t(rt, n_heads, tab0, blocks):
        base_row = 0 if blocks is AXIAL_BLOCKS else 2
        cos_t = gained(tab0, base_row)
        sin_t = gained(tab0 + 1, base_row + 1)
        return [_norm_rope_t(rt[i * HEAD_DIM:(i + 1) * HEAD_DIM], cos_t, sin_t, blocks)
                for i in range(n_heads)]

    row = 0
    rt = proj_t(row, A_Q_W)
    qa = heads_t(rt, A_Q_HEADS, 0, AXIAL_BLOCKS)
    for i in range(A_Q_HEADS):
        qat_ref[sub, i * HEAD_DIM:(i + 1) * HEAD_DIM, :] = qa[i].astype(BF16)
    row += A_Q_W

    rt = proj_t(row, 2 * A_KV_W)
    ka = heads_t(rt[:A_KV_W], A_KV_HEADS, 2, AXIAL_BLOCKS)
    ka_ref[tok, :] = jnp.concatenate(ka, axis=0).T.astype(BF16)
    vat_ref[sub] = rt[A_KV_W:].astype(BF16)
    row += 2 * A_KV_W

    for g in range(N_B):
        rt = proj_t(row, 2 * B_W)
        qb = heads_t(rt[:B_W], B_HEADS, 4 + 4 * g, SEQ_BLOCKS)
        kb = heads_t(rt[B_W:], B_HEADS, 6 + 4 * g, SEQ_BLOCKS)
        _store_class_major(qb_refs[g], stage_q_ref, jnp.concatenate(qb, axis=0).T, B_GROUPS[g][1], sub)
        _store_class_major(kb_refs[g], stage_k_ref, jnp.concatenate(kb, axis=0).T, B_GROUPS[g][1], sub)
        row += 2 * B_W

    for i, gate_ref in enumerate((ga_ref, gb_ref)):
        lo = N_B * B_W + i * D_MODEL
        z = jnp.dot(h, wn_ref[:, lo:lo + D_MODEL], preferred_element_type=F32)
        gate_ref[tok, :] = (1.0 / (1.0 + jnp.exp(-z))).astype(BF16)
    for g in reversed(range(N_B)):
        v = jnp.dot(h, wn_ref[:, g * B_W:(g + 1) * B_W], preferred_element_type=F32)
        _store_class_major(vb_refs[g], stage_v_ref, v, B_GROUPS[g][1], sub)


def _in_proj(x, ada_l, g_mix_l, wn, wt, rope, gains, l):
    nb, seq, _ = x.shape
    ns = seq // TM_IN
    nsub = TM_IN // CHUNK
    tok = lambda w: pl.BlockSpec((None, TM_IN, w), lambda s, b: (b, s, 0))
    tchunk = lambda r: pl.BlockSpec((None, nsub, r, CHUNK), lambda s, b: (b, s, 0, 0))
    nat = lambda w: jax.ShapeDtypeStruct((nb, seq, w), BF16)
    out_specs = [tchunk(A_Q_W), tok(A_KV_W), tchunk(A_KV_W)]
    out_shape = [jax.ShapeDtypeStruct((nb, seq // CHUNK, A_Q_W, CHUNK), BF16), nat(A_KV_W),
                 jax.ShapeDtypeStruct((nb, seq // CHUNK, A_KV_W, CHUNK), BF16)]
    for _, dil in B_GROUPS:
        out_specs += [pl.BlockSpec((None, TM_IN // dil, dil * B_W), lambda s, b: (b, s, 0))] * 3
        out_shape += [jax.ShapeDtypeStruct((nb, seq // dil, dil * B_W), BF16)] * 3
    out_specs += [tok(D_MODEL)] * 2
    out_shape += [nat(D_MODEL)] * 2
    return pl.pallas_call(
        _in_kernel,
        grid=(ns, nb),
        in_specs=[
            tok(D_MODEL),
            pl.BlockSpec((None, 6, D_MODEL), lambda s, b: (b, 0, 0)),
            pl.BlockSpec((None, 1, D_MODEL), lambda s, b: (l, 0, 0)),
            pl.BlockSpec((None, D_MODEL, N_COLS), lambda s, b: (l, 0, 0), pipeline_mode=pl.Buffered(1)),
            pl.BlockSpec((None, T_ROWS, D_MODEL), lambda s, b: (l, 0, 0), pipeline_mode=pl.Buffered(1)),
            pl.BlockSpec((4 * HEAD_DIM, TM_IN), lambda s, b: (0, s)),
            pl.BlockSpec((None, N_TABS * HEAD_DIM, V7X_LANES), lambda s, b: (l, 0, 0)),
        ],
        out_specs=out_specs,
        out_shape=out_shape,
        scratch_shapes=[pltpu.VMEM((nsub, B_W // V7X_LANES, CHUNK, V7X_LANES), F32)] * 3,
        compiler_params=_cparams(("arbitrary", "arbitrary")),
        name="in_proj",
    )(x, ada_l, g_mix_l, wn, wt, rope, gains)


A_REP = A_Q_HEADS // A_KV_HEADS
ONES_ROWS = 16


def _attn_a_kernel(qt_ref, k_ref, vt_ref, o_ref, qpad_ref, acc_ref, m_ref, s_ref, smax_ref):
    for sub in range(qt_ref.shape[0]):
        _attn_a_tile(qt_ref.at[sub], k_ref, vt_ref, o_ref.at[sub * CHUNK:(sub + 1) * CHUNK],
                     qpad_ref.at[sub], acc_ref.at[sub], m_ref.at[sub], s_ref, smax_ref)


def _attn_a_tile(qt_ref, k_ref, vt_ref, o_ref, qpad_ref, acc_ref, m_ref, s_ref, smax_ref):
    n_chunks, _, tk = vt_ref.shape
    tq = qt_ref.shape[1]
    qpad_ref[...] = jnp.zeros(qpad_ref.shape, BF16)
    for g in range(A_KV_HEADS):
        for hh in range(A_REP):
            h = g * A_REP + hh
            qpad_ref[g, g * HEAD_DIM:(g + 1) * HEAD_DIM, hh * tq:(hh + 1) * tq] = (
                qt_ref[h * HEAD_DIM:(h + 1) * HEAD_DIM, :])
    m_ref[...] = jnp.full(m_ref.shape, NEG_BIG, F32)
    acc_ref[...] = jnp.zeros(acc_ref.shape, F32)
    ones = jnp.ones((ONES_ROWS, tk), BF16)

    def scores(c, slot, g, hh):
        lanes = slice(hh * tq, (hh + 1) * tq)
        kc = k_ref[pl.ds(pl.multiple_of(c * tk, tk), tk), :]
        s = jnp.dot(kc, qpad_ref[g, :, lanes], preferred_element_type=F32)
        s_ref[slot, g, :, lanes] = s
        smax_ref[slot, g, :, lanes] = jnp.max(s, axis=0, keepdims=True)

    def consume(c, slot, g, hh):
        lanes = slice(hh * tq, (hh + 1) * tq)
        m_prev = m_ref[g, :, lanes]
        m_new = jnp.maximum(m_prev, smax_ref[slot, g, :, lanes])
        alpha = jnp.exp2(m_prev - m_new)
        p = jnp.exp2(s_ref[slot, g, :, lanes] - m_new).astype(BF16)
        v_aug = jnp.concatenate([vt_ref[c, g * HEAD_DIM:(g + 1) * HEAD_DIM, :], ones], axis=0)
        acc_ref[g, :, lanes] = acc_ref[g, :, lanes] * alpha + jnp.dot(v_aug, p, preferred_element_type=F32)
        m_ref[g, :, lanes] = m_new

    for g in range(A_KV_HEADS):
        for hh in range(A_REP):
            scores(0, 0, g, hh)

    def chunk_pair(c, last):
        for slot in range(2):
            for g in range(A_KV_HEADS):
                for hh in range(A_REP):
                    if not (last and slot == 1):
                        scores(c + slot + 1, 1 - slot, g, hh)
                    consume(c + slot, slot, g, hh)

    body_chunks = 2 * A_PAIRS_PER_BODY

    def loop_body(i, carry):
        for j in range(A_PAIRS_PER_BODY):
            chunk_pair(body_chunks * i + 2 * j, False)
        return carry

    n_loop = (n_chunks - 2) // body_chunks
    lax.fori_loop(0, n_loop, loop_body, 0)
    for c in range(n_loop * body_chunks, n_chunks, 2):
        chunk_pair(c, c == n_chunks - 2)
    for j in range(A_Q_W // V7X_LANES):
        halves = []
        for h in (2 * j, 2 * j + 1):
            g, hh = divmod(h, A_REP)
            cols = slice(hh * tq, (hh + 1) * tq)
            halves.append(acc_ref[g, :HEAD_DIM, cols] / acc_ref[g, HEAD_DIM:HEAD_DIM + 1, cols])
        blk = jnp.concatenate(halves, axis=0)
        o_ref[:, j * V7X_LANES:(j + 1) * V7X_LANES] = blk.T.astype(BF16)


def _attn_a(qat, ka, vat):
    nb, ns, _, chunk = qat.shape
    seq = ka.shape[1]
    nq = TQ_A // chunk
    wide = A_REP * chunk
    return pl.pallas_call(
        _attn_a_kernel,
        grid=(nb, seq // TQ_A),
        in_specs=[
            pl.BlockSpec((None, nq, A_Q_W, chunk), lambda b, s: (b, s, 0, 0)),
            pl.BlockSpec((None, seq, A_KV_W), lambda b, s: (b, 0, 0)),
            pl.BlockSpec((None, ns, A_KV_W, chunk), lambda b, s: (b, 0, 0, 0)),
        ],
        out_specs=pl.BlockSpec((None, TQ_A, A_Q_W), lambda b, s: (b, s, 0)),
        out_shape=jax.ShapeDtypeStruct((nb, seq, A_Q_W), BF16),
        scratch_shapes=[pltpu.VMEM((nq, A_KV_HEADS, A_KV_W, wide), BF16),
                        pltpu.VMEM((nq, A_KV_HEADS, HEAD_DIM + ONES_ROWS, wide), F32),
                        pltpu.VMEM((nq, A_KV_HEADS, 1, wide), F32),
                        pltpu.VMEM((2, A_KV_HEADS, chunk, wide), F32),
                        pltpu.VMEM((2, A_KV_HEADS, 1, wide), F32)],
        compiler_params=_cparams(("arbitrary", "arbitrary")),
        name="attn_a",
    )(qat, ka, vat)


def _attn_b_kernel(q_ref, k_ref, v_ref, o_ref, lse_ref, *, half_window):
    tj = q_ref.shape[0]
    length = k_ref.shape[0]
    j_tile = pl.program_id(2) * tj
    lane = lax.broadcasted_iota(jnp.int32, (1, V7X_LANES), 1)
    first_head = lane < HEAD_DIM
    nt = (((1,), (1,)), ((), ()))
    ones = jnp.ones((KW_B, V7X_LANES), BF16)
    n_pairs = q_ref.shape[1] // V7X_LANES
    units = [(qi, pair) for qi in range(tj // TQ_B) for pair in range(n_pairs)]
    window = {}

    def key_window(qi):
        if qi not in window:
            j0 = j_tile + qi * TQ_B
            ks = pl.multiple_of(jnp.clip(j0 - half_window, 0, length - KW_B), half_window)
            jq = j0 + lax.broadcasted_iota(jnp.int32, (TQ_B, 1), 0)
            jk = ks + lax.broadcasted_iota(jnp.int32, (1, KW_B), 1)
            bias = jnp.where(jnp.abs(jq - jk) <= half_window, 0.0, NEG_BIG)
            window[qi] = ks, jnp.concatenate([bias, bias], axis=0)
        return window[qi]

    def logits(qi, pair):
        ks, bias2 = key_window(qi)
        cols = slice(pair * V7X_LANES, (pair + 1) * V7X_LANES)
        q2 = q_ref[qi * TQ_B:(qi + 1) * TQ_B, cols]
        k2 = k_ref[pl.ds(ks, KW_B), cols]
        zero = jnp.zeros_like(q2)
        qs = jnp.concatenate([jnp.where(first_head, q2, zero), jnp.where(first_head, zero, q2)], axis=0)
        return lax.dot_general(qs, k2, nt, preferred_element_type=F32) + bias2

    def finish(qi, pair, s):
        ks, _ = key_window(qi)
        cols = slice(pair * V7X_LANES, (pair + 1) * V7X_LANES)
        v_aug = jnp.concatenate([v_ref[pl.ds(ks, KW_B), cols], ones], axis=1)
        m = jnp.max(s, axis=1, keepdims=True)
        p = jnp.exp2(s - m).astype(BF16)
        ol = jnp.dot(p, v_aug, preferred_element_type=F32)
        o_pair = jnp.where(first_head, ol[:TQ_B, :V7X_LANES], ol[TQ_B:, :V7X_LANES])
        l_pair = jnp.where(first_head, ol[:TQ_B, V7X_LANES:], ol[TQ_B:, V7X_LANES:])
        m_pair = jnp.where(first_head, m[:TQ_B], m[TQ_B:])
        o_ref[qi * TQ_B:(qi + 1) * TQ_B, cols] = (o_pair / l_pair).astype(BF16)
        lse_ref[qi * TQ_B:(qi + 1) * TQ_B, cols] = m_pair + jnp.log2(l_pair)

    s_cur = logits(*units[0])
    for i, unit in enumerate(units):
        s_next = logits(*units[i + 1]) if i + 1 < len(units) else None
        finish(*unit, s_cur)
        s_cur = s_next


def _attn_b(q, k, v, window, dil):
    nb, length, _ = q.shape
    half_window = (window // 2) // dil
    assert half_window * 2 + TQ_B == KW_B and length >= KW_B
    tj = min(B_ROWS_PER_STEP, length)
    ncls = min(B_ROWS_PER_STEP // tj, dil)
    qspec = pl.BlockSpec((None, tj, ncls * B_W), lambda b, r, j: (b, j, r))
    kvspec = pl.BlockSpec((None, length, ncls * B_W), lambda b, r, j: (b, 0, r))
    return pl.pallas_call(
        functools.partial(_attn_b_kernel, half_window=half_window),
        grid=(nb, dil // ncls, length // tj),
        in_specs=[qspec, kvspec, kvspec],
        out_specs=[qspec, qspec],
        out_shape=[jax.ShapeDtypeStruct((nb, length, dil * B_W), BF16),
                   jax.ShapeDtypeStruct((nb, length, dil * B_W), F32)],
        compiler_params=_cparams(("arbitrary", "arbitrary", "arbitrary")),
        name=f"attn_b_d{dil}",
    )(q, k, v)


def _load_class_major(blk_ref, stage_ref, dil, sub):
    rows = CHUNK // dil
    blk_rows = slice(sub * rows, (sub + 1) * rows)
    if dil == 1:
        return blk_ref[blk_rows, :].astype(F32)
    halves = range(B_W // V7X_LANES)
    for r in range(dil):
        for half in halves:
            col = r * B_W + half * V7X_LANES
            stage_ref[sub, half, pl.ds(r, rows, stride=dil), :] = (
                blk_ref[blk_rows, col:col + V7X_LANES].astype(F32))
    return jnp.concatenate([stage_ref[sub, half] for half in halves], axis=1)


def _mix_mlp_kernel(x_ref, ada_ref, oa_ref, ob0_ref, ob1_ref, ob2_ref, ls0_ref, ls1_ref, ls2_ref,
                    ga_ref, gb_ref, wa_ref, wb_ref, wo_ref, g_ref, w1_ref, w2_ref, xo_ref, *stage_refs):
    dils = [d for _, d in B_GROUPS]
    for sub in range(TM_MIX // CHUNK):
        tok = slice(sub * CHUNK, (sub + 1) * CHUNK)
        ya = jnp.dot(oa_ref[tok, :], wa_ref[...], preferred_element_type=F32)
        lses = [_load_class_major(r, stage_refs[2 * g], dils[g], sub)
                for g, r in enumerate((ls0_ref, ls1_ref, ls2_ref))]
        outs = [_load_class_major(r, stage_refs[2 * g + 1], dils[g], sub)
                for g, r in enumerate((ob0_ref, ob1_ref, ob2_ref))]
        top = jnp.maximum(jnp.maximum(lses[0], lses[1]), lses[2])
        es = [jnp.exp2(v - top) for v in lses]
        den = es[0] + es[1] + es[2]
        ob = (es[0] * outs[0] + es[1] * outs[1] + es[2] * outs[2]) / den
        yb = jnp.dot(ob.astype(BF16), wb_ref[...], preferred_element_type=F32)
        mixed = ga_ref[tok, :].astype(F32) * ya + gb_ref[tok, :].astype(F32) * yb
        upd = jnp.dot(mixed.astype(BF16), wo_ref[...], preferred_element_type=F32)
        xo_ref[tok, :] = x_ref[tok, :] + ada_ref[2:3, :] * upd

    x = xo_ref[...]
    h = _modulated_rms(x, g_ref[...], ada_ref[4:5, :], ada_ref[3:4, :]).astype(BF16)
    acc = jnp.zeros(x.shape, F32)
    for c in range(D_FF // FF_CHUNK):
        a = jnp.dot(h, w1_ref[:, c * FF_CHUNK:(c + 1) * FF_CHUNK], preferred_element_type=F32)
        a = jnp.square(jnp.maximum(a, 0.0)).astype(BF16)
        acc = acc + jnp.dot(a, w2_ref[c * FF_CHUNK:(c + 1) * FF_CHUNK, :], preferred_element_type=F32)
    xo_ref[...] = x + ada_ref[5:6, :] * acc


def _mix_mlp(x, ada_l, oa, obs, lss, ga, gb, wa, wb, wo, g_mlp_l, w1, w2, l):
    nb, seq, _ = x.shape
    tok = lambda w: pl.BlockSpec((None, TM_MIX, w), lambda b, s: (b, s, 0))
    wspec = lambda r, c: pl.BlockSpec((None, r, c), lambda b, s: (l, 0, 0), pipeline_mode=pl.Buffered(1))
    cls = [pl.BlockSpec((None, TM_MIX // d, d * B_W), lambda b, s: (b, s, 0)) for _, d in B_GROUPS]
    return pl.pallas_call(
        _mix_mlp_kernel,
        grid=(nb, seq // TM_MIX),
        in_specs=[tok(D_MODEL), pl.BlockSpec((None, 6, D_MODEL), lambda b, s: (b, 0, 0)), tok(A_Q_W)]
                 + cls + cls + [tok(D_MODEL)] * 2
                 + [wspec(A_Q_W, D_MODEL), wspec(B_W, D_MODEL), wspec(D_MODEL, D_MODEL),
                    pl.BlockSpec((None, 1, D_MODEL), lambda b, s: (l, 0, 0)),
                    wspec(D_MODEL, D_FF), wspec(D_FF, D_MODEL)],
        out_specs=tok(D_MODEL),
        out_shape=jax.ShapeDtypeStruct(x.shape, F32),
        scratch_shapes=[pltpu.VMEM((TM_MIX // CHUNK, B_W // V7X_LANES, CHUNK, V7X_LANES), F32)] * (2 * N_B),
        compiler_params=_cparams(("arbitrary", "arbitrary")),
        name="mix_mlp",
    )(x, ada_l, oa, *obs, *lss, ga, gb, wa, wb, wo, g_mlp_l, w1, w2)


def _rope_tables(seq):
    pos = jnp.arange(seq)
    rows = seq // GRID_W
    row_idx = jnp.broadcast_to(jnp.arange(rows)[:, None], (rows, GRID_W)).reshape(-1)
    col_idx = jnp.broadcast_to(jnp.arange(GRID_W)[None, :], (rows, GRID_W)).reshape(-1)

    def cos_sin(p, dim):
        inv = ROPE_THETA ** (-jnp.arange(0, dim, 2, dtype=F32) / dim)
        ang = inv[:, None] * p.astype(F32)[None, :]
        return jnp.cos(ang), jnp.sin(ang)

    rc, rs = cos_sin(row_idx, HEAD_DIM // 2)
    cc, cs = cos_sin(col_idx, HEAD_DIM // 2)
    sc, ss = cos_sin(pos, HEAD_DIM)
    return jnp.concatenate([rc, rc, cc, cc, -rs, rs, -cs, cs, sc, sc, -ss, ss], axis=0)


def _gain_columns(q_norm_a, k_norm_a, q_norm_b, k_norm_b):
    scale = HEAD_DIM ** -0.5 * LOG2_E

    def pair(gain, perm, mult):
        return [gain * mult, gain[:, perm] * mult]

    cols = pair(q_norm_a, PERM_AXIAL, scale) + pair(k_norm_a, PERM_AXIAL, 1.0)
    for g in range(N_B):
        cols += pair(q_norm_b[:, g], PERM_SEQ, scale) + pair(k_norm_b[:, g], PERM_SEQ, 1.0)
    cols = jnp.concatenate(cols, axis=1)
    return jnp.broadcast_to(cols[:, :, None], cols.shape + (V7X_LANES,))


def _wt_kernel(w_ref, o_ref):
    o_ref[...] = w_ref[...].T.astype(BF16)


def _wt_col_block(j):
    return j + (j >= 5).astype(jnp.int32) + (j >= 7).astype(jnp.int32)


def _split_w_in(w_in):
    depth = w_in.shape[0]
    sizes = [A_Q_W, A_KV_W, A_KV_W] + [B_W] * (3 * N_B) + [D_MODEL, D_MODEL]
    offs = np.concatenate([[0], np.cumsum(sizes)])
    names = ["qa", "ka", "va"] + [f"{t}b{g}" for g in range(N_B) for t in "qkv"] + ["ga", "gb"]
    parts = {n: w_in[:, :, offs[i]:offs[i + 1]] for i, n in enumerate(names)}
    wn = jnp.concatenate([parts["vb0"], parts["vb1"], parts["vb2"], parts["ga"], parts["gb"]], axis=-1)
    wt = pl.pallas_call(
        _wt_kernel,
        grid=(depth, T_ROWS // B_W),
        in_specs=[pl.BlockSpec((None, D_MODEL, B_W), lambda l, j: (l, 0, _wt_col_block(j)))],
        out_specs=pl.BlockSpec((None, B_W, D_MODEL), lambda l, j: (l, j, 0)),
        out_shape=jax.ShapeDtypeStruct((depth, T_ROWS, D_MODEL), BF16),
        compiler_params=_cparams(("arbitrary", "arbitrary")),
        name="w_in_t",
    )(w_in)
    return wn.astype(BF16), wt


def kernel(x, c, w_ada, b_ada, g_mix, g_mlp, w_in, q_norm_a, k_norm_a, q_norm_b, k_norm_b,
           w_branch_a, w_branch_b, w_out, w_ff1, w_ff2):
    nb, seq, _ = x.shape
    depth = w_in.shape[0]
    assert seq % GRID_W == 0 and seq % TM_MIX == 0 and seq % TM_IN == 0

    ada = _ada_all(c, w_ada, b_ada).reshape(depth, nb, 6, D_MODEL)
    rope = _rope_tables(seq)
    gains = _gain_columns(q_norm_a, k_norm_a, q_norm_b, k_norm_b)
    wn, wt = _split_w_in(w_in)
    wa, wb, wo = w_branch_a.astype(BF16), w_branch_b.astype(BF16), w_out.astype(BF16)
    w1, w2 = w_ff1.astype(BF16), w_ff2.astype(BF16)
    g_mix = g_mix.reshape(depth, 1, D_MODEL)
    g_mlp = g_mlp.reshape(depth, 1, D_MODEL)

    for l in range(depth):
        (qat, ka, vat, qb0, kb0, vb0, qb1, kb1, vb1, qb2, kb2, vb2, ga, gb) = _in_proj(
            x, ada[l], g_mix, wn, wt, rope, gains, l)
        oa = _attn_a(qat, ka, vat)
        obs, lss = [], []
        for (window, dil), q, k, v in zip(B_GROUPS, (qb0, qb1, qb2), (kb0, kb1, kb2), (vb0, vb1, vb2)):
            o, lse = _attn_b(q, k, v, window, dil)
            obs.append(o)
            lss.append(lse)
        x = _mix_mlp(x, ada[l], oa, obs, lss, ga, gb, wa, wb, wo, g_mlp, w1, w2, l)
    return x
```

```python
import functools

import jax
import jax.numpy as jnp
import numpy as np
from jax import lax
from jax.experimental import pallas as pl
from jax.experimental.pallas import tpu as pltpu

D_MODEL = 1024
HEAD_DIM = 64
A_Q_HEADS = 8
A_KV_HEADS = 2
B_GROUPS = ((128, 1), (512, 4), (2048, 16))
B_HEADS = 4
N_B = len(B_GROUPS)
D_FF = 4 * D_MODEL
GRID_W = 64
ROPE_THETA = 10000.0
EPS = 1e-6

A_Q_W = A_Q_HEADS * HEAD_DIM
A_KV_W = A_KV_HEADS * HEAD_DIM
B_W = B_HEADS * HEAD_DIM

V7X_LANES = 128
V7X_VMEM_BYTES = 64 * 1024 * 1024
VMEM_LIMIT = V7X_VMEM_BYTES * 13 // 16

BF16 = jnp.bfloat16
F32 = jnp.float32

CHUNK = 256
TM_IN = 1024
TQ_A = 1024
A_PAIRS_PER_BODY = 3
TM_MIX = 512
FF_CHUNK = 1024
B_ROWS_PER_STEP = 4096
TQ_B = 128
KW_B = 256
NEG_BIG = -1e30
LOG2_E = 1.4426950408889634

PERM_AXIAL = np.concatenate([np.arange(16, 32), np.arange(0, 16), np.arange(48, 64), np.arange(32, 48)])
PERM_SEQ = np.concatenate([np.arange(32, 64), np.arange(0, 32)])

T_SEGS = (("qa", A_Q_W), ("ka", A_KV_W), ("va", A_KV_W),
          ("qb0", B_W), ("kb0", B_W), ("qb1", B_W), ("kb1", B_W), ("qb2", B_W), ("kb2", B_W))
T_ROWS = sum(r for _, r in T_SEGS)
N_COLS = N_B * B_W + 2 * D_MODEL
N_TABS = 16


def _cparams(sem):
    return pltpu.CompilerParams(dimension_semantics=sem, vmem_limit_bytes=VMEM_LIMIT)


def _ada_kernel(c_ref, w_ref, b_ref, o_ref):
    c = c_ref[...]
    c_act = c * (1.0 / (1.0 + jnp.exp(-c)))
    o_ref[...] = jnp.dot(c_act, w_ref[...], preferred_element_type=F32,
                         precision=lax.Precision.HIGHEST) + b_ref[...]


def _ada_all(c, w_ada, b_ada):
    depth = w_ada.shape[0]
    nb = c.shape[0]
    return pl.pallas_call(
        _ada_kernel,
        grid=(depth, 6),
        in_specs=[
            pl.BlockSpec((nb, D_MODEL), lambda l, j: (0, 0)),
            pl.BlockSpec((None, D_MODEL, D_MODEL), lambda l, j: (l, 0, j)),
            pl.BlockSpec((None, 1, D_MODEL), lambda l, j: (l, 0, j)),
        ],
        out_specs=pl.BlockSpec((None, nb, D_MODEL), lambda l, j: (l, 0, j)),
        out_shape=jax.ShapeDtypeStruct((depth, nb, 6 * D_MODEL), F32),
        compiler_params=_cparams(("arbitrary", "arbitrary")),
        name="ada",
    )(c, w_ada, b_ada.reshape(depth, 1, 6 * D_MODEL))


def _modulated_rms(x, g, scale, shift):
    ms = jnp.mean(x * x, axis=-1, keepdims=True)
    return (x * lax.rsqrt(ms + EPS)) * g * (1.0 + scale) + shift


def _norm_rope_t(q, cos_t, sin_t, perm_blocks):
    ss = jnp.sum(q * q, axis=0, keepdims=True)
    r = lax.rsqrt(ss * (1.0 / HEAD_DIM) + EPS)
    qp = jnp.concatenate([q[a:b] for a, b in perm_blocks], axis=0)
    return (q * cos_t + qp * sin_t) * r


AXIAL_BLOCKS = ((16, 32), (0, 16), (48, 64), (32, 48))
SEQ_BLOCKS = ((32, 64), (0, 32))


def _store_class_major(out_ref, stage_ref, tile, dil, sub):
    rows = tile.shape[0] // dil
    out_rows = slice(sub * rows, (sub + 1) * rows)
    if dil == 1:
        out_ref[out_rows, :] = tile.astype(BF16)
        return
    for half in range(B_W // V7X_LANES):
        stage_ref[sub, half] = tile[:, half * V7X_LANES:(half + 1) * V7X_LANES]
    for r in range(dil):
        for half in range(B_W // V7X_LANES):
            col = r * B_W + half * V7X_LANES
            out_ref[out_rows, col:col + V7X_LANES] = (
                stage_ref[sub, half, pl.ds(r, rows, stride=dil), :].astype(BF16))


def _in_kernel(x_ref, ada_ref, g_ref, wn_ref, wt_ref, rope_ref, gain_ref,
               qat_ref, ka_ref, vat_ref,
               qb0_ref, kb0_ref, vb0_ref, qb1_ref, kb1_ref, vb1_ref, qb2_ref, kb2_ref, vb2_ref,
               ga_ref, gb_ref, stage_q_ref, stage_k_ref, stage_v_ref):
    for sub in range(TM_IN // CHUNK):
        _in_subtile(sub, x_ref, ada_ref, g_ref, wn_ref, wt_ref, rope_ref, gain_ref, qat_ref, ka_ref, vat_ref,
                    (qb0_ref, qb1_ref, qb2_ref), (kb0_ref, kb1_ref, kb2_ref), (vb0_ref, vb1_ref, vb2_ref),
                    ga_ref, gb_ref, stage_q_ref, stage_k_ref, stage_v_ref)


def _in_subtile(sub, x_ref, ada_ref, g_ref, wn_ref, wt_ref, rope_ref, gain_ref, qat_ref, ka_ref, vat_ref,
                qb_refs, kb_refs, vb_refs, ga_ref, gb_ref, stage_q_ref, stage_k_ref, stage_v_ref):
    tok = slice(sub * CHUNK, (sub + 1) * CHUNK)
    x = x_ref[tok, :]
    h = _modulated_rms(x, g_ref[...], ada_ref[1:2, :], ada_ref[0:1, :]).astype(BF16)

    nt = (((1,), (1,)), ((), ()))

    def proj_t(row0, rows):
        return lax.dot_general(wt_ref[row0:row0 + rows, :], h, nt, preferred_element_type=F32)

    def gained(tab, base_row):
        gain = gain_ref[tab * HEAD_DIM:(tab + 1) * HEAD_DIM, :]
        gain = jnp.concatenate([gain] * (CHUNK // V7X_LANES), axis=1)
        return rope_ref[base_row * HEAD_DIM:(base_row + 1) * HEAD_DIM, tok] * gain

    def heads_t(rt, n_heads, tab0, blocks):
        base_row = 0 if blocks is AXIAL_BLOCKS else 2
        cos_t = gained(tab0, base_row)
        sin_t = gained(tab0 + 1, base_row + 1)
        return [_norm_rope_t(rt[i * HEAD_DIM:(i + 1) * HEAD_DIM], cos_t, sin_t, blocks)
                for i in range(n_heads)]

    row = 0
    rt = proj_t(row, A_Q_W)
    qa = heads_t(rt, A_Q_HEADS, 0, AXIAL_BLOCKS)
    for i in range(A_Q_HEADS):
        qat_ref[sub, i * HEAD_DIM:(i + 1) * HEAD_DIM, :] = qa[i].astype(BF16)
    row += A_Q_W

    rt = proj_t(row, 2 * A_KV_W)
    ka = heads_t(rt[:A_KV_W], A_KV_HEADS, 2, AXIAL_BLOCKS)
    ka_ref[tok, :] = jnp.concatenate(ka, axis=0).T.astype(BF16)
    vat_ref[sub] = rt[A_KV_W:].astype(BF16)
    row += 2 * A_KV_W

    for g in range(N_B):
        rt = proj_t(row, 2 * B_W)
        qb = heads_t(rt[:B_W], B_HEADS, 4 + 4 * g, SEQ_BLOCKS)
        kb = heads_t(rt[B_W:], B_HEADS, 6 + 4 * g, SEQ_BLOCKS)
        _store_class_major(qb_refs[g], stage_q_ref, jnp.concatenate(qb, axis=0).T, B_GROUPS[g][1], sub)
        _store_class_major(kb_refs[g], stage_k_ref, jnp.concatenate(kb, axis=0).T, B_GROUPS[g][1], sub)
        row += 2 * B_W

    for i, gate_ref in enumerate((ga_ref, gb_ref)):
        lo = N_B * B_W + i * D_MODEL
        z = jnp.dot(h, wn_ref[:, lo:lo + D_MODEL], preferred_element_type=F32)
        gate_ref[tok, :] = (1.0 / (1.0 + jnp.exp(-z))).astype(BF16)
    for g in reversed(range(N_B)):
        v = jnp.dot(h, wn_ref[:, g * B_W:(g + 1) * B_W], preferred_element_type=F32)
        _store_class_major(vb_refs[g], stage_v_ref, v, B_GROUPS[g][1], sub)


def _in_proj(x, ada_l, g_mix_l, wn, wt, rope, gains, l):
    nb, seq, _ = x.shape
    ns = seq // TM_IN
    nsub = TM_IN // CHUNK
    tok = lambda w: pl.BlockSpec((None, TM_IN, w), lambda s, b: (b, s, 0))
    tchunk = lambda r: pl.BlockSpec((None, nsub, r, CHUNK), lambda s, b: (b, s, 0, 0))
    nat = lambda w: jax.ShapeDtypeStruct((nb, seq, w), BF16)
    out_specs = [tchunk(A_Q_W), tok(A_KV_W), tchunk(A_KV_W)]
    out_shape = [jax.ShapeDtypeStruct((nb, seq // CHUNK, A_Q_W, CHUNK), BF16), nat(A_KV_W),
                 jax.ShapeDtypeStruct((nb, seq // CHUNK, A_KV_W, CHUNK), BF16)]
    for _, dil in B_GROUPS:
        out_specs += [pl.BlockSpec((None, TM_IN // dil, dil * B_W), lambda s, b: (b, s, 0))] * 3
        out_shape += [jax.ShapeDtypeStruct((nb, seq // dil, dil * B_W), BF16)] * 3
    out_specs += [tok(D_MODEL)] * 2
    out_shape += [nat(D_MODEL)] * 2
    return pl.pallas_call(
        _in_kernel,
        grid=(ns, nb),
        in_specs=[
            tok(D_MODEL),
            pl.BlockSpec((None, 6, D_MODEL), lambda s, b: (b, 0, 0)),
            pl.BlockSpec((None, 1, D_MODEL), lambda s, b: (l, 0, 0)),
            pl.BlockSpec((None, D_MODEL, N_COLS), lambda s, b: (l, 0, 0), pipeline_mode=pl.Buffered(1)),
            pl.BlockSpec((None, T_ROWS, D_MODEL), lambda s, b: (l, 0, 0), pipeline_mode=pl.Buffered(1)),
            pl.BlockSpec((4 * HEAD_DIM, TM_IN), lambda s, b: (0, s)),
            pl.BlockSpec((None, N_TABS * HEAD_DIM, V7X_LANES), lambda s, b: (l, 0, 0)),
        ],
        out_specs=out_specs,
        out_shape=out_shape,
        scratch_shapes=[pltpu.VMEM((nsub, B_W // V7X_LANES, CHUNK, V7X_LANES), F32)] * 3,
        compiler_params=_cparams(("arbitrary", "arbitrary")),
        name="in_proj",
    )(x, ada_l, g_mix_l, wn, wt, rope, gains)


A_REP = A_Q_HEADS // A_KV_HEADS
ONES_ROWS = 16


def _attn_a_kernel(qt_ref, k_ref, vt_ref, o_ref, qpad_ref, acc_ref, m_ref, s_ref, smax_ref):
    for sub in range(qt_ref.shape[0]):
        _attn_a_tile(qt_ref.at[sub], k_ref, vt_ref, o_ref.at[sub * CHUNK:(sub + 1) * CHUNK],
                     qpad_ref.at[sub], acc_ref.at[sub], m_ref.at[sub], s_ref, smax_ref)


def _attn_a_tile(qt_ref, k_ref, vt_ref, o_ref, qpad_ref, acc_ref, m_ref, s_ref, smax_ref):
    n_chunks, _, tk = vt_ref.shape
    tq = qt_ref.shape[1]
    qpad_ref[...] = jnp.zeros(qpad_ref.shape, BF16)
    for g in range(A_KV_HEADS):
        for hh in range(A_REP):
            h = g * A_REP + hh
            qpad_ref[g, g * HEAD_DIM:(g + 1) * HEAD_DIM, hh * tq:(hh + 1) * tq] = (
                qt_ref[h * HEAD_DIM:(h + 1) * HEAD_DIM, :])
    m_ref[...] = jnp.full(m_ref.shape, NEG_BIG, F32)
    acc_ref[...] = jnp.zeros(acc_ref.shape, F32)
    ones = jnp.ones((ONES_ROWS, tk), BF16)

    def scores(c, slot, g, hh):
        lanes = slice(hh * tq, (hh + 1) * tq)
        kc = k_ref[pl.ds(pl.multiple_of(c * tk, tk), tk), :]
        s = jnp.dot(kc, qpad_ref[g, :, lanes], preferred_element_type=F32)
        s_ref[slot, g, :, lanes] = s
        smax_ref[slot, g, :, lanes] = jnp.max(s, axis=0, keepdims=True)

    def consume(c, slot, g, hh):
        lanes = slice(hh * tq, (hh + 1) * tq)
        m_prev = m_ref[g, :, lanes]
        m_new = jnp.maximum(m_prev, smax_ref[slot, g, :, lanes])
        alpha = jnp.exp2(m_prev - m_new)
        p = jnp.exp2(s_ref[slot, g, :, lanes] - m_new).astype(BF16)
        v_aug = jnp.concatenate([vt_ref[c, g * HEAD_DIM:(g + 1) * HEAD_DIM, :], ones], axis=0)
        acc_ref[g, :, lanes] = acc_ref[g, :, lanes] * alpha + jnp.dot(v_aug, p, preferred_element_type=F32)
        m_ref[g, :, lanes] = m_new

    for g in range(A_KV_HEADS):
        for hh in range(A_REP):
            scores(0, 0, g, hh)

    def chunk_pair(c, last):
        for slot in range(2):
            for g in range(A_KV_HEADS):
                for hh in range(A_REP):
                    if not (last and slot == 1):
                        scores(c + slot + 1, 1 - slot, g, hh)
                    consume(c + slot, slot, g, hh)

    body_chunks = 2 * A_PAIRS_PER_BODY

    def loop_body(i, carry):
        for j in range(A_PAIRS_PER_BODY):
            chunk_pair(body_chunks * i + 2 * j, False)
        return carry

    n_loop = (n_chunks - 2) // body_chunks
    lax.fori_loop(0, n_loop, loop_body, 0)
    for c in range(n_loop * body_chunks, n_chunks, 2):
        chunk_pair(c, c == n_chunks - 2)
    for j in range(A_Q_W // V7X_LANES):
        halves = []
        for h in (2 * j, 2 * j + 1):
            g, hh = divmod(h, A_REP)
            cols = slice(hh * tq, (hh + 1) * tq)
            halves.append(acc_ref[g, :HEAD_DIM, cols] / acc_ref[g, HEAD_DIM:HEAD_DIM + 1, cols])
        blk = jnp.concatenate(halves, axis=0)
        o_ref[:, j * V7X_LANES:(j + 1) * V7X_LANES] = blk.T.astype(BF16)


def _attn_a(qat, ka, vat):
    nb, ns, _, chunk = qat.shape
    seq = ka.shape[1]
    nq = TQ_A // chunk
    wide = A_REP * chunk
    return pl.pallas_call(
        _attn_a_kernel,
        grid=(nb, seq // TQ_A),
        in_specs=[
            pl.BlockSpec((None, nq, A_Q_W, chunk), lambda b, s: (b, s, 0, 0)),
            pl.BlockSpec((None, seq, A_KV_W), lambda b, s: (b, 0, 0)),
            pl.BlockSpec((None, ns, A_KV_W, chunk), lambda b, s: (b, 0, 0, 0)),
        ],
        out_specs=pl.BlockSpec((None, TQ_A, A_Q_W), lambda b, s: (b, s, 0)),
        out_shape=jax.ShapeDtypeStruct((nb, seq, A_Q_W), BF16),
        scratch_shapes=[pltpu.VMEM((nq, A_KV_HEADS, A_KV_W, wide), BF16),
                        pltpu.VMEM((nq, A_KV_HEADS, HEAD_DIM + ONES_ROWS, wide), F32),
                        pltpu.VMEM((nq, A_KV_HEADS, 1, wide), F32),
                        pltpu.VMEM((2, A_KV_HEADS, chunk, wide), F32),
                        pltpu.VMEM((2, A_KV_HEADS, 1, wide), F32)],
        compiler_params=_cparams(("arbitrary", "arbitrary")),
        name="attn_a",
    )(qat, ka, vat)


def _attn_b_kernel(q_ref, k_ref, v_ref, o_ref, lse_ref, *, half_window):
    tj = q_ref.shape[0]
    length = k_ref.shape[0]
    j_tile = pl.program_id(2) * tj
    lane = lax.broadcasted_iota(jnp.int32, (1, V7X_LANES), 1)
    first_head = lane < HEAD_DIM
    nt = (((1,), (1,)), ((), ()))
    ones = jnp.ones((KW_B, V7X_LANES), BF16)
    n_pairs = q_ref.shape[1] // V7X_LANES
    units = [(qi, pair) for qi in range(tj // TQ_B) for pair in range(n_pairs)]
    window = {}

    def key_window(qi):
        if qi not in window:
            j0 = j_tile + qi * TQ_B
            ks = pl.multiple_of(jnp.clip(j0 - half_window, 0, length - KW_B), half_window)
            jq = j0 + lax.broadcasted_iota(jnp.int32, (TQ_B, 1), 0)
            jk = ks + lax.broadcasted_iota(jnp.int32, (1, KW_B), 1)
            bias = jnp.where(jnp.abs(jq - jk) <= half_window, 0.0, NEG_BIG)
            window[qi] = ks, jnp.concatenate([bias, bias], axis=0)
        return window[qi]

    def logits(qi, pair):
        ks, bias2 = key_window(qi)
        cols = slice(pair * V7X_LANES, (pair + 1) * V7X_LANES)
        q2 = q_ref[qi * TQ_B:(qi + 1) * TQ_B, cols]
        k2 = k_ref[pl.ds(ks, KW_B), cols]
        zero = jnp.zeros_like(q2)
        qs = jnp.concatenate([jnp.where(first_head, q2, zero), jnp.where(first_head, zero, q2)], axis=0)
        return lax.dot_general(qs, k2, nt, preferred_element_type=F32) + bias2

    def finish(qi, pair, s):
        ks, _ = key_window(qi)
        cols = slice(pair * V7X_LANES, (pair + 1) * V7X_LANES)
        v_aug = jnp.concatenate([v_ref[pl.ds(ks, KW_B), cols], ones], axis=1)
        m = jnp.max(s, axis=1, keepdims=True)
        p = jnp.exp2(s - m).astype(BF16)
        ol = jnp.dot(p, v_aug, preferred_element_type=F32)
        o_pair = jnp.where(first_head, ol[:TQ_B, :V7X_LANES], ol[TQ_B:, :V7X_LANES])
        l_pair = jnp.where(first_head, ol[:TQ_B, V7X_LANES:], ol[TQ_B:, V7X_LANES:])
        m_pair = jnp.where(first_head, m[:TQ_B], m[TQ_B:])
        o_ref[qi * TQ_B:(qi + 1) * TQ_B, cols] = (o_pair / l_pair).astype(BF16)
        lse_ref[qi * TQ_B:(qi + 1) * TQ_B, cols] = m_pair + jnp.log2(l_pair)

    s_cur = logits(*units[0])
    for i, unit in enumerate(units):
        s_next = logits(*units[i + 1]) if i + 1 < len(units) else None
        finish(*unit, s_cur)
        s_cur = s_next


def _attn_b(q, k, v, window, dil):
    nb, length, _ = q.shape
    half_window = (window // 2) // dil
    assert half_window * 2 + TQ_B == KW_B and length >= KW_B
    tj = min(B_ROWS_PER_STEP, length)
    ncls = min(B_ROWS_PER_STEP // tj, dil)
    qspec = pl.BlockSpec((None, tj, ncls * B_W), lambda b, r, j: (b, j, r))
    kvspec = pl.BlockSpec((None, length, ncls * B_W), lambda b, r, j: (b, 0, r))
    return pl.pallas_call(
        functools.partial(_attn_b_kernel, half_window=half_window),
        grid=(nb, dil // ncls, length // tj),
        in_specs=[qspec, kvspec, kvspec],
        out_specs=[qspec, qspec],
        out_shape=[jax.ShapeDtypeStruct((nb, length, dil * B_W), BF16),
                   jax.ShapeDtypeStruct((nb, length, dil * B_W), F32)],
        compiler_params=_cparams(("arbitrary", "arbitrary", "arbitrary")),
        name=f"attn_b_d{dil}",
    )(q, k, v)


def _load_class_major(blk_ref, stage_ref, dil, sub):
    rows = CHUNK // dil
    blk_rows = slice(sub * rows, (sub + 1) * rows)
    if dil == 1:
        return blk_ref[blk_rows, :].astype(F32)
    halves = range(B_W // V7X_LANES)
    for r in range(dil):
        for half in halves:
            col = r * B_W + half * V7X_LANES
            stage_ref[sub, half, pl.ds(r, rows, stride=dil), :] = (
                blk_ref[blk_rows, col:col + V7X_LANES].astype(F32))
    return jnp.concatenate([stage_ref[sub, half] for half in halves], axis=1)


def _mix_mlp_kernel(x_ref, ada_ref, oa_ref, ob0_ref, ob1_ref, ob2_ref, ls0_ref, ls1_ref, ls2_ref,
                    ga_ref, gb_ref, wa_ref, wb_ref, wo_ref, g_ref, w1_ref, w2_ref, xo_ref, *stage_refs):
    dils = [d for _, d in B_GROUPS]
    for sub in range(TM_MIX // CHUNK):
        tok = slice(sub * CHUNK, (sub + 1) * CHUNK)
        ya = jnp.dot(oa_ref[tok, :], wa_ref[...], preferred_element_type=F32)
        lses = [_load_class_major(r, stage_refs[2 * g], dils[g], sub)
                for g, r in enumerate((ls0_ref, ls1_ref, ls2_ref))]
        outs = [_load_class_major(r, stage_refs[2 * g + 1], dils[g], sub)
                for g, r in enumerate((ob0_ref, ob1_ref, ob2_ref))]
        top = jnp.maximum(jnp.maximum(lses[0], lses[1]), lses[2])
        es = [jnp.exp2(v - top) for v in lses]
        den = es[0] + es[1] + es[2]
        ob = (es[0] * outs[0] + es[1] * outs[1] + es[2] * outs[2]) / den
        yb = jnp.dot(ob.astype(BF16), wb_ref[...], preferred_element_type=F32)
        mixed = ga_ref[tok, :].astype(F32) * ya + gb_ref[tok, :].astype(F32) * yb
        upd = jnp.dot(mixed.astype(BF16), wo_ref[...], preferred_element_type=F32)
        xo_ref[tok, :] = x_ref[tok, :] + ada_ref[2:3, :] * upd

    x = xo_ref[...]
    h = _modulated_rms(x, g_ref[...], ada_ref[4:5, :], ada_ref[3:4, :]).astype(BF16)
    acc = jnp.zeros(x.shape, F32)
    for c in range(D_FF // FF_CHUNK):
        a = jnp.dot(h, w1_ref[:, c * FF_CHUNK:(c + 1) * FF_CHUNK], preferred_element_type=F32)
        a = jnp.square(jnp.maximum(a, 0.0)).astype(BF16)
        acc = acc + jnp.dot(a, w2_ref[c * FF_CHUNK:(c + 1) * FF_CHUNK, :], preferred_element_type=F32)
    xo_ref[...] = x + ada_ref[5:6, :] * acc


def _mix_mlp(x, ada_l, oa, obs, lss, ga, gb, wa, wb, wo, g_mlp_l, w1, w2, l):
    nb, seq, _ = x.shape
    tok = lambda w: pl.BlockSpec((None, TM_MIX, w), lambda b, s: (b, s, 0))
    wspec = lambda r, c: pl.BlockSpec((None, r, c), lambda b, s: (l, 0, 0), pipeline_mode=pl.Buffered(1))
    cls = [pl.BlockSpec((None, TM_MIX // d, d * B_W), lambda b, s: (b, s, 0)) for _, d in B_GROUPS]
    return pl.pallas_call(
        _mix_mlp_kernel,
        grid=(nb, seq // TM_MIX),
        in_specs=[tok(D_MODEL), pl.BlockSpec((None, 6, D_MODEL), lambda b, s: (b, 0, 0)), tok(A_Q_W)]
                 + cls + cls + [tok(D_MODEL)] * 2
                 + [wspec(A_Q_W, D_MODEL), wspec(B_W, D_MODEL), wspec(D_MODEL, D_MODEL),
                    pl.BlockSpec((None, 1, D_MODEL), lambda b, s: (l, 0, 0)),
                    wspec(D_MODEL, D_FF), wspec(D_FF, D_MODEL)],
        out_specs=tok(D_MODEL),
        out_shape=jax.ShapeDtypeStruct(x.shape, F32),
        scratch_shapes=[pltpu.VMEM((TM_MIX // CHUNK, B_W // V7X_LANES, CHUNK, V7X_LANES), F32)] * (2 * N_B),
        compiler_params=_cparams(("arbitrary", "arbitrary")),
        name="mix_mlp",
    )(x, ada_l, oa, *obs, *lss, ga, gb, wa, wb, wo, g_mlp_l, w1, w2)


def _rope_tables(seq):
    pos = jnp.arange(seq)
    rows = seq // GRID_W
    row_idx = jnp.broadcast_to(jnp.arange(rows)[:, None], (rows, GRID_W)).reshape(-1)
    col_idx = jnp.broadcast_to(jnp.arange(GRID_W)[None, :], (rows, GRID_W)).reshape(-1)

    def cos_sin(p, dim):
        inv = ROPE_THETA ** (-jnp.arange(0, dim, 2, dtype=F32) / dim)
        ang = inv[:, None] * p.astype(F32)[None, :]
        return jnp.cos(ang), jnp.sin(ang)

    rc, rs = cos_sin(row_idx, HEAD_DIM // 2)
    cc, cs = cos_sin(col_idx, HEAD_DIM // 2)
    sc, ss = cos_sin(pos, HEAD_DIM)
    return jnp.concatenate([rc, rc, cc, cc, -rs, rs, -cs, cs, sc, sc, -ss, ss], axis=0)


def _gain_columns(q_norm_a, k_norm_a, q_norm_b, k_norm_b):
    scale = HEAD_DIM ** -0.5 * LOG2_E

    def pair(gain, perm, mult):
        return [gain * mult, gain[:, perm] * mult]

    cols = pair(q_norm_a, PERM_AXIAL, scale) + pair(k_norm_a, PERM_AXIAL, 1.0)
    for g in range(N_B):
        cols += pair(q_norm_b[:, g], PERM_SEQ, scale) + pair(k_norm_b[:, g], PERM_SEQ, 1.0)
    cols = jnp.concatenate(cols, axis=1)
    return jnp.broadcast_to(cols[:, :, None], cols.shape + (V7X_LANES,))


def _wt_kernel(w_ref, o_ref):
    o_ref[...] = w_ref[...].T.astype(BF16)


def _wt_col_block(j):
    return j + (j >= 5).astype(jnp.int32) + (j >= 7).astype(jnp.int32)


def _split_w_in(w_in):
    depth = w_in.shape[0]
    sizes = [A_Q_W, A_KV_W, A_KV_W] + [B_W] * (3 * N_B) + [D_MODEL, D_MODEL]
    offs = np.concatenate([[0], np.cumsum(sizes)])
    names = ["qa", "ka", "va"] + [f"{t}b{g}" for g in range(N_B) for t in "qkv"] + ["ga", "gb"]
    parts = {n: w_in[:, :, offs[i]:offs[i + 1]] for i, n in enumerate(names)}
    wn = jnp.concatenate([parts["vb0"], parts["vb1"], parts["vb2"], parts["ga"], parts["gb"]], axis=-1)
    wt = pl.pallas_call(
        _wt_kernel,
        grid=(depth, T_ROWS // B_W),
        in_specs=[pl.BlockSpec((None, D_MODEL, B_W), lambda l, j: (l, 0, _wt_col_block(j)))],
        out_specs=pl.BlockSpec((None, B_W, D_MODEL), lambda l, j: (l, j, 0)),
        out_shape=jax.ShapeDtypeStruct((depth, T_ROWS, D_MODEL), BF16),
        compiler_params=_cparams(("arbitrary", "arbitrary")),
        name="w_in_t",
    )(w_in)
    return wn.astype(BF16), wt


def kernel(x, c, w_ada, b_ada, g_mix, g_mlp, w_in, q_norm_a, k_norm_a, q_norm_b, k_norm_b,
           w_branch_a, w_branch_b, w_out, w_ff1, w_ff2):
    nb, seq, _ = x.shape
    depth = w_in.shape[0]
    assert seq % GRID_W == 0 and seq % TM_MIX == 0 and seq % TM_IN == 0

    ada = _ada_all(c, w_ada, b_ada).reshape(depth, nb, 6, D_MODEL)
    rope = _rope_tables(seq)
    gains = _gain_columns(q_norm_a, k_norm_a, q_norm_b, k_norm_b)
    wn, wt = _split_w_in(w_in)
    wa, wb, wo = w_branch_a.astype(BF16), w_branch_b.astype(BF16), w_out.astype(BF16)
    w1, w2 = w_ff1.astype(BF16), w_ff2.astype(BF16)
    g_mix = g_mix.reshape(depth, 1, D_MODEL)
    g_mlp = g_mlp.reshape(depth, 1, D_MODEL)

    for l in range(depth):
        (qat, ka, vat, qb0, kb0, vb0, qb1, kb1, vb1, qb2, kb2, vb2, ga, gb) = _in_proj(
            x, ada[l], g_mix, wn, wt, rope, gains, l)
        oa = _attn_a(qat, ka, vat)
        obs, lss = [], []
        for (window, dil), q, k, v in zip(B_GROUPS, (qb0, qb1, qb2), (kb0, kb1, kb2), (vb0, vb1, vb2)):
            o, lse = _attn_b(q, k, v, window, dil)
            obs.append(o)
            lss.append(lse)
        x = _mix_mlp(x, ada[l], oa, obs, lss, ga, gb, wa, wb, wo, g_mlp, w1, w2, l)
    return x
```
